```python
import jax
import jax.numpy as jnp
from jax import lax
import numpy as np

D_MODEL = 4096
BATCH = 1
SEQ = 8192
DEPTH = 2

GRID_W = 64
CTX_LEN = 256
N_EVEN = (DEPTH + 1) // 2
N_ODD = DEPTH // 2
N_MOD = 6
EPS = 1e-6

RWKV_WIDTH = D_MODEL // 2
RWKV_HEAD = 64
RWKV_HEADS = RWKV_WIDTH // RWKV_HEAD
DECAY_LORA = 96
ICLR_LORA = 96
GATE_LORA = 256
GN_EPS = 64e-5
CONV_WIDTH = D_MODEL - RWKV_WIDTH
CONV_K = 3
EVEN_SPLIT = (RWKV_WIDTH, RWKV_WIDTH, RWKV_WIDTH, DECAY_LORA, DECAY_LORA, ICLR_LORA, ICLR_LORA, GATE_LORA, CONV_WIDTH, CONV_WIDTH, CONV_WIDTH)
EVEN_IN = sum(EVEN_SPLIT)

FOURIER_WIDTH = D_MODEL // 2
FOURIER_GROUP = 128
FOURIER_GROUPS = FOURIER_WIDTH // FOURIER_GROUP
MLA_HEADS = 16
QK_NOPE = 128
QK_ROPE = 64
V_HEAD = 128
Q_LORA = 1024
KV_LORA = 512
ROPE_PAIRS = QK_ROPE // 4
ROPE_BASE = 10000.0
Q_BLOCK = 128
SM_SCALE = (QK_NOPE + QK_ROPE) ** -0.5
ODD_SPLIT = (FOURIER_WIDTH, Q_LORA, KV_LORA, QK_ROPE)
ODD_IN = sum(ODD_SPLIT)

N_EXPERTS = 16
N_GROUPS = 4
EXPERTS_PER_GROUP = N_EXPERTS // N_GROUPS
TOP_K = 2
D_EXPERT = 1024
MOE_BLOCK = 256

kernel_name = 'hybrid_rwkv7_shortconv_fourier_mla_moe_dit'


def split_cols(z, sizes):
    return jnp.split(z, [int(i) for i in np.cumsum(sizes)[:-1]], axis=-1)


def rmsnorm(x, g):
    xf = x.astype(jnp.float32)
    y = xf * lax.rsqrt(jnp.mean(xf * xf, axis=-1, keepdims=True) + EPS)
    return (y * g.astype(jnp.float32)).astype(x.dtype)


def adaln(cond, w, b):
    m = jax.nn.silu(cond) @ w + b
    return jnp.split(m[..., None, :], N_MOD, axis=-1)


def modulate(x, g, shift, scale):
    return rmsnorm(x, g) * (1 + scale) + shift


def axial_rope_tables(n_tokens):
    rows = n_tokens // GRID_W
    row = jnp.repeat(jnp.arange(rows), GRID_W)
    col = jnp.tile(jnp.arange(GRID_W), rows)
    pos = jnp.stack([row, col], axis=-1).astype(jnp.float32)
    inv_freq = ROPE_BASE ** (-jnp.arange(ROPE_PAIRS, dtype=jnp.float32) / ROPE_PAIRS)
    ang = pos[:, :, None, None] * inv_freq
    shape = (n_tokens, 2, 2, ROPE_PAIRS)
    cos = jnp.broadcast_to(jnp.cos(ang), shape).reshape(n_tokens, QK_ROPE)
    sin = jnp.broadcast_to(jnp.sin(ang), shape).reshape(n_tokens, QK_ROPE)
    return cos, sin


def apply_rope(x, cos, sin):
    xr = x.reshape(*x.shape[:-1], 2, 2, ROPE_PAIRS)
    rot = jnp.stack([-xr[..., 1, :], xr[..., 0, :]], axis=-2).reshape(x.shape)
    return (x * cos + rot * sin).astype(x.dtype)


def wkv_scan(s0, r, w, k, v, a, b, reverse):
    seqs = tuple(jnp.moveaxis(t.astype(jnp.float32), 1, 0) for t in (r, w, k, v, a, b))

    def step(S, inp):
        r_t, w_t, k_t, v_t, a_t, b_t = inp
        sa = jnp.einsum('bhij,bhj->bhi', S, a_t)
        S = S * w_t[:, :, None, :] + sa[..., :, None] * b_t[:, :, None, :] + v_t[..., :, None] * k_t[:, :, None, :]
        return S, jnp.einsum('bhij,bhj->bhi', S, r_t)

    s_T, ys = lax.scan(step, s0, seqs, reverse=reverse)
    return s_T, jnp.moveaxis(ys, 0, 1)


def head_norm(y, g):
    mu = jnp.mean(y, axis=-1, keepdims=True)
    yc = y - mu
    yn = yc * lax.rsqrt(jnp.mean(yc * yc, axis=-1, keepdims=True) + GN_EPS)
    return yn.reshape(*y.shape[:2], -1) * g.astype(jnp.float32)


def rwkv_time_mix(r, k, v, xw_f, xw_b, xa_f, xa_b, xg, w0, w2, a0, a2, g2, k_k, k_a, r_k, ln_x, s_f0, s_b0):
    B, T, _ = r.shape

    def heads(t):
        return t.reshape(B, T, RWKV_HEADS, RWKV_HEAD)

    kk = heads((k * k_k).astype(jnp.float32))
    kk = kk * lax.rsqrt(jnp.sum(kk * kk, axis=-1, keepdims=True) + 1e-12)
    r_h, v_h = heads(r), heads(v)
    wkv, bonus, finals = 0.0, 0.0, []
    for d, (xw, xa, s0, rev) in enumerate(((xw_f, xa_f, s_f0, False), (xw_b, xa_b, s_b0, True))):
        w_log = -jax.nn.softplus(-(w0[d] + jnp.tanh(xw) @ w2[d]).astype(jnp.float32)) - 0.5
        decay = jnp.exp(-jnp.exp(w_log))
        iclr = jax.nn.sigmoid(a0[d] + xa @ a2[d])
        k_d = heads(k * (1 + (iclr - 1) * k_a))
        s_T, y = wkv_scan(s0, r_h, heads(decay), k_d, v_h, -kk, kk * heads(iclr), rev)
        wkv = wkv + y
        bonus = bonus + jnp.sum(r_h * k_d * r_k, axis=-1, keepdims=True) * v_h
        finals.append(s_T)
    o = (head_norm(wkv, ln_x) + bonus.reshape(B, T, RWKV_WIDTH)) * (jax.nn.sigmoid(xg) @ g2)
    return o.astype(r.dtype), finals[0], finals[1]


def short_conv(u, w):
    pad = (CONV_K - 1) // 2
    return lax.conv_general_dilated(u, w[:, None, :].astype(u.dtype), window_strides=(1,), padding=[(pad, pad)], dimension_numbers=('NWC', 'WIO', 'NWC'), feature_group_count=u.shape[-1])


def even_mixer(h_lat, h_ctx, w_in, w_out, w0, w2, a0, a2, g2, k_k, k_a, r_k, ln_x, conv_w, keep_ctx):
    def head_groups(h, s_f0, s_b0):
        r, k, v, xw_f, xw_b, xa_f, xa_b, xg, gate_b, gate_c, u = split_cols(h @ w_in, EVEN_SPLIT)
        o_rwkv, s_f, s_b = rwkv_time_mix(r, k, v, xw_f, xw_b, xa_f, xa_b, xg, w0, w2, a0, a2, g2, k_k, k_a, r_k, ln_x, s_f0, s_b0)
        o_conv = gate_b * short_conv(gate_c * u, conv_w)
        return jnp.concatenate([o_rwkv, o_conv], axis=-1), s_f, s_b

    zero = jnp.zeros((h_ctx.shape[0], RWKV_HEADS, RWKV_HEAD, RWKV_HEAD), jnp.float32)
    y_ctx, s_f, s_b = head_groups(h_ctx, zero, zero)
    y_lat, _, _ = head_groups(h_lat, s_f, s_b)
    return y_lat @ w_out, (y_ctx @ w_out if keep_ctx else None)


def fourier_mix(u):
    B, T, _ = u.shape
    ug = jnp.transpose(u.astype(jnp.float32).reshape(B, T, FOURIER_GROUPS, FOURIER_GROUP), (0, 2, 1, 3))
    f = jnp.fft.fft2(ug, norm='ortho').real
    return jnp.transpose(f, (0, 2, 1, 3)).reshape(B, T, FOURIER_WIDTH).astype(u.dtype)


def block_attention(q, k, v):
    B, Tq, H, Dk = q.shape
    nb = Tq // Q_BLOCK
    qb = jnp.moveaxis(q.reshape(B, nb, Q_BLOCK, H, Dk), 1, 0)
    kf = k.astype(jnp.float32)

    def one(q_blk):
        s = jnp.einsum('bqhd,bkhd->bhqk', q_blk.astype(jnp.float32), kf) * SM_SCALE
        p = jax.nn.softmax(s, axis=-1).astype(v.dtype)
        return jnp.einsum('bhqk,bkhd->bqhd', p, v)

    o = lax.map(one, qb)
    return jnp.moveaxis(o, 0, 1).reshape(B, Tq, H * v.shape[-1])


def mla_kv(kva, k_rope, kv_norm, w_ukv, cos, sin):
    B, T, _ = kva.shape
    kv = (rmsnorm(kva, kv_norm) @ w_ukv).reshape(B, T, MLA_HEADS, QK_NOPE + V_HEAD)
    k_nope, v = kv[..., :QK_NOPE], kv[..., QK_NOPE:]
    if cos is not None:
        k_rope = apply_rope(k_rope, cos, sin)
    k_rope = jnp.broadcast_to(k_rope[:, :, None, :], (B, T, MLA_HEADS, QK_ROPE))
    return jnp.concatenate([k_nope, k_rope], axis=-1), v


def mla_q(qa, q_norm, w_uq, cos, sin):
    B, T, _ = qa.shape
    q = (rmsnorm(qa, q_norm) @ w_uq).reshape(B, T, MLA_HEADS, QK_NOPE + QK_ROPE)
    if cos is None:
        return q
    q_rope = apply_rope(q[..., QK_NOPE:], cos[:, None, :], sin[:, None, :])
    return jnp.concatenate([q[..., :QK_NOPE], q_rope], axis=-1)


def odd_mixer(h_lat, h_ctx, w_in, w_out, q_norm, w_uq, kv_norm, w_ukv, cos, sin, keep_ctx):
    u_l, qa_l, kva_l, kr_l = split_cols(h_lat @ w_in, ODD_SPLIT)
    u_c, qa_c, kva_c, kr_c = split_cols(h_ctx @ w_in, ODD_SPLIT)
    k_c, v_c = mla_kv(kva_c, kr_c, kv_norm, w_ukv, None, None)
    k_l, v_l = mla_kv(kva_l, kr_l, kv_norm, w_ukv, cos, sin)
    q_l = mla_q(qa_l, q_norm, w_uq, cos, sin)
    att_l = block_attention(q_l, jnp.concatenate([k_l, k_c], axis=1), jnp.concatenate([v_l, v_c], axis=1))
    o_lat = jnp.concatenate([fourier_mix(u_l), att_l], axis=-1) @ w_out
    if not keep_ctx:
        return o_lat, None
    att_c = block_attention(mla_q(qa_c, q_norm, w_uq, None, None), k_c, v_c)
    return o_lat, jnp.concatenate([fourier_mix(u_c), att_c], axis=-1) @ w_out


def route(x, router_w, router_b):
    T = x.shape[0]
    s = jax.nn.sigmoid(x.astype(jnp.float32) @ router_w.astype(jnp.float32))
    sel = (s + router_b.astype(jnp.float32)).reshape(T, N_GROUPS, EXPERTS_PER_GROUP)
    grp = jnp.argmax(lax.top_k(sel, 2)[0].sum(-1), axis=-1)
    _, loc = lax.top_k(sel[jnp.arange(T), grp], TOP_K)
    idx = grp[:, None] * EXPERTS_PER_GROUP + loc
    wts = jnp.take_along_axis(s, idx, axis=1)
    return idx, wts / jnp.sum(wts, axis=-1, keepdims=True)


def swiglu(x, wg, wu, wd):
    return (jax.nn.silu(x @ wg) * (x @ wu)) @ wd


def moe_ffn(x, router_w, router_b, wg, wu, wd, sg, su, sd):
    T = x.shape[0]
    idx, wts = route(x, router_w, router_b)
    A = T * TOP_K
    flat_e = idx.reshape(A)
    order = jnp.argsort(flat_e)
    e_sorted = flat_e[order]
    tok_sorted = (order // TOP_K).astype(jnp.int32)
    w_sorted = wts.reshape(A)[order]
    counts = jnp.bincount(flat_e, length=N_EXPERTS)
    padded = (counts + MOE_BLOCK - 1) // MOE_BLOCK * MOE_BLOCK
    pad_end = jnp.cumsum(padded)
    pad_start = pad_end - padded
    start = jnp.cumsum(counts) - counts
    dest = pad_start[e_sorted] + jnp.arange(A) - start[e_sorted]
    n_blocks = -(-A // MOE_BLOCK) + N_EXPERTS
    slot_tok = jnp.full((n_blocks * MOE_BLOCK,), T, jnp.int32).at[dest].set(tok_sorted)
    x_pad = jnp.concatenate([x, jnp.zeros((1, x.shape[1]), x.dtype)], axis=0)
    xb = x_pad[slot_tok].reshape(n_blocks, MOE_BLOCK, x.shape[1])
    block_e = jnp.minimum(jnp.searchsorted(pad_end, jnp.arange(n_blocks) * MOE_BLOCK, side='right'), N_EXPERTS - 1)
    yb = lax.map(lambda a: swiglu(a[0], wg[a[1]], wu[a[1]], wd[a[1]]), (xb, block_e))
    y_slots = yb.reshape(-1, x.shape[1])
    routed = jnp.zeros_like(x).at[tok_sorted].add((y_slots[dest] * w_sorted[:, None]).astype(x.dtype))
    return routed + swiglu(x, sg, su, sd)


def setup_inputs(seed: int = 0) -> dict:
    key = jax.random.key(seed)
    ks = iter(jax.random.split(key, 40))

    def nrm(shape, scale):
        return jax.random.normal(next(ks), shape, jnp.float32) * scale

    def gain(shape):
        return 1.0 + nrm(shape, 0.02)

    D = D_MODEL
    return {
        'x': nrm((BATCH, SEQ, D), 1.0),
        'c': nrm((BATCH, D), 1.0),
        'ctx': nrm((BATCH, CTX_LEN, D), 1.0),
        'c_ctx': nrm((D,), 1.0),
        'ada_w': nrm((DEPTH, D, N_MOD * D), 0.5 * D ** -0.5),
        'ada_b': nrm((DEPTH, N_MOD * D), 0.01),
        'norm1_g': gain((DEPTH, D)),
        'norm2_g': gain((DEPTH, D)),
        'ev_w_in': nrm((N_EVEN, D, EVEN_IN), D ** -0.5),
        'ev_w_out': nrm((N_EVEN, D, D), D ** -0.5),
        'ev_w0': -3.0 + nrm((N_EVEN, 2, RWKV_WIDTH), 1.5),
        'ev_w2': nrm((N_EVEN, 2, DECAY_LORA, RWKV_WIDTH), 0.1 * DECAY_LORA ** -0.5),
        'ev_a0': nrm((N_EVEN, 2, RWKV_WIDTH), 0.5),
        'ev_a2': nrm((N_EVEN, 2, ICLR_LORA, RWKV_WIDTH), 0.1 * ICLR_LORA ** -0.5),
        'ev_g2': nrm((N_EVEN, GATE_LORA, RWKV_WIDTH), GATE_LORA ** -0.5),
        'ev_k_k': 0.85 + nrm((N_EVEN, RWKV_WIDTH), 0.05),
        'ev_k_a': 1.0 + nrm((N_EVEN, RWKV_WIDTH), 0.05),
        'ev_r_k': nrm((N_EVEN, RWKV_HEADS, RWKV_HEAD), 0.1),
        'ev_ln_x': gain((N_EVEN, RWKV_WIDTH)),
        'ev_conv_w': nrm((N_EVEN, CONV_K, CONV_WIDTH), CONV_K ** -0.5),
        'od_w_in': nrm((N_ODD, D, ODD_IN), D ** -0.5),
        'od_w_out': nrm((N_ODD, D, D), D ** -0.5),
        'od_q_norm': gain((N_ODD, Q_LORA)),
        'od_w_uq': nrm((N_ODD, Q_LORA, MLA_HEADS * (QK_NOPE + QK_ROPE)), Q_LORA ** -0.5),
        'od_kv_norm': gain((N_ODD, KV_LORA)),
        'od_w_ukv': nrm((N_ODD, KV_LORA, MLA_HEADS * (QK_NOPE + V_HEAD)), KV_LORA ** -0.5),
        'router_w': nrm((D, N_EXPERTS), D ** -0.5),
        'router_b': nrm((N_EXPERTS,), 0.01),
        'moe_wg': nrm((DEPTH, N_EXPERTS, D, D_EXPERT), D ** -0.5),
        'moe_wu': nrm((DEPTH, N_EXPERTS, D, D_EXPERT), D ** -0.5),
        'moe_wd': nrm((DEPTH, N_EXPERTS, D_EXPERT, D), D_EXPERT ** -0.5),
        'shared_wg': nrm((DEPTH, D, D_EXPERT), D ** -0.5),
        'shared_wu': nrm((DEPTH, D, D_EXPERT), D ** -0.5),
        'shared_wd': nrm((DEPTH, D_EXPERT, D), D_EXPERT ** -0.5),
        'final_g': gain((D,)),
    }


def reference(x, c, ctx, c_ctx, ada_w, ada_b, norm1_g, norm2_g, ev_w_in, ev_w_out, ev_w0, ev_w2, ev_a0, ev_a2, ev_g2, ev_k_k, ev_k_a, ev_r_k, ev_ln_x, ev_conv_w, od_w_in, od_w_out, od_q_norm, od_w_uq, od_kv_norm, od_w_ukv, router_w, router_b, moe_wg, moe_wu, moe_wd, shared_wg, shared_wu, shared_wd, final_g):
    cos, sin = axial_rope_tables(x.shape[1])
    xl, xc = x, ctx
    for layer in range(DEPTH):
        keep_ctx = layer < DEPTH - 1
        j = layer // 2
        sh1, sc1, g1, sh2, sc2, g2 = adaln(c, ada_w[layer], ada_b[layer])
        cmod = adaln(c_ctx[None], ada_w[layer], ada_b[layer])
        hl = modulate(xl, norm1_g[layer], sh1, sc1)
        hc = modulate(xc, norm1_g[layer], cmod[0], cmod[1])
        if layer % 2 == 0:
            ol, oc = even_mixer(hl, hc, ev_w_in[j], ev_w_out[j], ev_w0[j], ev_w2[j], ev_a0[j], ev_a2[j], ev_g2[j], ev_k_k[j], ev_k_a[j], ev_r_k[j], ev_ln_x[j], ev_conv_w[j], keep_ctx)
        else:
            ol, oc = odd_mixer(hl, hc, od_w_in[j], od_w_out[j], od_q_norm[j], od_w_uq[j], od_kv_norm[j], od_w_ukv[j], cos, sin, keep_ctx)
        xl = xl + g1 * ol
        hl = modulate(xl, norm2_g[layer], sh2, sc2)
        moe_args = (router_w, router_b, moe_wg[layer], moe_wu[layer], moe_wd[layer], shared_wg[layer], shared_wu[layer], shared_wd[layer])
        if keep_ctx:
            xc = xc + cmod[2] * oc
            hc = modulate(xc, norm2_g[layer], cmod[3], cmod[4])
            n_ctx = hc.shape[0] * hc.shape[1]
            y = moe_ffn(jnp.concatenate([hc.reshape(-1, D_MODEL), hl.reshape(-1, D_MODEL)], axis=0), *moe_args)
            xc = xc + cmod[5] * y[:n_ctx].reshape(hc.shape)
            xl = xl + g2 * y[n_ctx:].reshape(hl.shape)
        else:
            xl = xl + g2 * moe_ffn(hl.reshape(-1, D_MODEL), *moe_args).reshape(hl.shape)
    return rmsnorm(xl, final_g)
```

```python
import functools
import math

import numpy as np
import jax
import jax.numpy as jnp
from jax import lax
from jax.experimental import pallas as pl
from jax.experimental.pallas import tpu as pltpu

F32 = jnp.float32
BF16 = jnp.bfloat16
HIGHEST = lax.Precision.HIGHEST

LANES = 128
SUBLANES = 8
VMEM_LIMIT_BYTES = 56 * 1024 * 1024

EPS = 1e-6
GN_EPS = 64e-5
RWKV_HEAD = 64
DECAY_LORA = 96
ICLR_LORA = 96
GATE_LORA = 256
CONV_K = 3
FOURIER_GROUP = 128
MLA_HEADS = 16
QK_NOPE = 128
QK_ROPE = 64
V_HEAD = 128
Q_LORA = 1024
KV_LORA = 512
ROPE_PAIRS = QK_ROPE // 4
ROPE_BASE = 10000.0
GRID_W = 64
SM_SCALE = (QK_NOPE + QK_ROPE) ** -0.5
N_EXPERTS = 16
N_GROUPS = 4
EXPERTS_PER_GROUP = N_EXPERTS // N_GROUPS
TOP_K = 2
MOE_BLOCK = 256
ROW_BLOCK = 256
SCAN_CHUNK = 64


def _params(*sem):
    return pltpu.CompilerParams(dimension_semantics=sem, vmem_limit_bytes=VMEM_LIMIT_BYTES)


def _pick(n, candidates):
    for c in candidates:
        if n % c == 0:
            return c
    raise ValueError(f"no tile for {n} among {candidates}")


def _mm_body(*refs, nk, a_act, epi, n_rows, n_fulls, tm):
    a_ref, b_ref = refs[0], refs[1]
    row_refs = refs[2:2 + n_rows]
    full_refs = refs[2 + n_rows:2 + n_rows + n_fulls]
    o_ref = refs[2 + n_rows + n_fulls]
    acc_ref = refs[3 + n_rows + n_fulls]
    k = pl.program_id(2)
    av = a_ref[...]
    if a_act is not None:
        av = a_act(av.astype(F32))
    part = jnp.dot(av.astype(BF16), b_ref[...].astype(BF16), preferred_element_type=F32)

    @pl.when(k == 0)
    def _():
        acc_ref[...] = part

    @pl.when(k > 0)
    def _():
        acc_ref[...] += part

    @pl.when(k == nk - 1)
    def _():
        acc = acc_ref[...]
        if epi is not None:
            row0 = pl.program_id(0) * tm
            acc = epi(acc, row0, [r[...] for r in row_refs], [f[...] for f in full_refs])
        o_ref[...] = acc.astype(o_ref.dtype)


def _matmul(a, b, *, tm, tn, tk, out_dtype=F32, a_col_off=0, a_row_off=0, m=None, a_act=None,
            rows=(), fulls=(), fulls_row_off=0, epi=None):
    K, N = b.shape
    M = a.shape[0] - a_row_off if m is None else m
    assert M % tm == 0 and N % tn == 0 and K % tk == 0
    assert a_col_off % tk == 0 and a_row_off % tm == 0 and fulls_row_off % tm == 0
    nk = K // tk
    ko, ro, fo = a_col_off // tk, a_row_off // tm, fulls_row_off // tm
    in_specs = [
        pl.BlockSpec((tm, tk), lambda i, j, k: (i + ro, k + ko)),
        pl.BlockSpec((tk, tn), lambda i, j, k: (k, j)),
    ]
    for r in rows:
        in_specs.append(pl.BlockSpec((r.shape[0], tn), lambda i, j, k: (0, j)))
    for _ in fulls:
        in_specs.append(pl.BlockSpec((tm, tn), lambda i, j, k: (i + fo, j)))
    body = functools.partial(_mm_body, nk=nk, a_act=a_act, epi=epi, n_rows=len(rows),
                             n_fulls=len(fulls), tm=tm)
    return pl.pallas_call(
        body,
        grid=(M // tm, N // tn, nk),
        in_specs=in_specs,
        out_specs=pl.BlockSpec((tm, tn), lambda i, j, k: (i, j)),
        out_shape=jax.ShapeDtypeStruct((M, N), out_dtype),
        scratch_shapes=[pltpu.VMEM((tm, tn), F32)],
        compiler_params=_params("parallel", "parallel", "arbitrary"),
    )(a, b, *rows, *fulls)


def _modulate_body(x_ref, g_ref, mod_ref, *rest, shift_idx, scale_idx, with_router):
    xv = x_ref[...]
    y = xv * lax.rsqrt(jnp.mean(xv * xv, axis=-1, keepdims=True) + EPS) * g_ref[...]
    h = y * (1.0 + mod_ref[0, scale_idx:scale_idx + 1, :]) + mod_ref[0, shift_idx:shift_idx + 1, :]
    if with_router:
        rw_ref, o_ref, lg_ref = rest
        lg_ref[...] = jnp.dot(h, rw_ref[...], precision=HIGHEST, preferred_element_type=F32)
    else:
        (o_ref,) = rest
    o_ref[...] = h.astype(o_ref.dtype)


def _modulate(x, g, mods, shift_idx, scale_idx, n_ctx, router_w=None):
    T, D = x.shape
    tm = ROW_BLOCK
    assert T % tm == 0 and n_ctx % tm == 0
    nc = n_ctx // tm
    in_specs = [
        pl.BlockSpec((tm, D), lambda i: (i, 0)),
        pl.BlockSpec((1, D), lambda i: (0, 0)),
        pl.BlockSpec((1, mods.shape[1], D), lambda i: (jnp.where(i < nc, 0, 1), 0, 0)),
    ]
    out_specs = [pl.BlockSpec((tm, D), lambda i: (i, 0))]
    out_shape = [jax.ShapeDtypeStruct((T, D), BF16)]
    args = [x, g.reshape(1, D), mods]
    if router_w is not None:
        in_specs.append(pl.BlockSpec((D, LANES), lambda i: (0, 0)))
        out_specs.append(pl.BlockSpec((tm, LANES), lambda i: (i, 0)))
        out_shape.append(jax.ShapeDtypeStruct((T, LANES), F32))
        args.append(router_w)
    body = functools.partial(_modulate_body, shift_idx=shift_idx, scale_idx=scale_idx,
                             with_router=router_w is not None)
    outs = pl.pallas_call(body, grid=(T // tm,), in_specs=in_specs, out_specs=out_specs,
                          out_shape=out_shape, compiler_params=_params("parallel"))(*args)
    return outs if router_w is not None else outs[0]


def _rmsnorm_cols_body(x_ref, g_ref, o_ref):
    xv = x_ref[...]
    y = xv * lax.rsqrt(jnp.mean(xv * xv, axis=-1, keepdims=True) + EPS) * g_ref[...]
    o_ref[...] = y.astype(o_ref.dtype)


def _rmsnorm_cols(z, col_off, width, g, out_dtype=BF16):
    T = z.shape[0]
    tm = ROW_BLOCK
    assert col_off % width == 0 and T % tm == 0
    cb = col_off // width
    return pl.pallas_call(
        _rmsnorm_cols_body,
        grid=(T // tm,),
        in_specs=[pl.BlockSpec((tm, width), lambda i: (i, cb)), pl.BlockSpec((1, width), lambda i: (0, 0))],
        out_specs=pl.BlockSpec((tm, width), lambda i: (i, 0)),
        out_shape=jax.ShapeDtypeStruct((T, width), out_dtype),
        compiler_params=_params("parallel"),
    )(z, g.reshape(1, width))


def _combine_body(x_ref, a_ref, b_ref, gate_ref, g_ref, o_ref, *, final_norm):
    y = x_ref[...] + gate_ref[0] * (a_ref[...] + b_ref[...])
    if final_norm:
        y = y * lax.rsqrt(jnp.mean(y * y, axis=-1, keepdims=True) + EPS) * g_ref[...]
    o_ref[...] = y


def _combine(x, routed, shared, gates, n_ctx, final_g=None):
    T, D = x.shape
    tm = ROW_BLOCK
    nc = n_ctx // tm
    g = jnp.ones((1, D), F32) if final_g is None else final_g.reshape(1, D)
    blk = pl.BlockSpec((tm, D), lambda i: (i, 0))
    return pl.pallas_call(
        functools.partial(_combine_body, final_norm=final_g is not None),
        grid=(T // tm,),
        in_specs=[blk, blk, blk,
                  pl.BlockSpec((1, 1, D), lambda i: (jnp.where(i < nc, 0, 1), 0, 0)),
                  pl.BlockSpec((1, D), lambda i: (0, 0))],
        out_specs=blk,
        out_shape=jax.ShapeDtypeStruct((T, D), F32),
        compiler_params=_params("parallel"),
    )(x, routed, shared, gates, g)


def _conv_body(gb_ref, gc_ref, u_ref, gcp_ref, up_ref, gcn_ref, un_ref, w_ref, o_ref, *, tb, nb, nc):
    i = pl.program_id(0)
    p = gc_ref[...] * u_ref[...]
    prev_row = (gcp_ref[...] * up_ref[...])[SUBLANES - 1:SUBLANES, :]
    next_row = (gcn_ref[...] * un_ref[...])[0:1, :]
    starts = jnp.logical_or(i == 0, i == nc)
    ends = jnp.logical_or(i == nc - 1, i == nb - 1)
    prev_row = jnp.where(starts, 0.0, prev_row)
    next_row = jnp.where(ends, 0.0, next_row)
    rid = lax.broadcasted_iota(jnp.int32, p.shape, 0)
    xm1 = jnp.where(rid == 0, prev_row, pltpu.roll(p, 1, axis=0))
    xp1 = jnp.where(rid == tb - 1, next_row, pltpu.roll(p, tb - 1, axis=0))
    w = w_ref[...]
    o_ref[...] = (gb_ref[...] * (w[0:1] * xm1 + w[1:2] * p + w[2:3] * xp1)).astype(o_ref.dtype)


def _short_conv(z, gb_off, gc_off, u_off, conv_w, n_ctx):
    T = z.shape[0]
    C = conv_w.shape[1]
    tb, tn = ROW_BLOCK, 512
    nb, nc = T // tb, n_ctx // tb
    hb = tb // SUBLANES
    last_h = T // SUBLANES - 1
    cur = lambda off: pl.BlockSpec((tb, tn), lambda i, j: (i, off // tn + j))
    prv = lambda off: pl.BlockSpec((SUBLANES, tn), lambda i, j: (jnp.maximum(i * hb - 1, 0), off // tn + j))
    nxt = lambda off: pl.BlockSpec((SUBLANES, tn), lambda i, j: (jnp.minimum((i + 1) * hb, last_h), off // tn + j))
    return pl.pallas_call(
        functools.partial(_conv_body, tb=tb, nb=nb, nc=nc),
        grid=(nb, C // tn),
        in_specs=[cur(gb_off), cur(gc_off), cur(u_off), prv(gc_off), prv(u_off), nxt(gc_off), nxt(u_off),
                  pl.BlockSpec((CONV_K, tn), lambda i, j: (0, j))],
        out_specs=pl.BlockSpec((tb, tn), lambda i, j: (i, j)),
        out_shape=jax.ShapeDtypeStruct((T, C), BF16),
        compiler_params=_params("parallel", "parallel"),
    )(z, z, z, z, z, z, z, conv_w)


def _dot_hi(a, b):
    return jnp.dot(a, b, precision=HIGHEST, preferred_element_type=F32)


def _dot_nt_hi(a, b):
    return lax.dot_general(a, b, (((1,), (1,)), ((), ())), precision=HIGHEST, preferred_element_type=F32)


def _dot_tn_hi(a, b):
    return lax.dot_general(a, b, (((0,), (0,)), ((), ())), precision=HIGHEST, preferred_element_type=F32)


def _scan_chunk(r, k, v, a, b, lw, state, reverse):
    C = r.shape[0]
    n = 2 * C
    lane = lax.broadcasted_iota(jnp.int32, (C, LANES), 1)
    in_h0 = lane < RWKV_HEAD

    def stack(x):
        return jnp.concatenate([jnp.where(in_h0, x, 0.0), jnp.where(in_h0, 0.0, x)], axis=0)

    row = lax.broadcasted_iota(jnp.int32, (n, n), 0)
    col = lax.broadcasted_iota(jnp.int32, (n, n), 1)
    same = (row // C) == (col // C)
    before = (col > row) if reverse else (col < row)
    strict = jnp.logical_and(same, before)
    incl = jnp.logical_and(same, jnp.logical_or(before, col == row))

    lw_s, r_s, k_s, v_s, a_s, b_s = (stack(t) for t in (lw, r, k, v, a, b))
    cum = _dot_hi(jnp.where(incl, 1.0, 0.0), lw_s)
    tot = jnp.sum(lw, axis=0, keepdims=True)
    a_hat = a_s * jnp.exp(cum - lw_s)
    r_hat = r_s * jnp.exp(cum)
    inv = jnp.exp(-cum)
    b_chk, k_chk = b_s * inv, k_s * inv
    fin = jnp.exp(tot - cum)
    b_til, k_til = b_s * fin, k_s * fin

    g = _dot_nt_hi(jnp.concatenate([a_hat, r_hat], axis=0), jnp.concatenate([b_chk, k_chk], axis=0))
    n_ab = jnp.where(strict, g[:n, :n], 0.0)
    m_ak = jnp.where(strict, g[:n, n:], 0.0)
    m_rb = jnp.where(incl, g[n:, :n], 0.0)
    m_rk = jnp.where(incl, g[n:, n:], 0.0)

    eye = jnp.where(row == col, 1.0, 0.0)
    blk = 8
    nd = jnp.where((row // blk) == (col // blk), n_ab, 0.0)
    t_inv = eye + nd
    pw = _dot_hi(nd, nd)
    t_inv = t_inv + _dot_hi(pw, t_inv)
    pw = _dot_hi(pw, pw)
    t_inv = t_inv + _dot_hi(pw, t_inv)
    while blk < C:
        off = jnp.where(jnp.logical_and((row // (2 * blk)) == (col // (2 * blk)), (row // blk) != (col // blk)), n_ab, 0.0)
        t_inv = t_inv + _dot_hi(t_inv, _dot_hi(off, t_inv))
        blk *= 2

    u = _dot_hi(t_inv, _dot_hi(a_hat, state) + _dot_hi(m_ak, v_s))
    y = _dot_hi(r_hat, state) + _dot_hi(m_rb, u) + _dot_hi(m_rk, v_s)
    tot_col = _dot_tn_hi(lw, jnp.ones((C, LANES), F32))
    new_state = state * jnp.exp(tot_col) + _dot_tn_hi(b_til, u) + _dot_tn_hi(k_til, v_s)
    return y[:C] + y[C:], new_state


def _scan_body(rf, kf, vf, lwf, icf, rb, kb, vb, lwb, icb, kk_ref, ka_ref, seg_ref, yf_ref, yb_ref, sf_ref, sb_ref, *, tb):
    @pl.when(pl.program_id(1) == 0)
    def _():
        sf_ref[...] = jnp.zeros_like(sf_ref)
        sb_ref[...] = jnp.zeros_like(sb_ref)

    k_k, k_a, seg = kk_ref[...], ka_ref[...], seg_ref[...]
    nch = tb // SCAN_CHUNK
    for refs, y_ref, s_ref, reverse in ((rf, kf, vf, lwf, icf), yf_ref, sf_ref, False), ((rb, kb, vb, lwb, icb), yb_ref, sb_ref, True):
        r, k, v, lw, ic = (t[...] for t in refs)
        kk = k * k_k
        kk = kk * lax.rsqrt(_dot_hi(kk * kk, seg) + 1e-12)
        a, b = -kk, kk * ic
        kd = k * (1.0 + (ic - 1.0) * k_a)
        state = s_ref[...]
        for c in (range(nch - 1, -1, -1) if reverse else range(nch)):
            sl = slice(c * SCAN_CHUNK, (c + 1) * SCAN_CHUNK)
            y, state = _scan_chunk(r[sl], kd[sl], v[sl], a[sl], b[sl], lw[sl], state, reverse)
            y_ref[sl, :] = y
        s_ref[...] = state


def _rwkv_scan(z, r_off, k_off, v_off, lw_f, ic_f, lw_b, ic_b, k_k, k_a, n_ctx):
    T = z.shape[0]
    W = lw_f.shape[1]
    tb = ROW_BLOCK
    assert n_ctx == tb and T % tb == 0
    nb = T // tb
    npair = W // LANES
    fwd = lambda i: i
    bwd = lambda i: jnp.where(i == 0, 0, nb - i)
    zspec = lambda off, o: pl.BlockSpec((tb, LANES), lambda p, i: (o(i), off // LANES + p))
    wspec = lambda o: pl.BlockSpec((tb, LANES), lambda p, i: (o(i), p))
    par = pl.BlockSpec((1, LANES), lambda p, i: (0, p))
    lane = np.arange(LANES)
    seg = jnp.asarray((lane[:, None] // RWKV_HEAD == lane[None, :] // RWKV_HEAD).astype(np.float32))
    return pl.pallas_call(
        functools.partial(_scan_body, tb=tb),
        grid=(npair, nb),
        in_specs=[zspec(r_off, fwd), zspec(k_off, fwd), zspec(v_off, fwd), wspec(fwd), wspec(fwd),
                  zspec(r_off, bwd), zspec(k_off, bwd), zspec(v_off, bwd), wspec(bwd), wspec(bwd),
                  par, par, pl.BlockSpec((LANES, LANES), lambda p, i: (0, 0))],
        out_specs=[wspec(fwd), wspec(bwd)],
        out_shape=[jax.ShapeDtypeStruct((T, W), F32)] * 2,
        scratch_shapes=[pltpu.VMEM((LANES, LANES), F32)] * 2,
        compiler_params=_params("parallel", "arbitrary"),
    )(z, z, z, lw_f, ic_f, z, z, z, lw_b, ic_b, k_k.reshape(1, W), k_a.reshape(1, W), seg)


def _rwkv_post_body(yf, yb, r, k, v, icf, icb, gate, ka, rk, lnx, seg_ref, o_ref):
    seg = seg_ref[...]
    inv_n = 1.0 / RWKV_HEAD
    wkv = yf[...] + yb[...]
    yc = wkv - _dot_hi(wkv, seg) * inv_n
    yn = yc * lax.rsqrt(_dot_hi(yc * yc, seg) * inv_n + GN_EPS) * lnx[...]
    kv, kav = k[...], ka[...]
    kd_sum = kv * (1.0 + (icf[...] - 1.0) * kav) + kv * (1.0 + (icb[...] - 1.0) * kav)
    bonus = _dot_hi(r[...] * kd_sum * rk[...], seg) * v[...]
    o_ref[...] = ((yn + bonus) * gate[...]).astype(o_ref.dtype)


def _rwkv_post(y_f, y_b, z, r_off, k_off, v_off, ic_f, ic_b, gate, k_a, r_k, ln_x):
    T, W = y_f.shape
    tb, tn = ROW_BLOCK, 256
    blk = pl.BlockSpec((tb, tn), lambda i, j: (i, j))
    zspec = lambda off: pl.BlockSpec((tb, tn), lambda i, j: (i, off // tn + j))
    par = pl.BlockSpec((1, tn), lambda i, j: (0, j))
    lane = np.arange(tn)
    seg = jnp.asarray((lane[:, None] // RWKV_HEAD == lane[None, :] // RWKV_HEAD).astype(np.float32))
    return pl.pallas_call(
        _rwkv_post_body,
        grid=(T // tb, W // tn),
        in_specs=[blk, blk, zspec(r_off), zspec(k_off), zspec(v_off), blk, blk, blk, par, par, par,
                  pl.BlockSpec((tn, tn), lambda i, j: (0, 0))],
        out_specs=blk,
        out_shape=jax.ShapeDtypeStruct((T, W), BF16),
        compiler_params=_params("parallel", "parallel"),
    )(y_f, y_b, z, z, z, ic_f, ic_b, gate, k_a.reshape(1, W), r_k.reshape(1, W), ln_x.reshape(1, W), seg)


def _dft_cols_body(u_ref, cs_ref, o_ref):
    pq = jnp.dot(u_ref[...].astype(BF16), cs_ref[...], preferred_element_type=F32)
    o_ref[0] = pq[:, :FOURIER_GROUP].astype(o_ref.dtype)
    o_ref[1] = pq[:, FOURIER_GROUP:].astype(o_ref.dtype)


def _fourier_mix(z, n_ctx, width):
    T = z.shape[0] - n_ctx
    tm = ROW_BLOCK
    ro = n_ctx // tm
    G = FOURIER_GROUP
    c = np.arange(G)
    ang_c = 2.0 * np.pi * ((c[:, None] * c[None, :]) % G) / G
    cs = jnp.asarray(np.concatenate([np.cos(ang_c), np.sin(ang_c)], axis=1), BF16)
    pq = pl.pallas_call(
        _dft_cols_body,
        grid=(T // tm, width // G),
        in_specs=[pl.BlockSpec((tm, G), lambda i, g: (i + ro, g)), pl.BlockSpec((G, 2 * G), lambda i, g: (0, 0))],
        out_specs=pl.BlockSpec((2, tm, G), lambda i, g: (0, i, g)),
        out_shape=jax.ShapeDtypeStruct((2, T, width), BF16),
        compiler_params=_params("parallel", "parallel"),
    )(z, cs)
    t = jnp.arange(T, dtype=jnp.int32)
    ang = ((t[:, None] * t[None, :]) % T).astype(F32) * (2.0 * math.pi / T)
    dft = jnp.concatenate([jnp.cos(ang), -jnp.sin(ang)], axis=1).astype(BF16)
    scale = 1.0 / math.sqrt(T * G)
    tmm = _pick(T, (1024, 512, 256))
    return _matmul(dft, pq.reshape(2 * T, width), tm=tmm, tn=_pick(width, (1024, 512)), tk=_pick(2 * T, (1024, 512)),
                   out_dtype=BF16, epi=lambda acc, row0, rows, fulls: acc * scale)


def _rope_body(x_ref, tab_ref, o_ref, *, scale, keep_dup):
    xt = x_ref[...] * tab_ref[...]
    y = xt + pltpu.roll(xt, QK_ROPE, axis=1)
    if not keep_dup:
        lane = lax.broadcasted_iota(jnp.int32, y.shape, 1)
        y = jnp.where(lane < QK_ROPE, y, 0.0)
    o_ref[...] = (y * scale).astype(o_ref.dtype)


def _rope(z, col_off, n_heads, tab, scale):
    T = z.shape[0]
    tm = ROW_BLOCK
    cb = col_off // LANES
    return pl.pallas_call(
        functools.partial(_rope_body, scale=scale, keep_dup=False),
        grid=(T // tm, n_heads),
        in_specs=[pl.BlockSpec((tm, LANES), lambda i, h: (i, cb + h)), pl.BlockSpec((tm, LANES), lambda i, h: (i, 0))],
        out_specs=pl.BlockSpec((tm, LANES), lambda i, h: (i, h)),
        out_shape=jax.ShapeDtypeStruct((T, n_heads * LANES), BF16),
        compiler_params=_params("parallel", "parallel"),
    )(z, tab)


def _scale_cols_body(x_ref, o_ref, *, scale):
    o_ref[...] = (x_ref[...] * scale).astype(o_ref.dtype)


def _attn_body(qn_ref, qr_ref, kn_ref, kr_ref, v_ref, o_ref, m_ref, l_ref, acc_ref, *, nk):
    j = pl.program_id(2)

    @pl.when(j == 0)
    def _():
        m_ref[...] = jnp.full_like(m_ref, -jnp.inf)
        l_ref[...] = jnp.zeros_like(l_ref)
        acc_ref[...] = jnp.zeros_like(acc_ref)

    nt = (((1,), (1,)), ((), ()))
    s = lax.dot_general(qn_ref[...], kn_ref[...], nt, preferred_element_type=F32)
    s = s + lax.dot_general(qr_ref[...], kr_ref[...], nt, preferred_element_type=F32)
    m_old = m_ref[...]
    m_new = jnp.maximum(m_old, jnp.max(s, axis=-1, keepdims=True))
    alpha = jnp.exp2(m_old - m_new)
    p = jnp.exp2(s - m_new)
    l_ref[...] = alpha * l_ref[...] + jnp.sum(p, axis=-1, keepdims=True)
    acc_ref[...] = alpha * acc_ref[...] + jnp.dot(p.astype(BF16), v_ref[...], preferred_element_type=F32)
    m_ref[...] = m_new

    @pl.when(j == nk - 1)
    def _():
        o_ref[...] = (acc_ref[...] / l_ref[...]).astype(o_ref.dtype)


def _attention(qn, qr, kv, kr, n_ctx):
    T = kv.shape[0]
    Tq = T - n_ctx
    tq = _pick(Tq, (1024, 512, 256))
    tk = _pick(T, (768, 256))
    nk = T // tk
    qspec = lambda: pl.BlockSpec((tq, LANES), lambda h, i, j: (i, h))
    return pl.pallas_call(
        functools.partial(_attn_body, nk=nk),
        grid=(MLA_HEADS, Tq // tq, nk),
        in_specs=[qspec(), qspec(),
                  pl.BlockSpec((tk, LANES), lambda h, i, j: (j, 2 * h)),
                  pl.BlockSpec((tk, LANES), lambda h, i, j: (j, 0)),
                  pl.BlockSpec((tk, LANES), lambda h, i, j: (j, 2 * h + 1))],
        out_specs=pl.BlockSpec((tq, LANES), lambda h, i, j: (i, h)),
        out_shape=jax.ShapeDtypeStruct((Tq, MLA_HEADS * V_HEAD), BF16),
        scratch_shapes=[pltpu.VMEM((tq, 1), F32), pltpu.VMEM((tq, 1), F32), pltpu.VMEM((tq, V_HEAD), F32)],
        compiler_params=_params("parallel", "parallel", "arbitrary"),
    )(qn, qr, kv, kr, kv)


def _moe_up_body(be_ref, x_ref, wg_ref, wu_ref, o_ref, wgb_ref, wub_ref):
    b = pl.program_id(1)
    changed = jnp.logical_or(b == 0, be_ref[b] != be_ref[jnp.maximum(b - 1, 0)])

    @pl.when(changed)
    def _():
        wgb_ref[...] = wg_ref[0].astype(BF16)
        wub_ref[...] = wu_ref[0].astype(BF16)

    xv = x_ref[...]
    gate = jnp.dot(xv, wgb_ref[...], preferred_element_type=F32)
    up = jnp.dot(xv, wub_ref[...], preferred_element_type=F32)
    o_ref[...] = (gate * jax.nn.sigmoid(gate) * up).astype(o_ref.dtype)


def _moe_up(xb, block_e, wg, wu):
    n_rows, D = xb.shape
    DE = wg.shape[2]
    tb, tn = MOE_BLOCK, 256
    return pl.pallas_call(
        _moe_up_body,
        grid_spec=pltpu.PrefetchScalarGridSpec(
            num_scalar_prefetch=1,
            grid=(DE // tn, n_rows // tb),
            in_specs=[pl.BlockSpec((tb, D), lambda n, b, be: (b, 0)),
                      pl.BlockSpec((1, D, tn), lambda n, b, be: (be[b], 0, n)),
                      pl.BlockSpec((1, D, tn), lambda n, b, be: (be[b], 0, n))],
            out_specs=pl.BlockSpec((tb, tn), lambda n, b, be: (b, n)),
            scratch_shapes=[pltpu.VMEM((D, tn), BF16), pltpu.VMEM((D, tn), BF16)]),
        out_shape=jax.ShapeDtypeStruct((n_rows, DE), BF16),
        compiler_params=_params("arbitrary", "arbitrary"),
    )(block_e, xb, wg, wu)


def _moe_down_body(be_ref, h_ref, wd_ref, sw_ref, o_ref, wdb_ref):
    b = pl.program_id(1)
    changed = jnp.logical_or(b == 0, be_ref[b] != be_ref[jnp.maximum(b - 1, 0)])

    @pl.when(changed)
    def _():
        wdb_ref[...] = wd_ref[0].astype(BF16)

    o_ref[...] = jnp.dot(h_ref[...], wdb_ref[...], preferred_element_type=F32) * sw_ref[...]


def _moe_down(h, block_e, wd, slot_w):
    n_rows, DE = h.shape
    D = wd.shape[2]
    tb, tn = MOE_BLOCK, 1024
    return pl.pallas_call(
        _moe_down_body,
        grid_spec=pltpu.PrefetchScalarGridSpec(
            num_scalar_prefetch=1,
            grid=(D // tn, n_rows // tb),
            in_specs=[pl.BlockSpec((tb, DE), lambda n, b, be: (b, 0)),
                      pl.BlockSpec((1, DE, tn), lambda n, b, be: (be[b], 0, n)),
                      pl.BlockSpec((tb, 1), lambda n, b, be: (b, 0))],
            out_specs=pl.BlockSpec((tb, tn), lambda n, b, be: (b, n)),
            scratch_shapes=[pltpu.VMEM((DE, tn), BF16)]),
        out_shape=jax.ShapeDtypeStruct((n_rows, D), F32),
        compiler_params=_params("arbitrary", "arbitrary"),
    )(block_e, h, wd, slot_w.reshape(n_rows, 1))


def _route(logits, router_b):
    T = logits.shape[0]
    s = jax.nn.sigmoid(logits)
    sel = (s + router_b.astype(F32)).reshape(T, N_GROUPS, EXPERTS_PER_GROUP)
    grp = jnp.argmax(lax.top_k(sel, 2)[0].sum(-1), axis=-1)
    _, loc = lax.top_k(sel[jnp.arange(T), grp], TOP_K)
    idx = grp[:, None] * EXPERTS_PER_GROUP + loc
    wts = jnp.take_along_axis(s, idx, axis=1)
    return idx, wts / jnp.sum(wts, axis=-1, keepdims=True)


def _moe_ffn(h, logits, router_b, wg, wu, wd, sg, su, sd):
    T, D = h.shape
    idx, wts = _route(logits[:, :N_EXPERTS], router_b)
    A = T * TOP_K
    flat_e = idx.reshape(A)
    order = jnp.argsort(flat_e)
    e_sorted = flat_e[order]
    tok_sorted = (order // TOP_K).astype(jnp.int32)
    w_sorted = wts.reshape(A)[order]
    counts = jnp.bincount(flat_e, length=N_EXPERTS)
    padded = (counts + MOE_BLOCK - 1) // MOE_BLOCK * MOE_BLOCK
    pad_end = jnp.cumsum(padded)
    pad_start = pad_end - padded
    start = jnp.cumsum(counts) - counts
    dest = (pad_start[e_sorted] + jnp.arange(A) - start[e_sorted]).astype(jnp.int32)
    n_blocks = -(-A // MOE_BLOCK) + N_EXPERTS
    n_slots = n_blocks * MOE_BLOCK
    slot_tok = jnp.full((n_slots,), T, jnp.int32).at[dest].set(tok_sorted)
    slot_w = jnp.zeros((n_slots,), F32).at[dest].set(w_sorted)
    xb = jnp.concatenate([h, jnp.zeros((1, D), h.dtype)], axis=0)[slot_tok]
    block_e = jnp.minimum(jnp.searchsorted(pad_end, jnp.arange(n_blocks) * MOE_BLOCK, side='right'),
                          N_EXPERTS - 1).astype(jnp.int32)
    y_slots = _moe_down(_moe_up(xb, block_e, wg, wu), block_e, wd, slot_w)
    slot_of = jnp.zeros((A,), jnp.int32).at[order].set(dest).reshape(T, TOP_K)
    routed = y_slots[slot_of[:, 0]] + y_slots[slot_of[:, 1]]
    zeros_e = jnp.zeros((T // MOE_BLOCK,), jnp.int32)
    shared = _moe_down(_moe_up(h, zeros_e, sg[None], su[None]), zeros_e, sd[None], jnp.ones((T,), F32))
    return routed, shared


def _silu(x):
    return x * jax.nn.sigmoid(x)


def _softplus(x):
    return jnp.maximum(x, 0.0) + jnp.log(1.0 + jnp.exp(-jnp.abs(x)))


def _adaln(cond2, w, b):
    D = cond2.shape[1]
    a = jnp.zeros((SUBLANES, D), F32).at[:2].set(cond2)
    out = _matmul(a, w, tm=SUBLANES, tn=512, tk=D, a_act=_silu, rows=(b.reshape(1, -1),),
                  epi=lambda acc, row0, rows, fulls: acc + rows[0])
    return out[:2].reshape(2, -1, D)


def _gated_residual_epi(n_ctx):
    def epi(acc, row0, rows, fulls):
        rid = row0 + lax.broadcasted_iota(jnp.int32, acc.shape, 0)
        gate = jnp.where(rid < n_ctx, rows[0][0:1], rows[0][1:2])
        return fulls[0] + gate * acc
    return epi


def _pad_cols(w, width):
    return jnp.pad(w, ((0, 0), (0, width - w.shape[1])))


def _pad_rows(w, height):
    return jnp.pad(w, ((0, height - w.shape[0]), (0, 0)))


def _even_mixer(x, h, mods, n_ctx, w_in, w_out, w0, w2, a0, a2, g2, k_k, k_a, r_k, ln_x, conv_w):
    T, D = x.shape
    W = D // 2
    o = np.cumsum((0, W, W, W, DECAY_LORA, DECAY_LORA, ICLR_LORA, ICLR_LORA, GATE_LORA, W, W))
    lora = [_pad_cols(w_in[:, o[i]:o[i + 1]], LANES) for i in range(3, 7)]
    w_in_p = jnp.concatenate([w_in[:, :o[3]], w_in[:, o[8]:], *lora, w_in[:, o[7]:o[8]]], axis=1).astype(BF16)
    tm = _pick(T, (1408, 768, 256))
    z = _matmul(h, w_in_p, tm=tm, tn=768, tk=512)
    r_off, k_off, v_off, gb_off, gc_off, u_off = (i * W for i in range(6))
    lo = 6 * W

    def lora_mm(col, kdim, w, bias, act, epi):
        rows = () if bias is None else (bias.reshape(1, W),)
        return _matmul(z, _pad_rows(w, kdim), tm=tm, tn=1024, tk=kdim, a_col_off=col, a_act=act, rows=rows, epi=epi)

    decay_epi = lambda acc, row0, rows, fulls: -jnp.exp(-_softplus(-(rows[0] + acc)) - 0.5)
    iclr_epi = lambda acc, row0, rows, fulls: jax.nn.sigmoid(rows[0] + acc)
    lw_f = lora_mm(lo, LANES, w2[0], w0[0], jnp.tanh, decay_epi)
    lw_b = lora_mm(lo + LANES, LANES, w2[1], w0[1], jnp.tanh, decay_epi)
    ic_f = lora_mm(lo + 2 * LANES, LANES, a2[0], a0[0], None, iclr_epi)
    ic_b = lora_mm(lo + 3 * LANES, LANES, a2[1], a0[1], None, iclr_epi)
    gate = lora_mm(lo + 4 * LANES, GATE_LORA, g2, None, jax.nn.sigmoid, None)

    y_f, y_b = _rwkv_scan(z, r_off, k_off, v_off, lw_f, ic_f, lw_b, ic_b, k_k, k_a, n_ctx)
    o_rwkv = _rwkv_post(y_f, y_b, z, r_off, k_off, v_off, ic_f, ic_b, gate, k_a, r_k, ln_x)
    o_conv = _short_conv(z, gb_off, gc_off, u_off, conv_w, n_ctx)
    y = jnp.concatenate([o_rwkv, o_conv], axis=1)
    return _matmul(y, w_out.astype(BF16), tm=tm, tn=1024, tk=512, rows=(mods[:, 2],), fulls=(x,),
                   epi=_gated_residual_epi(n_ctx))


def _rope_tables(n_ctx, n_lat):
    rows = n_lat // GRID_W
    row = jnp.repeat(jnp.arange(rows), GRID_W)
    col = jnp.tile(jnp.arange(GRID_W), rows)
    pos = jnp.stack([row, col], axis=-1).astype(F32)
    inv_freq = ROPE_BASE ** (-jnp.arange(ROPE_PAIRS, dtype=F32) / ROPE_PAIRS)
    ang = pos[:, :, None, None] * inv_freq
    shape = (n_lat, 2, 2, ROPE_PAIRS)
    cos = jnp.broadcast_to(jnp.cos(ang), shape).reshape(n_lat, QK_ROPE)
    sin = jnp.broadcast_to(jnp.sin(ang), shape).reshape(n_lat, QK_ROPE)
    cos = jnp.concatenate([jnp.ones((n_ctx, QK_ROPE), F32), cos], axis=0)
    sin = jnp.concatenate([jnp.zeros((n_ctx, QK_ROPE), F32), sin], axis=0)
    return jnp.concatenate([cos, sin], axis=1)


def _rot_cols(w):
    lead = w.shape[:-1]
    wr = w.reshape(*lead, 2, 2, ROPE_PAIRS)
    return jnp.stack([-wr[..., 1, :], wr[..., 0, :]], axis=-2).reshape(*lead, QK_ROPE)


def _odd_mixer(x, h, mods, n_ctx, w_in, w_out, q_norm, w_uq, kv_norm, w_ukv):
    T, D = x.shape
    W = D // 2
    n_lat = T - n_ctx
    kr_w = w_in[:, W + Q_LORA + KV_LORA:]
    w_in_p = jnp.concatenate([w_in, _rot_cols(kr_w)], axis=1).astype(BF16)
    w_in_p = _pad_cols(w_in_p, -(-w_in_p.shape[1] // 768) * 768)
    tm = _pick(T, (1408, 768, 256))
    z = _matmul(h, w_in_p, tm=tm, tn=768, tk=512)
    qa_off, kva_off, kr_off = W, W + Q_LORA, W + Q_LORA + KV_LORA

    qn = _rmsnorm_cols(z, qa_off, Q_LORA, q_norm)
    kvn = _rmsnorm_cols(z, kva_off, KV_LORA, kv_norm)
    uq = w_uq.reshape(Q_LORA, MLA_HEADS, QK_NOPE + QK_ROPE)
    uq_rope = uq[:, :, QK_NOPE:]
    w_uq_p = jnp.concatenate([uq[:, :, :QK_NOPE].reshape(Q_LORA, -1),
                              jnp.concatenate([uq_rope, _rot_cols(uq_rope)], axis=-1).reshape(Q_LORA, -1)], axis=1).astype(BF16)
    tml = _pick(n_lat, (1024, 512, 256))
    q_lat = _matmul(qn[n_ctx:], w_uq_p, tm=tml, tn=1024, tk=Q_LORA)
    kv = _matmul(kvn, w_ukv.astype(BF16), tm=tm, tn=1024, tk=KV_LORA, out_dtype=BF16)

    tab = _rope_tables(n_ctx, n_lat)
    q_scale = SM_SCALE * math.log2(math.e)
    tab_lat = tab[n_ctx:]
    qr = _rope(q_lat, MLA_HEADS * QK_NOPE, MLA_HEADS, tab_lat, q_scale)
    qnope = pl.pallas_call(
        functools.partial(_scale_cols_body, scale=q_scale),
        grid=(n_lat // ROW_BLOCK, MLA_HEADS * QK_NOPE // 1024),
        in_specs=[pl.BlockSpec((ROW_BLOCK, 1024), lambda i, j: (i, j))],
        out_specs=pl.BlockSpec((ROW_BLOCK, 1024), lambda i, j: (i, j)),
        out_shape=jax.ShapeDtypeStruct((n_lat, MLA_HEADS * QK_NOPE), BF16),
        compiler_params=_params("parallel", "parallel"),
    )(q_lat)
    kr = _rope(z, kr_off, 1, tab, 1.0)
    att = _attention(qnope, qr, kv, kr, n_ctx)
    four = _fourier_mix(z, n_ctx, W)
    y = jnp.concatenate([four, att], axis=1)
    return _matmul(y, w_out.astype(BF16), tm=tml, tn=1024, tk=512, rows=(mods[:, 2],), fulls=(x[n_ctx:],),
                   epi=_gated_residual_epi(0))


def kernel(x, c, ctx, c_ctx, ada_w, ada_b, norm1_g, norm2_g, ev_w_in, ev_w_out, ev_w0, ev_w2, ev_a0, ev_a2, ev_g2, ev_k_k, ev_k_a, ev_r_k, ev_ln_x, ev_conv_w, od_w_in, od_w_out, od_q_norm, od_w_uq, od_kv_norm, od_w_ukv, router_w, router_b, moe_wg, moe_wu, moe_wd, shared_wg, shared_wu, shared_wd, final_g):
    B, n_lat, D = x.shape
    n_ctx = ctx.shape[1]
    depth = ada_w.shape[0]
    assert B == 1 and depth == 2 and n_ctx == ROW_BLOCK
    xs = jnp.concatenate([ctx[0], x[0]], axis=0)
    cond2 = jnp.concatenate([c_ctx[None], c], axis=0)
    router_w_p = _pad_cols(router_w.astype(F32), LANES)

    mods = _adaln(cond2, ada_w[0], ada_b[0])
    h = _modulate(xs, norm1_g[0], mods, 0, 1, n_ctx)
    xs = _even_mixer(xs, h, mods, n_ctx, ev_w_in[0], ev_w_out[0], ev_w0[0], ev_w2[0], ev_a0[0], ev_a2[0], ev_g2[0],
                     ev_k_k[0], ev_k_a[0], ev_r_k[0], ev_ln_x[0], ev_conv_w[0])
    h, logits = _modulate(xs, norm2_g[0], mods, 3, 4, n_ctx, router_w=router_w_p)
    routed, shared = _moe_ffn(h, logits, router_b, moe_wg[0], moe_wu[0], moe_wd[0], shared_wg[0], shared_wu[0], shared_wd[0])
    xs = _combine(xs, routed, shared, mods[:, 5:6], n_ctx)

    mods = _adaln(cond2, ada_w[1], ada_b[1])
    h = _modulate(xs, norm1_g[1], mods, 0, 1, n_ctx)
    xl = _odd_mixer(xs, h, mods, n_ctx, od_w_in[0], od_w_out[0], od_q_norm[0], od_w_uq[0], od_kv_norm[0], od_w_ukv[0])
    h, logits = _modulate(xl, norm2_g[1], mods, 3, 4, 0, router_w=router_w_p)
    routed, shared = _moe_ffn(h, logits, router_b, moe_wg[1], moe_wu[1], moe_wd[1], shared_wg[1], shared_wu[1], shared_wd[1])
    out = _combine(xl, routed, shared, mods[:, 5:6], 0, final_g=final_g)
    return out[None]
```

```python
import functools
import math

import numpy as np
import jax
import jax.numpy as jnp
from jax import lax
from jax.experimental import pallas as pl
from jax.experimental.pallas import tpu as pltpu

F32 = jnp.float32
BF16 = jnp.bfloat16
HIGHEST = lax.Precision.HIGHEST

LANES = 128
SUBLANES = 8
VMEM_LIMIT_BYTES = 56 * 1024 * 1024

EPS = 1e-6
GN_EPS = 64e-5
RWKV_HEAD = 64
DECAY_LORA = 96
ICLR_LORA = 96
GATE_LORA = 256
CONV_K = 3
FOURIER_GROUP = 128
MLA_HEADS = 16
QK_NOPE = 128
QK_ROPE = 64
V_HEAD = 128
Q_LORA = 1024
KV_LORA = 512
ROPE_PAIRS = QK_ROPE // 4
ROPE_BASE = 10000.0
GRID_W = 64
SM_SCALE = (QK_NOPE + QK_ROPE) ** -0.5
N_EXPERTS = 16
N_GROUPS = 4
EXPERTS_PER_GROUP = N_EXPERTS // N_GROUPS
TOP_K = 2
MOE_BLOCK = 256
ROW_BLOCK = 256
SCAN_CHUNK = 64


def _params(*sem):
    return pltpu.CompilerParams(dimension_semantics=sem, vmem_limit_bytes=VMEM_LIMIT_BYTES)


def _pick(n, candidates):
    for c in candidates:
        if n % c == 0:
            return c
    raise ValueError(f"no tile for {n} among {candidates}")


def _mm_body(*refs, nk, a_act, epi, n_rows, n_fulls, tm):
    a_ref, b_ref = refs[0], refs[1]
    row_refs = refs[2:2 + n_rows]
    full_refs = refs[2 + n_rows:2 + n_rows + n_fulls]
    o_ref = refs[2 + n_rows + n_fulls]
    acc_ref = refs[3 + n_rows + n_fulls]
    k = pl.program_id(2)
    av = a_ref[...]
    if a_act is not None:
        av = a_act(av.astype(F32))
    part = jnp.dot(av.astype(BF16), b_ref[...].astype(BF16), preferred_element_type=F32)

    @pl.when(k == 0)
    def _():
        acc_ref[...] = part

    @pl.when(k > 0)
    def _():
        acc_ref[...] += part

    @pl.when(k == nk - 1)
    def _():
        acc = acc_ref[...]
        if epi is not None:
            row0 = pl.program_id(0) * tm
            acc = epi(acc, row0, [r[...] for r in row_refs], [f[...] for f in full_refs])
        o_ref[...] = acc.astype(o_ref.dtype)


def _matmul(a, b, *, name, tm, tn, tk, out_dtype=F32, a_col_off=0, a_row_off=0, m=None, a_act=None,
            rows=(), fulls=(), fulls_row_off=0, epi=None):
    K, N = b.shape
    M = a.shape[0] - a_row_off if m is None else m
    assert M % tm == 0 and N % tn == 0 and K % tk == 0
    assert a_col_off % tk == 0 and a_row_off % tm == 0 and fulls_row_off % tm == 0
    nk = K // tk
    ko, ro, fo = a_col_off // tk, a_row_off // tm, fulls_row_off // tm
    in_specs = [
        pl.BlockSpec((tm, tk), lambda i, j, k: (i + ro, k + ko)),
        pl.BlockSpec((tk, tn), lambda i, j, k: (k, j)),
    ]
    for r in rows:
        in_specs.append(pl.BlockSpec((r.shape[0], tn), lambda i, j, k: (0, j)))
    for _ in fulls:
        in_specs.append(pl.BlockSpec((tm, tn), lambda i, j, k: (i + fo, j)))
    body = functools.partial(_mm_body, nk=nk, a_act=a_act, epi=epi, n_rows=len(rows),
                             n_fulls=len(fulls), tm=tm)
    return pl.pallas_call(
        body,
        grid=(M // tm, N // tn, nk),
        in_specs=in_specs,
        out_specs=pl.BlockSpec((tm, tn), lambda i, j, k: (i, j)),
        out_shape=jax.ShapeDtypeStruct((M, N), out_dtype),
        scratch_shapes=[pltpu.VMEM((tm, tn), F32)],
        compiler_params=_params("parallel", "parallel", "arbitrary"),
        name=name,
    )(a, b, *rows, *fulls)


def _modulate_body(x_ref, g_ref, mod_ref, *rest, shift_idx, scale_idx, with_router):
    xv = x_ref[...]
    y = xv * lax.rsqrt(jnp.mean(xv * xv, axis=-1, keepdims=True) + EPS) * g_ref[...]
    h = y * (1.0 + mod_ref[0, scale_idx:scale_idx + 1, :]) + mod_ref[0, shift_idx:shift_idx + 1, :]
    if with_router:
        rw_ref, o_ref, lg_ref = rest
        lg_ref[...] = jnp.dot(h, rw_ref[...], precision=HIGHEST, preferred_element_type=F32)
    else:
        (o_ref,) = rest
    o_ref[...] = h.astype(o_ref.dtype)


def _modulate(x, g, mods, shift_idx, scale_idx, n_ctx, router_w=None):
    T, D = x.shape
    tm = ROW_BLOCK
    assert T % tm == 0 and n_ctx % tm == 0
    nc = n_ctx // tm
    in_specs = [
        pl.BlockSpec((tm, D), lambda i: (i, 0)),
        pl.BlockSpec((1, D), lambda i: (0, 0)),
        pl.BlockSpec((1, mods.shape[1], D), lambda i: (jnp.where(i < nc, 0, 1), 0, 0)),
    ]
    out_specs = [pl.BlockSpec((tm, D), lambda i: (i, 0))]
    out_shape = [jax.ShapeDtypeStruct((T, D), BF16)]
    args = [x, g.reshape(1, D), mods]
    if router_w is not None:
        in_specs.append(pl.BlockSpec((D, LANES), lambda i: (0, 0)))
        out_specs.append(pl.BlockSpec((tm, LANES), lambda i: (i, 0)))
        out_shape.append(jax.ShapeDtypeStruct((T, LANES), F32))
        args.append(router_w)
    body = functools.partial(_modulate_body, shift_idx=shift_idx, scale_idx=scale_idx,
                             with_router=router_w is not None)
    outs = pl.pallas_call(body, grid=(T // tm,), in_specs=in_specs, out_specs=out_specs,
                          out_shape=out_shape, compiler_params=_params("parallel"), name="modulate")(*args)
    return outs if router_w is not None else outs[0]


def _rmsnorm_cols_body(x_ref, g_ref, o_ref):
    xv = x_ref[...]
    y = xv * lax.rsqrt(jnp.mean(xv * xv, axis=-1, keepdims=True) + EPS) * g_ref[...]
    o_ref[...] = y.astype(o_ref.dtype)


def _rmsnorm_cols(z, col_off, width, g, out_dtype=BF16):
    T = z.shape[0]
    tm = ROW_BLOCK
    assert col_off % width == 0 and T % tm == 0
    cb = col_off // width
    return pl.pallas_call(
        _rmsnorm_cols_body,
        name="rmsnorm_cols",
        grid=(T // tm,),
        in_specs=[pl.BlockSpec((tm, width), lambda i: (i, cb)), pl.BlockSpec((1, width), lambda i: (0, 0))],
        out_specs=pl.BlockSpec((tm, width), lambda i: (i, 0)),
        out_shape=jax.ShapeDtypeStruct((T, width), out_dtype),
        compiler_params=_params("parallel"),
    )(z, g.reshape(1, width))


def _combine_body(x_ref, a_ref, b_ref, gate_ref, g_ref, o_ref, *, final_norm):
    y = x_ref[...] + gate_ref[0] * (a_ref[...] + b_ref[...])
    if final_norm:
        y = y * lax.rsqrt(jnp.mean(y * y, axis=-1, keepdims=True) + EPS) * g_ref[...]
    o_ref[...] = y


def _combine(x, routed, shared, gates, n_ctx, final_g=None):
    T, D = x.shape
    tm = ROW_BLOCK
    nc = n_ctx // tm
    g = jnp.ones((1, D), F32) if final_g is None else final_g.reshape(1, D)
    blk = pl.BlockSpec((tm, D), lambda i: (i, 0))
    return pl.pallas_call(
        functools.partial(_combine_body, final_norm=final_g is not None),
        name="combine",
        grid=(T // tm,),
        in_specs=[blk, blk, blk,
                  pl.BlockSpec((1, 1, D), lambda i: (jnp.where(i < nc, 0, 1), 0, 0)),
                  pl.BlockSpec((1, D), lambda i: (0, 0))],
        out_specs=blk,
        out_shape=jax.ShapeDtypeStruct((T, D), F32),
        compiler_params=_params("parallel"),
    )(x, routed, shared, gates, g)


def _conv_body(gb_ref, gc_ref, u_ref, gcp_ref, up_ref, gcn_ref, un_ref, w_ref, o_ref, *, tb, nb, nc):
    i = pl.program_id(0)
    p = gc_ref[...] * u_ref[...]
    prev_row = (gcp_ref[...] * up_ref[...])[SUBLANES - 1:SUBLANES, :]
    next_row = (gcn_ref[...] * un_ref[...])[0:1, :]
    starts = jnp.logical_or(i == 0, i == nc)
    ends = jnp.logical_or(i == nc - 1, i == nb - 1)
    prev_row = jnp.where(starts, 0.0, prev_row)
    next_row = jnp.where(ends, 0.0, next_row)
    rid = lax.broadcasted_iota(jnp.int32, p.shape, 0)
    xm1 = jnp.where(rid == 0, prev_row, pltpu.roll(p, 1, axis=0))
    xp1 = jnp.where(rid == tb - 1, next_row, pltpu.roll(p, tb - 1, axis=0))
    w = w_ref[...]
    o_ref[...] = (gb_ref[...] * (w[0:1] * xm1 + w[1:2] * p + w[2:3] * xp1)).astype(o_ref.dtype)


def _short_conv(z, gb_off, gc_off, u_off, conv_w, n_ctx):
    T = z.shape[0]
    C = conv_w.shape[1]
    tb, tn = ROW_BLOCK, 512
    nb, nc = T // tb, n_ctx // tb
    hb = tb // SUBLANES
    last_h = T // SUBLANES - 1
    cur = lambda off: pl.BlockSpec((tb, tn), lambda i, j: (i, off // tn + j))
    prv = lambda off: pl.BlockSpec((SUBLANES, tn), lambda i, j: (jnp.maximum(i * hb - 1, 0), off // tn + j))
    nxt = lambda off: pl.BlockSpec((SUBLANES, tn), lambda i, j: (jnp.minimum((i + 1) * hb, last_h), off // tn + j))
    return pl.pallas_call(
        functools.partial(_conv_body, tb=tb, nb=nb, nc=nc),
        name="short_conv",
        grid=(nb, C // tn),
        in_specs=[cur(gb_off), cur(gc_off), cur(u_off), prv(gc_off), prv(u_off), nxt(gc_off), nxt(u_off),
                  pl.BlockSpec((CONV_K, tn), lambda i, j: (0, j))],
        out_specs=pl.BlockSpec((tb, tn), lambda i, j: (i, j)),
        out_shape=jax.ShapeDtypeStruct((T, C), BF16),
        compiler_params=_params("parallel", "parallel"),
    )(z, z, z, z, z, z, z, conv_w)


def _dot_hi(a, b):
    return jnp.dot(a, b, precision=HIGHEST, preferred_element_type=F32)


def _dot_nt_hi(a, b):
    return lax.dot_general(a, b, (((1,), (1,)), ((), ())), precision=HIGHEST, preferred_element_type=F32)


def _dot_tn_hi(a, b):
    return lax.dot_general(a, b, (((0,), (0,)), ((), ())), precision=HIGHEST, preferred_element_type=F32)


NN = (((1,), (0,)), ((), ()))
NT = (((1,), (1,)), ((), ()))
TN = (((0,), (0,)), ((), ()))

SCAN_PIECES_LOGW = 3
SCAN_PIECES_GRAM = 1
SCAN_PIECES_INV = 1
SCAN_PIECES_OUT = 1


def _pieces(x, n):
    out = []
    for i in range(n):
        p = x.astype(BF16)
        out.append(p)
        if i + 1 < n:
            x = x - p.astype(F32)
    return out


def _pdot(ap, bp, dims=NN):
    order = max(len(ap), len(bp))
    acc = None
    for i, x in enumerate(ap):
        for j, y in enumerate(bp):
            if i + j < order:
                t = lax.dot_general(x, y, dims, preferred_element_type=F32)
                acc = t if acc is None else acc + t
    return acc


def _pcat(parts, axis):
    return [jnp.concatenate(ps, axis=axis) for ps in zip(*parts)]


def _scan_stages(C, masks):
    n = 2 * C
    in_h0, strict, incl, eye, tri, ones_c, diag_blocks, off_blocks = masks
    pg, pi, po = SCAN_PIECES_GRAM, SCAN_PIECES_INV, SCAN_PIECES_OUT

    def stack(x):
        return jnp.concatenate([jnp.where(in_h0, x, 0.0), jnp.where(in_h0, 0.0, x)], axis=0)

    def s_cum(d):
        for nm in ("lw", "r", "k", "v", "a", "b"):
            d[nm + "_s"] = stack(d[nm])
        d["cum"] = _pdot([tri], _pieces(d["lw_s"], SCAN_PIECES_LOGW))
        d["tot_col"] = _pdot(_pieces(d["lw"], SCAN_PIECES_LOGW), [ones_c], TN)

    def s_exp(d):
        cum = d["cum"]
        tot = jnp.sum(d["lw"], axis=0, keepdims=True)
        inv, fin = jnp.exp(-cum), jnp.exp(tot - cum)
        d["r_hat"] = d["r_s"] * jnp.exp(cum)
        d["a_hat_p"] = _pieces(d["a_s"] * jnp.exp(cum - d["lw_s"]), max(pg, po))
        d["r_hat_p"] = _pieces(d["r_hat"], pg)
        d["bk_chk_p"] = _pcat([_pieces(d["b_s"] * inv, pg), _pieces(d["k_s"] * inv, pg)], 0)
        d["b_til_p"] = _pieces(d["b_s"] * fin, po)
        d["bk_til_p"] = _pcat([d["b_til_p"], _pieces(d["k_s"] * fin, po)], 0)
        d["v_p"] = _pieces(d["v_s"], po)
        d["decay"] = jnp.exp(d["tot_col"])

    def s_gram(d):
        g = _pdot(_pcat([d["a_hat_p"][:pg], d["r_hat_p"]], 0), d["bk_chk_p"], NT)
        d["n_ab"] = jnp.where(strict, g[:n, :n], 0.0)
        d["m_ak_p"] = _pieces(jnp.where(strict, g[:n, n:], 0.0), po)
        d["m_rb_p"] = _pieces(jnp.where(incl, g[n:, :n], 0.0), po)
        d["m_rk_p"] = _pieces(jnp.where(incl, g[n:, n:], 0.0), po)
        nd = jnp.where(diag_blocks, d["n_ab"], 0.0)
        d["nd_p"] = _pieces(nd, pi)
        d["t"] = eye + nd

    def s_sq1(d):
        d["pw"] = _pdot(d["nd_p"], d["nd_p"])

    def s_ap1(d):
        d["pw_p"] = _pieces(d["pw"], pi)
        d["t"] = d["t"] + _pdot(d["pw_p"], _pieces(d["t"], pi))

    def s_sq2(d):
        d["pw_p"] = _pieces(_pdot(d["pw_p"], d["pw_p"]), pi)

    def s_ap2(d):
        d["t"] = d["t"] + _pdot(d["pw_p"], _pieces(d["t"], pi))

    def s_merge_a(off_mask):
        def f(d):
            d["t_p"] = _pieces(d["t"], pi)
            d["ot_p"] = _pieces(_pdot(_pieces(jnp.where(off_mask, d["n_ab"], 0.0), pi), d["t_p"]), pi)
        return f

    def s_merge_b(d):
        d["t"] = d["t"] + _pdot(d["t_p"], d["ot_p"])

    def s_abar(d):
        d["t_p"] = _pieces(d["t"], po)
        d["abar_p"] = _pieces(_pdot(d["t_p"], d["a_hat_p"][:po]), po)
        d["mv_p"] = _pieces(_pdot(d["m_ak_p"], d["v_p"]), po)

    def s_u0(d):
        d["uv_p"] = _pcat([_pieces(_pdot(d["t_p"], d["mv_p"]), po), d["v_p"]], 0)
        d["rbar_p"] = _pieces(d["r_hat"] + _pdot(d["m_rb_p"], d["abar_p"]), po)
        d["phi_p"] = _pieces(_pdot(d["b_til_p"], d["abar_p"], TN), po)

    def s_out(d):
        d["y0"] = _pdot(_pcat([d["m_rb_p"], d["m_rk_p"]], 1), d["uv_p"])
        d["s0"] = _pdot(d["bk_til_p"], d["uv_p"], TN)

    stages = [s_cum, s_exp, s_gram, s_sq1, s_ap1, s_sq2, s_ap2]
    for off_mask in off_blocks:
        stages += [s_merge_a(off_mask), s_merge_b]
    return stages + [s_abar, s_u0, s_out]


def _scan_apply(d, state):
    C = d["r"].shape[0]
    st_p = _pieces(state, SCAN_PIECES_OUT)
    y = _pdot(d["rbar_p"], st_p) + d["y0"]
    return y[:C] + y[C:], d["decay"] * state + _pdot(d["phi_p"], st_p) + d["s0"]


def _scan_masks(C, reverse):
    n = 2 * C
    lane = lax.broadcasted_iota(jnp.int32, (C, LANES), 1)
    row = lax.broadcasted_iota(jnp.int32, (n, n), 0)
    col = lax.broadcasted_iota(jnp.int32, (n, n), 1)
    same = (row // C) == (col // C)
    before = (col > row) if reverse else (col < row)
    strict = jnp.logical_and(same, before)
    incl = jnp.logical_and(same, jnp.logical_or(before, col == row))
    eye = jnp.where(row == col, 1.0, 0.0)
    tri = jnp.where(incl, 1.0, 0.0).astype(BF16)
    ones_c = jnp.ones((C, LANES), BF16)
    blk = 8
    diag_blocks = (row // blk) == (col // blk)
    off_blocks = []
    while blk < C:
        off_blocks.append(jnp.logical_and((row // (2 * blk)) == (col // (2 * blk)), (row // blk) != (col // blk)))
        blk *= 2
    return lane < RWKV_HEAD, strict, incl, eye, tri, ones_c, diag_blocks, off_blocks


def _scan_body(rf, kf, vf, lwf, icf, rb, kb, vb, lwb, icb, kk_ref, ka_ref, seg_ref, yf_ref, yb_ref, sf_ref, sb_ref, *, tb):
    @pl.when(pl.program_id(1) == 0)
    def _():
        sf_ref[...] = jnp.zeros_like(sf_ref)
        sb_ref[...] = jnp.zeros_like(sb_ref)

    k_k, k_a, seg = kk_ref[...], ka_ref[...], seg_ref[...]
    nch = tb // SCAN_CHUNK
    dirs = []
    for refs, y_ref, s_ref, reverse in ((rf, kf, vf, lwf, icf), yf_ref, sf_ref, False), ((rb, kb, vb, lwb, icb), yb_ref, sb_ref, True):
        r, k, v, lw, ic = (t[...] for t in refs)
        kk = k * k_k
        kk = kk * lax.rsqrt(_dot_hi(kk * kk, seg) + 1e-12)
        a, b = -kk, kk * ic
        kd = k * (1.0 + (ic - 1.0) * k_a)
        chunks = []
        for c in (range(nch - 1, -1, -1) if reverse else range(nch)):
            sl = slice(c * SCAN_CHUNK, (c + 1) * SCAN_CHUNK)
            chunks.append(dict(r=r[sl], k=kd[sl], v=v[sl], a=a[sl], b=b[sl], lw=lw[sl], rows=sl))
        dirs.append((chunks, _scan_stages(SCAN_CHUNK, _scan_masks(SCAN_CHUNK, reverse)), y_ref, s_ref))

    for step in range(len(dirs[0][1])):
        for chunks, stages, _, _ in dirs:
            for d in chunks:
                stages[step](d)
    states = [s_ref[...] for _, _, _, s_ref in dirs]
    for c in range(nch):
        for i, (chunks, _, y_ref, _) in enumerate(dirs):
            y, states[i] = _scan_apply(chunks[c], states[i])
            y_ref[chunks[c]["rows"], :] = y
    for (_, _, _, s_ref), state in zip(dirs, states):
        s_ref[...] = state


def _rwkv_scan(z, r_off, k_off, v_off, lw_f, ic_f, lw_b, ic_b, k_k, k_a, n_ctx):
    T = z.shape[0]
    W = lw_f.shape[1]
    tb = ROW_BLOCK
    assert n_ctx == tb and T % tb == 0
    nb = T // tb
    npair = W // LANES
    fwd = lambda i: i
    bwd = lambda i: jnp.where(i == 0, 0, nb - i)
    zspec = lambda off, o: pl.BlockSpec((tb, LANES), lambda p, i: (o(i), off // LANES + p))
    wspec = lambda o: pl.BlockSpec((tb, LANES), lambda p, i: (o(i), p))
    par = pl.BlockSpec((1, LANES), lambda p, i: (0, p))
    lane = np.arange(LANES)
    seg = jnp.asarray((lane[:, None] // RWKV_HEAD == lane[None, :] // RWKV_HEAD).astype(np.float32))
    return pl.pallas_call(
        functools.partial(_scan_body, tb=tb),
        name="rwkv_scan",
        grid=(npair, nb),
        in_specs=[zspec(r_off, fwd), zspec(k_off, fwd), zspec(v_off, fwd), wspec(fwd), wspec(fwd),
                  zspec(r_off, bwd), zspec(k_off, bwd), zspec(v_off, bwd), wspec(bwd), wspec(bwd),
                  par, par, pl.BlockSpec((LANES, LANES), lambda p, i: (0, 0))],
        out_specs=[wspec(fwd), wspec(bwd)],
        out_shape=[jax.ShapeDtypeStruct((T, W), F32)] * 2,
        scratch_shapes=[pltpu.VMEM((LANES, LANES), F32)] * 2,
        compiler_params=_params("parallel", "arbitrary"),
    )(z, z, z, lw_f, ic_f, z, z, z, lw_b, ic_b, k_k.reshape(1, W), k_a.reshape(1, W), seg)


def _rwkv_post_body(yf, yb, r, k, v, icf, icb, gate, ka, rk, lnx, seg_ref, o_ref):
    seg = seg_ref[...]
    inv_n = 1.0 / RWKV_HEAD
    wkv = yf[...] + yb[...]
    yc = wkv - _dot_hi(wkv, seg) * inv_n
    yn = yc * lax.rsqrt(_dot_hi(yc * yc, seg) * inv_n + GN_EPS) * lnx[...]
    kv, kav = k[...], ka[...]
    kd_sum = kv * (1.0 + (icf[...] - 1.0) * kav) + kv * (1.0 + (icb[...] - 1.0) * kav)
    bonus = _dot_hi(r[...] * kd_sum * rk[...], seg) * v[...]
    o_ref[...] = ((yn + bonus) * gate[...]).astype(o_ref.dtype)


def _rwkv_post(y_f, y_b, z, r_off, k_off, v_off, ic_f, ic_b, gate, k_a, r_k, ln_x):
    T, W = y_f.shape
    tb, tn = ROW_BLOCK, 256
    blk = pl.BlockSpec((tb, tn), lambda i, j: (i, j))
    zspec = lambda off: pl.BlockSpec((tb, tn), lambda i, j: (i, off // tn + j))
    par = pl.BlockSpec((1, tn), lambda i, j: (0, j))
    lane = np.arange(tn)
    seg = jnp.asarray((lane[:, None] // RWKV_HEAD == lane[None, :] // RWKV_HEAD).astype(np.float32))
    return pl.pallas_call(
        _rwkv_post_body,
        name="rwkv_post",
        grid=(T // tb, W // tn),
        in_specs=[blk, blk, zspec(r_off), zspec(k_off), zspec(v_off), blk, blk, blk, par, par, par,
                  pl.BlockSpec((tn, tn), lambda i, j: (0, 0))],
        out_specs=blk,
        out_shape=jax.ShapeDtypeStruct((T, W), BF16),
        compiler_params=_params("parallel", "parallel"),
    )(y_f, y_b, z, z, z, ic_f, ic_b, gate, k_a.reshape(1, W), r_k.reshape(1, W), ln_x.reshape(1, W), seg)


def _dft_cols_body(u_ref, cs_ref, o_ref):
    pq = jnp.dot(u_ref[...].astype(BF16), cs_ref[...], preferred_element_type=F32)
    o_ref[0] = pq[:, :FOURIER_GROUP].astype(o_ref.dtype)
    o_ref[1] = pq[:, FOURIER_GROUP:].astype(o_ref.dtype)


def _fourier_mix(z, n_ctx, width):
    T = z.shape[0] - n_ctx
    tm = ROW_BLOCK
    ro = n_ctx // tm
    G = FOURIER_GROUP
    c = np.arange(G)
    ang_c = 2.0 * np.pi * ((c[:, None] * c[None, :]) % G) / G
    cs = jnp.asarray(np.concatenate([np.cos(ang_c), np.sin(ang_c)], axis=1), BF16)
    pq = pl.pallas_call(
        _dft_cols_body,
        name="dft_cols",
        grid=(T // tm, width // G),
        in_specs=[pl.BlockSpec((tm, G), lambda i, g: (i + ro, g)), pl.BlockSpec((G, 2 * G), lambda i, g: (0, 0))],
        out_specs=pl.BlockSpec((2, tm, G), lambda i, g: (0, i, g)),
        out_shape=jax.ShapeDtypeStruct((2, T, width), BF16),
        compiler_params=_params("parallel", "parallel"),
    )(z, cs)
    t = jnp.arange(T, dtype=jnp.int32)
    ang = ((t[:, None] * t[None, :]) % T).astype(F32) * (2.0 * math.pi / T)
    dft = jnp.concatenate([jnp.cos(ang), -jnp.sin(ang)], axis=1).astype(BF16)
    scale = 1.0 / math.sqrt(T * G)
    tmm = _pick(T, (1024, 512, 256))
    return _matmul(dft, pq.reshape(2 * T, width), name="dft_rows", tm=tmm, tn=_pick(width, (1024, 512)), tk=_pick(2 * T, (1024, 512)),
                   out_dtype=BF16, epi=lambda acc, row0, rows, fulls: acc * scale)


def _rope_body(x_ref, tab_ref, o_ref, *, scale, keep_dup):
    xt = x_ref[...] * tab_ref[...]
    y = xt + pltpu.roll(xt, QK_ROPE, axis=1)
    if not keep_dup:
        lane = lax.broadcasted_iota(jnp.int32, y.shape, 1)
        y = jnp.where(lane < QK_ROPE, y, 0.0)
    o_ref[...] = (y * scale).astype(o_ref.dtype)


def _rope(z, col_off, n_heads, tab, scale):
    T = z.shape[0]
    tm = ROW_BLOCK
    cb = col_off // LANES
    return pl.pallas_call(
        functools.partial(_rope_body, scale=scale, keep_dup=False),
        name="rope",
        grid=(T // tm, n_heads),
        in_specs=[pl.BlockSpec((tm, LANES), lambda i, h: (i, cb + h)), pl.BlockSpec((tm, LANES), lambda i, h: (i, 0))],
        out_specs=pl.BlockSpec((tm, LANES), lambda i, h: (i, h)),
        out_shape=jax.ShapeDtypeStruct((T, n_heads * LANES), BF16),
        compiler_params=_params("parallel", "parallel"),
    )(z, tab)


def _q_final_body(qn_ref, qr_ref, tab_ref, o_ref, *, scale):
    xt = qr_ref[...] * tab_ref[...]
    y = xt + pltpu.roll(xt, QK_ROPE, axis=1)
    lane = lax.broadcasted_iota(jnp.int32, y.shape, 1)
    o_ref[:, :QK_NOPE] = (qn_ref[...] * scale).astype(o_ref.dtype)
    o_ref[:, QK_NOPE:] = (jnp.where(lane < QK_ROPE, y, 0.0) * scale).astype(o_ref.dtype)


def _q_final(q, tab, scale):
    T = q.shape[0]
    tm = ROW_BLOCK
    blk = lambda off: pl.BlockSpec((tm, LANES), lambda i, h: (i, off + h))
    return pl.pallas_call(
        functools.partial(_q_final_body, scale=scale),
        name="q_final",
        grid=(T // tm, MLA_HEADS),
        in_specs=[blk(0), blk(MLA_HEADS), pl.BlockSpec((tm, LANES), lambda i, h: (i, 0))],
        out_specs=pl.BlockSpec((tm, 2 * LANES), lambda i, h: (i, h)),
        out_shape=jax.ShapeDtypeStruct((T, MLA_HEADS * 2 * LANES), BF16),
        compiler_params=_params("parallel", "parallel"),
    )(q, q, tab)


ATTN_SUBTILES = 4


def _attn_body(q_ref, kn_ref, kr_ref, v_ref, o_ref, m_ref, acc_ref, *, nk):
    j = pl.program_id(2)

    @pl.when(j == 0)
    def _():
        m_ref[...] = jnp.full_like(m_ref, -jnp.inf)
        acc_ref[...] = jnp.zeros_like(acc_ref)

    k_cat = jnp.concatenate([kn_ref[...], kr_ref[...]], axis=1)
    v_cat = jnp.concatenate([v_ref[...], jnp.ones(v_ref.shape, BF16)], axis=1)
    ts = q_ref.shape[0] // ATTN_SUBTILES

    def logits(i):
        return lax.dot_general(q_ref[i * ts:(i + 1) * ts, :], k_cat, NT, preferred_element_type=F32)

    def update(i, s):
        rows = slice(i * ts, (i + 1) * ts)
        m_old = m_ref[rows, :]
        m_new = jnp.maximum(m_old, jnp.max(s, axis=-1, keepdims=True))
        p = jnp.exp2(s - m_new).astype(BF16)
        acc_ref[rows, :] = jnp.exp2(m_old - m_new) * acc_ref[rows, :] + jnp.dot(p, v_cat, preferred_element_type=F32)
        m_ref[rows, :] = m_new

    s_prev = logits(0)
    for i in range(1, ATTN_SUBTILES):
        s_next = logits(i)
        update(i - 1, s_prev)
        s_prev = s_next
    update(ATTN_SUBTILES - 1, s_prev)

    @pl.when(j == nk - 1)
    def _():
        acc = acc_ref[...]
        o_ref[...] = (acc[:, :V_HEAD] / acc[:, V_HEAD:]).astype(o_ref.dtype)


def _attention(q, kv, kr, n_ctx):
    T = kv.shape[0]
    Tq = q.shape[0]
    tq = _pick(Tq, (1024, 512, 256))
    tk = _pick(T, (1408, 768, 256))
    nk = T // tk
    return pl.pallas_call(
        functools.partial(_attn_body, nk=nk),
        name="mla_attention",
        grid=(MLA_HEADS, Tq // tq, nk),
        in_specs=[pl.BlockSpec((tq, 2 * LANES), lambda h, i, j: (i, h)),
                  pl.BlockSpec((tk, LANES), lambda h, i, j: (j, 2 * h)),
                  pl.BlockSpec((tk, LANES), lambda h, i, j: (j, 0)),
                  pl.BlockSpec((tk, LANES), lambda h, i, j: (j, 2 * h + 1))],
        out_specs=pl.BlockSpec((tq, LANES), lambda h, i, j: (i, h)),
        out_shape=jax.ShapeDtypeStruct((Tq, MLA_HEADS * V_HEAD), BF16),
        scratch_shapes=[pltpu.VMEM((tq, 1), F32), pltpu.VMEM((tq, 2 * V_HEAD), F32)],
        compiler_params=_params("parallel", "parallel", "arbitrary"),
    )(q, kv, kr, kv)


def _moe_up_body(be_ref, x_ref, wg_ref, wu_ref, o_ref, wgb_ref, wub_ref):
    b = pl.program_id(1)
    changed = jnp.logical_or(b == 0, be_ref[b] != be_ref[jnp.maximum(b - 1, 0)])

    @pl.when(changed)
    def _():
        wgb_ref[...] = wg_ref[0].astype(BF16)
        wub_ref[...] = wu_ref[0].astype(BF16)

    xv = x_ref[...]
    gate = jnp.dot(xv, wgb_ref[...], preferred_element_type=F32)
    up = jnp.dot(xv, wub_ref[...], preferred_element_type=F32)
    o_ref[...] = (gate * jax.nn.sigmoid(gate) * up).astype(o_ref.dtype)


def _moe_up(xb, block_e, wg, wu):
    n_rows, D = xb.shape
    DE = wg.shape[2]
    tb, tn = MOE_BLOCK, 256
    return pl.pallas_call(
        _moe_up_body,
        name="moe_up",
        grid_spec=pltpu.PrefetchScalarGridSpec(
            num_scalar_prefetch=1,
            grid=(DE // tn, n_rows // tb),
            in_specs=[pl.BlockSpec((tb, D), lambda n, b, be: (b, 0)),
                      pl.BlockSpec((1, D, tn), lambda n, b, be: (be[b], 0, n)),
                      pl.BlockSpec((1, D, tn), lambda n, b, be: (be[b], 0, n))],
            out_specs=pl.BlockSpec((tb, tn), lambda n, b, be: (b, n)),
            scratch_shapes=[pltpu.VMEM((D, tn), BF16), pltpu.VMEM((D, tn), BF16)]),
        out_shape=jax.ShapeDtypeStruct((n_rows, DE), BF16),
        compiler_params=_params("arbitrary", "arbitrary"),
    )(block_e, xb, wg, wu)


def _moe_down_body(be_ref, h_ref, wd_ref, sw_ref, o_ref, wdb_ref):
    b = pl.program_id(1)
    changed = jnp.logical_or(b == 0, be_ref[b] != be_ref[jnp.maximum(b - 1, 0)])

    @pl.when(changed)
    def _():
        wdb_ref[...] = wd_ref[0].astype(BF16)

    o_ref[...] = jnp.dot(h_ref[...], wdb_ref[...], preferred_element_type=F32) * sw_ref[...]


def _moe_down(h, block_e, wd, slot_w):
    n_rows, DE = h.shape
    D = wd.shape[2]
    tb, tn = MOE_BLOCK, 1024
    return pl.pallas_call(
        _moe_down_body,
        name="moe_down",
        grid_spec=pltpu.PrefetchScalarGridSpec(
            num_scalar_prefetch=1,
            grid=(D // tn, n_rows // tb),
            in_specs=[pl.BlockSpec((tb, DE), lambda n, b, be: (b, 0)),
                      pl.BlockSpec((1, DE, tn), lambda n, b, be: (be[b], 0, n)),
                      pl.BlockSpec((tb, 1), lambda n, b, be: (b, 0))],
            out_specs=pl.BlockSpec((tb, tn), lambda n, b, be: (b, n)),
            scratch_shapes=[pltpu.VMEM((DE, tn), BF16)]),
        out_shape=jax.ShapeDtypeStruct((n_rows, D), F32),
        compiler_params=_params("arbitrary", "arbitrary"),
    )(block_e, h, wd, slot_w.reshape(n_rows, 1))


def _route(logits, router_b):
    T = logits.shape[0]
    s = jax.nn.sigmoid(logits)
    sel = (s + router_b.astype(F32)).reshape(T, N_GROUPS, EXPERTS_PER_GROUP)
    grp = jnp.argmax(lax.top_k(sel, 2)[0].sum(-1), axis=-1)
    _, loc = lax.top_k(sel[jnp.arange(T), grp], TOP_K)
    idx = grp[:, None] * EXPERTS_PER_GROUP + loc
    wts = jnp.take_along_axis(s, idx, axis=1)
    return idx, wts / jnp.sum(wts, axis=-1, keepdims=True)


def _moe_ffn(h, logits, router_b, wg, wu, wd, sg, su, sd):
    T, D = h.shape
    idx, wts = _route(logits[:, :N_EXPERTS], router_b)
    A = T * TOP_K
    flat_e = idx.reshape(A)
    onehot = (flat_e[:, None] == jnp.arange(N_EXPERTS)[None, :]).astype(jnp.int32)
    csum = jnp.cumsum(onehot, axis=0)
    counts = csum[-1]
    rank = jnp.sum(onehot * csum, axis=1) - 1
    padded = (counts + MOE_BLOCK - 1) // MOE_BLOCK * MOE_BLOCK
    pad_end = jnp.cumsum(padded)
    pad_start = pad_end - padded
    dest = (jnp.sum(onehot * pad_start[None, :], axis=1) + rank).astype(jnp.int32)
    n_blocks = -(-A // MOE_BLOCK) + N_EXPERTS
    n_slots = n_blocks * MOE_BLOCK
    slot_tok = jnp.full((n_slots,), T, jnp.int32).at[dest].set(jnp.arange(A, dtype=jnp.int32) // TOP_K)
    slot_w = jnp.zeros((n_slots,), F32).at[dest].set(wts.reshape(A))
    xb = jnp.concatenate([h, jnp.zeros((1, D), h.dtype)], axis=0)[slot_tok]
    block_e = jnp.minimum(jnp.sum(pad_end[None, :] <= (jnp.arange(n_blocks) * MOE_BLOCK)[:, None], axis=1),
                          N_EXPERTS - 1).astype(jnp.int32)
    y_slots = _moe_down(_moe_up(xb, block_e, wg, wu), block_e, wd, slot_w)
    slot_of = dest.reshape(T, TOP_K)
    routed = y_slots[slot_of[:, 0]] + y_slots[slot_of[:, 1]]
    zeros_e = jnp.zeros((T // MOE_BLOCK,), jnp.int32)
    shared = _moe_down(_moe_up(h, zeros_e, sg[None], su[None]), zeros_e, sd[None], jnp.ones((T,), F32))
    return routed, shared


def _silu(x):
    return x * jax.nn.sigmoid(x)


def _softplus(x):
    return jnp.maximum(x, 0.0) + jnp.log(1.0 + jnp.exp(-jnp.abs(x)))


def _adaln(cond2, w, b):
    D = cond2.shape[1]
    a = jnp.zeros((SUBLANES, D), F32).at[:2].set(cond2)
    out = _matmul(a, w, name="adaln", tm=SUBLANES, tn=512, tk=D, a_act=_silu, rows=(b.reshape(1, -1),),
                  epi=lambda acc, row0, rows, fulls: acc + rows[0])
    return out[:2].reshape(2, -1, D)


def _gated_residual_epi(n_ctx):
    def epi(acc, row0, rows, fulls):
        rid = row0 + lax.broadcasted_iota(jnp.int32, acc.shape, 0)
        gate = jnp.where(rid < n_ctx, rows[0][0:1], rows[0][1:2])
        return fulls[0] + gate * acc
    return epi


def _pad_cols(w, width):
    return jnp.pad(w, ((0, 0), (0, width - w.shape[1])))


def _pad_rows(w, height):
    return jnp.pad(w, ((0, height - w.shape[0]), (0, 0)))


def _even_mixer(x, h, mods, n_ctx, w_in, w_out, w0, w2, a0, a2, g2, k_k, k_a, r_k, ln_x, conv_w):
    T, D = x.shape
    W = D // 2
    o = np.cumsum((0, W, W, W, DECAY_LORA, DECAY_LORA, ICLR_LORA, ICLR_LORA, GATE_LORA, W, W))
    lora = [_pad_cols(w_in[:, o[i]:o[i + 1]], LANES) for i in range(3, 7)]
    w_in_p = jnp.concatenate([w_in[:, :o[3]], w_in[:, o[8]:], *lora, w_in[:, o[7]:o[8]]], axis=1).astype(BF16)
    tm = _pick(T, (1408, 768, 256))
    z = _matmul(h, w_in_p, name="even_w_in", tm=tm, tn=768, tk=512)
    r_off, k_off, v_off, gb_off, gc_off, u_off = (i * W for i in range(6))
    lo = 6 * W

    def lora_mm(col, kdim, w, bias, act, epi):
        rows = () if bias is None else (bias.reshape(1, W),)
        return _matmul(z, _pad_rows(w, kdim), name="rwkv_lora", tm=tm, tn=1024, tk=kdim, a_col_off=col, a_act=act, rows=rows, epi=epi)

    decay_epi = lambda acc, row0, rows, fulls: -jnp.exp(-_softplus(-(rows[0] + acc)) - 0.5)
    iclr_epi = lambda acc, row0, rows, fulls: jax.nn.sigmoid(rows[0] + acc)
    lw_f = lora_mm(lo, LANES, w2[0], w0[0], jnp.tanh, decay_epi)
    lw_b = lora_mm(lo + LANES, LANES, w2[1], w0[1], jnp.tanh, decay_epi)
    ic_f = lora_mm(lo + 2 * LANES, LANES, a2[0], a0[0], None, iclr_epi)
    ic_b = lora_mm(lo + 3 * LANES, LANES, a2[1], a0[1], None, iclr_epi)
    gate = lora_mm(lo + 4 * LANES, GATE_LORA, g2, None, jax.nn.sigmoid, None)

    y_f, y_b = _rwkv_scan(z, r_off, k_off, v_off, lw_f, ic_f, lw_b, ic_b, k_k, k_a, n_ctx)
    o_rwkv = _rwkv_post(y_f, y_b, z, r_off, k_off, v_off, ic_f, ic_b, gate, k_a, r_k, ln_x)
    o_conv = _short_conv(z, gb_off, gc_off, u_off, conv_w, n_ctx)
    y = jnp.concatenate([o_rwkv, o_conv], axis=1)
    return _matmul(y, w_out.astype(BF16), name="even_w_out", tm=tm, tn=1024, tk=512, rows=(mods[:, 2],), fulls=(x,),
                   epi=_gated_residual_epi(n_ctx))


def _rope_tables(n_ctx, n_lat):
    rows = n_lat // GRID_W
    row = jnp.repeat(jnp.arange(rows), GRID_W)
    col = jnp.tile(jnp.arange(GRID_W), rows)
    pos = jnp.stack([row, col], axis=-1).astype(F32)
    inv_freq = ROPE_BASE ** (-jnp.arange(ROPE_PAIRS, dtype=F32) / ROPE_PAIRS)
    ang = pos[:, :, None, None] * inv_freq
    shape = (n_lat, 2, 2, ROPE_PAIRS)
    cos = jnp.broadcast_to(jnp.cos(ang), shape).reshape(n_lat, QK_ROPE)
    sin = jnp.broadcast_to(jnp.sin(ang), shape).reshape(n_lat, QK_ROPE)
    cos = jnp.concatenate([jnp.ones((n_ctx, QK_ROPE), F32), cos], axis=0)
    sin = jnp.concatenate([jnp.zeros((n_ctx, QK_ROPE), F32), sin], axis=0)
    return jnp.concatenate([cos, sin], axis=1)


def _rot_cols(w):
    lead = w.shape[:-1]
    wr = w.reshape(*lead, 2, 2, ROPE_PAIRS)
    return jnp.stack([-wr[..., 1, :], wr[..., 0, :]], axis=-2).reshape(*lead, QK_ROPE)


def _odd_mixer(x, h, mods, n_ctx, w_in, w_out, q_norm, w_uq, kv_norm, w_ukv):
    T, D = x.shape
    W = D // 2
    n_lat = T - n_ctx
    kr_w = w_in[:, W + Q_LORA + KV_LORA:]
    w_in_p = jnp.concatenate([w_in, _rot_cols(kr_w)], axis=1).astype(BF16)
    w_in_p = _pad_cols(w_in_p, -(-w_in_p.shape[1] // 768) * 768)
    tm = _pick(T, (1408, 768, 256))
    z = _matmul(h, w_in_p, name="odd_w_in", tm=tm, tn=768, tk=512)
    qa_off, kva_off, kr_off = W, W + Q_LORA, W + Q_LORA + KV_LORA

    qn = _rmsnorm_cols(z, qa_off, Q_LORA, q_norm)
    kvn = _rmsnorm_cols(z, kva_off, KV_LORA, kv_norm)
    uq = w_uq.reshape(Q_LORA, MLA_HEADS, QK_NOPE + QK_ROPE)
    uq_rope = uq[:, :, QK_NOPE:]
    w_uq_p = jnp.concatenate([uq[:, :, :QK_NOPE].reshape(Q_LORA, -1),
                              jnp.concatenate([uq_rope, _rot_cols(uq_rope)], axis=-1).reshape(Q_LORA, -1)], axis=1).astype(BF16)
    tml = _pick(n_lat, (1024, 512, 256))
    q_lat = _matmul(qn[n_ctx:], w_uq_p, name="mla_uq", tm=tml, tn=1024, tk=Q_LORA)
    kv = _matmul(kvn, w_ukv.astype(BF16), name="mla_ukv", tm=tm, tn=1024, tk=KV_LORA, out_dtype=BF16)

    tab = _rope_tables(n_ctx, n_lat)
    q_scale = SM_SCALE * math.log2(math.e)
    tab_lat = tab[n_ctx:]
    q_fin = _q_final(q_lat, tab_lat, q_scale)
    kr = _rope(z, kr_off, 1, tab, 1.0)
    att = _attention(q_fin, kv, kr, n_ctx)
    four = _fourier_mix(z, n_ctx, W)
    y = jnp.concatenate([four, att], axis=1)
    return _matmul(y, w_out.astype(BF16), name="odd_w_out", tm=tml, tn=1024, tk=512, rows=(mods[:, 2],), fulls=(x[n_ctx:],),
                   epi=_gated_residual_epi(0))


def kernel(x, c, ctx, c_ctx, ada_w, ada_b, norm1_g, norm2_g, ev_w_in, ev_w_out, ev_w0, ev_w2, ev_a0, ev_a2, ev_g2, ev_k_k, ev_k_a, ev_r_k, ev_ln_x, ev_conv_w, od_w_in, od_w_out, od_q_norm, od_w_uq, od_kv_norm, od_w_ukv, router_w, router_b, moe_wg, moe_wu, moe_wd, shared_wg, shared_wu, shared_wd, final_g):
    B, n_lat, D = x.shape
    n_ctx = ctx.shape[1]
    depth = ada_w.shape[0]
    assert B == 1 and depth == 2 and n_ctx == ROW_BLOCK
    xs = jnp.concatenate([ctx[0], x[0]], axis=0)
    cond2 = jnp.concatenate([c_ctx[None], c], axis=0)
    router_w_p = _pad_cols(router_w.astype(F32), LANES)

    mods = _adaln(cond2, ada_w[0], ada_b[0])
    h = _modulate(xs, norm1_g[0], mods, 0, 1, n_ctx)
    xs = _even_mixer(xs, h, mods, n_ctx, ev_w_in[0], ev_w_out[0], ev_w0[0], ev_w2[0], ev_a0[0], ev_a2[0], ev_g2[0],
                     ev_k_k[0], ev_k_a[0], ev_r_k[0], ev_ln_x[0], ev_conv_w[0])
    h, logits = _modulate(xs, norm2_g[0], mods, 3, 4, n_ctx, router_w=router_w_p)
    routed, shared = _moe_ffn(h, logits, router_b, moe_wg[0], moe_wu[0], moe_wd[0], shared_wg[0], shared_wu[0], shared_wd[0])
    xs = _combine(xs, routed, shared, mods[:, 5:6], n_ctx)

    mods = _adaln(cond2, ada_w[1], ada_b[1])
    h = _modulate(xs, norm1_g[1], mods, 0, 1, n_ctx)
    xl = _odd_mixer(xs, h, mods, n_ctx, od_w_in[0], od_w_out[0], od_q_norm[0], od_w_uq[0], od_kv_norm[0], od_w_ukv[0])
    h, logits = _modulate(xl, norm2_g[1], mods, 3, 4, 0, router_w=router_w_p)
    routed, shared = _moe_ffn(h, logits, router_b, moe_wg[1], moe_wu[1], moe_wd[1], shared_wg[1], shared_wu[1], shared_wd[1])
    out = _combine(xl, routed, shared, mods[:, 5:6], 0, final_g=final_g)
    return out[None]
```

```python
import functools
import math

import numpy as np
import jax
import jax.numpy as jnp
from jax import lax
from jax.experimental import pallas as pl
from jax.experimental.pallas import tpu as pltpu

F32 = jnp.float32
BF16 = jnp.bfloat16
HIGHEST = lax.Precision.HIGHEST

LANES = 128
SUBLANES = 8
VMEM_LIMIT_BYTES = 56 * 1024 * 1024

EPS = 1e-6
GN_EPS = 64e-5
RWKV_HEAD = 64
DECAY_LORA = 96
ICLR_LORA = 96
GATE_LORA = 256
CONV_K = 3
FOURIER_GROUP = 128
MLA_HEADS = 16
QK_NOPE = 128
QK_ROPE = 64
V_HEAD = 128
Q_LORA = 1024
KV_LORA = 512
ROPE_PAIRS = QK_ROPE // 4
ROPE_BASE = 10000.0
GRID_W = 64
SM_SCALE = (QK_NOPE + QK_ROPE) ** -0.5
N_EXPERTS = 16
N_GROUPS = 4
EXPERTS_PER_GROUP = N_EXPERTS // N_GROUPS
TOP_K = 2
MOE_BLOCK = 256
SHARED_TN = 1024
ROW_BLOCK = 256
SCAN_CHUNK = 64


def _params(*sem):
    return pltpu.CompilerParams(dimension_semantics=sem, vmem_limit_bytes=VMEM_LIMIT_BYTES)


def _pick(n, candidates):
    for c in candidates:
        if n % c == 0:
            return c
    raise ValueError(f"no tile for {n} among {candidates}")


def _mm_body(*refs, nk, a_act, epi, n_rows, n_fulls, tm):
    a_ref, b_ref = refs[0], refs[1]
    row_refs = refs[2:2 + n_rows]
    full_refs = refs[2 + n_rows:2 + n_rows + n_fulls]
    o_ref = refs[2 + n_rows + n_fulls]
    acc_ref = refs[3 + n_rows + n_fulls]
    k = pl.program_id(2)
    av = a_ref[...]
    if a_act is not None:
        av = a_act(av.astype(F32))
    part = jnp.dot(av.astype(BF16), b_ref[...].astype(BF16), preferred_element_type=F32)

    def finish(acc):
        if epi is not None:
            row0 = pl.program_id(0) * tm
            acc = epi(acc, row0, [r[...] for r in row_refs], [f[...] for f in full_refs])
        o_ref[...] = acc.astype(o_ref.dtype)

    if nk == 1:
        finish(part)
        return

    @pl.when(k == 0)
    def _():
        acc_ref[...] = part

    @pl.when(jnp.logical_and(k > 0, k < nk - 1))
    def _():
        acc_ref[...] += part

    @pl.when(k == nk - 1)
    def _():
        finish(acc_ref[...] + part)


def _matmul(a, b, *, name, tm, tn, tk, out_dtype=F32, out_tn=None, a_col_off=0, a_act=None, b_lead=None,
            rows=(), fulls=(), epi=None):
    K, N = b.shape[-2:]
    M = a.shape[0]
    out_tn = tn if out_tn is None else out_tn
    assert M % tm == 0 and N % tn == 0 and K % tk == 0 and a_col_off % tk == 0
    nk = K // tk
    ko = a_col_off // tk
    if b.ndim == 3:
        b_spec = pl.BlockSpec((None, tk, tn), lambda i, j, k: (b_lead, k, j))
    else:
        b_spec = pl.BlockSpec((tk, tn), lambda i, j, k: (k, j))
    in_specs = [pl.BlockSpec((tm, tk), lambda i, j, k: (i, k + ko)), b_spec]
    for r in rows:
        in_specs.append(pl.BlockSpec((r.shape[0], tn), lambda i, j, k: (0, j)))
    for _ in fulls:
        in_specs.append(pl.BlockSpec((tm, out_tn), lambda i, j, k: (i, j)))
    body = functools.partial(_mm_body, nk=nk, a_act=a_act, epi=epi, n_rows=len(rows),
                             n_fulls=len(fulls), tm=tm)
    return pl.pallas_call(
        body,
        grid=(M // tm, N // tn, nk),
        in_specs=in_specs,
        out_specs=pl.BlockSpec((tm, out_tn), lambda i, j, k: (i, j)),
        out_shape=jax.ShapeDtypeStruct((M, N // tn * out_tn), out_dtype),
        scratch_shapes=[pltpu.VMEM((tm, tn) if nk > 1 else (SUBLANES, LANES), F32)],
        compiler_params=_params("parallel", "parallel", "arbitrary"),
        name=name,
    )(a, b, *rows, *fulls)


def _modulate_body(x_ref, g_ref, mod_ref, *rest, shift_idx, scale_idx, with_router):
    xv = x_ref[...]
    y = xv * lax.rsqrt(jnp.mean(xv * xv, axis=-1, keepdims=True) + EPS) * g_ref[...]
    h = y * (1.0 + mod_ref[0, scale_idx:scale_idx + 1, :]) + mod_ref[0, shift_idx:shift_idx + 1, :]
    if with_router:
        rw_ref, o_ref, lg_ref = rest
        lg_ref[...] = jnp.dot(h, rw_ref[...], precision=HIGHEST, preferred_element_type=F32)
    else:
        (o_ref,) = rest
    o_ref[...] = h.astype(o_ref.dtype)


def _modulate(x, g, mods, shift_idx, scale_idx, n_ctx, router_w=None):
    T, D = x.shape
    tm = ROW_BLOCK
    assert T % tm == 0 and n_ctx % tm == 0
    nc = n_ctx // tm
    in_specs = [
        pl.BlockSpec((tm, D), lambda i: (i, 0)),
        pl.BlockSpec((1, D), lambda i: (0, 0)),
        pl.BlockSpec((1, mods.shape[1], D), lambda i: (jnp.where(i < nc, 0, 1), 0, 0)),
    ]
    out_specs = [pl.BlockSpec((tm, D), lambda i: (i, 0))]
    out_shape = [jax.ShapeDtypeStruct((T, D), BF16)]
    args = [x, g.reshape(1, D), mods]
    if router_w is not None:
        in_specs.append(pl.BlockSpec((D, LANES), lambda i: (0, 0)))
        out_specs.append(pl.BlockSpec((tm, LANES), lambda i: (i, 0)))
        out_shape.append(jax.ShapeDtypeStruct((T, LANES), F32))
        args.append(router_w)
    body = functools.partial(_modulate_body, shift_idx=shift_idx, scale_idx=scale_idx,
                             with_router=router_w is not None)
    outs = pl.pallas_call(body, grid=(T // tm,), in_specs=in_specs, out_specs=out_specs,
                          out_shape=out_shape, compiler_params=_params("parallel"), name="modulate")(*args)
    return outs if router_w is not None else outs[0]


def _rmsnorm_cols_body(x_ref, g_ref, o_ref):
    xv = x_ref[...]
    y = xv * lax.rsqrt(jnp.mean(xv * xv, axis=-1, keepdims=True) + EPS) * g_ref[...]
    o_ref[...] = y.astype(o_ref.dtype)


def _rmsnorm_cols(z, col_off, width, g, out_dtype=BF16):
    T = z.shape[0]
    tm = ROW_BLOCK
    assert col_off % width == 0 and T % tm == 0
    cb = col_off // width
    return pl.pallas_call(
        _rmsnorm_cols_body,
        name="rmsnorm_cols",
        grid=(T // tm,),
        in_specs=[pl.BlockSpec((tm, width), lambda i: (i, cb)), pl.BlockSpec((1, width), lambda i: (0, 0))],
        out_specs=pl.BlockSpec((tm, width), lambda i: (i, 0)),
        out_shape=jax.ShapeDtypeStruct((T, width), out_dtype),
        compiler_params=_params("parallel"),
    )(z, g.reshape(1, width))


def _conv_body(gb_ref, gc_ref, u_ref, gcp_ref, up_ref, gcn_ref, un_ref, w_ref, o_ref, *, tb, nb, nc):
    i = pl.program_id(0)
    p = gc_ref[...] * u_ref[...]
    prev_row = (gcp_ref[...] * up_ref[...])[SUBLANES - 1:SUBLANES, :]
    next_row = (gcn_ref[...] * un_ref[...])[0:1, :]
    starts = jnp.logical_or(i == 0, i == nc)
    ends = jnp.logical_or(i == nc - 1, i == nb - 1)
    prev_row = jnp.where(starts, 0.0, prev_row)
    next_row = jnp.where(ends, 0.0, next_row)
    rid = lax.broadcasted_iota(jnp.int32, p.shape, 0)
    xm1 = jnp.where(rid == 0, prev_row, pltpu.roll(p, 1, axis=0))
    xp1 = jnp.where(rid == tb - 1, next_row, pltpu.roll(p, tb - 1, axis=0))
    w = w_ref[...]
    o_ref[...] = (gb_ref[...] * (w[0:1] * xm1 + w[1:2] * p + w[2:3] * xp1)).astype(o_ref.dtype)


def _short_conv(z, gb_off, gc_off, u_off, conv_w, n_ctx):
    T = z.shape[0]
    C = conv_w.shape[1]
    tb, tn = ROW_BLOCK, 512
    nb, nc = T // tb, n_ctx // tb
    hb = tb // SUBLANES
    last_h = T // SUBLANES - 1
    cur = lambda off: pl.BlockSpec((tb, tn), lambda i, j: (i, off // tn + j))
    prv = lambda off: pl.BlockSpec((SUBLANES, tn), lambda i, j: (jnp.maximum(i * hb - 1, 0), off // tn + j))
    nxt = lambda off: pl.BlockSpec((SUBLANES, tn), lambda i, j: (jnp.minimum((i + 1) * hb, last_h), off // tn + j))
    return pl.pallas_call(
        functools.partial(_conv_body, tb=tb, nb=nb, nc=nc),
        name="short_conv",
        grid=(nb, C // tn),
        in_specs=[cur(gb_off), cur(gc_off), cur(u_off), prv(gc_off), prv(u_off), nxt(gc_off), nxt(u_off),
                  pl.BlockSpec((CONV_K, tn), lambda i, j: (0, j))],
        out_specs=pl.BlockSpec((tb, tn), lambda i, j: (i, j)),
        out_shape=jax.ShapeDtypeStruct((T, C), BF16),
        compiler_params=_params("parallel", "parallel"),
    )(z, z, z, z, z, z, z, conv_w)


def _dot_hi(a, b):
    return jnp.dot(a, b, precision=HIGHEST, preferred_element_type=F32)


def _dot_nt_hi(a, b):
    return lax.dot_general(a, b, (((1,), (1,)), ((), ())), precision=HIGHEST, preferred_element_type=F32)


def _dot_tn_hi(a, b):
    return lax.dot_general(a, b, (((0,), (0,)), ((), ())), precision=HIGHEST, preferred_element_type=F32)


NN = (((1,), (0,)), ((), ()))
NT = (((1,), (1,)), ((), ()))
TN = (((0,), (0,)), ((), ()))

SCAN_PIECES_LOGW = 3
SCAN_PIECES_GRAM = 1
SCAN_PIECES_INV = 1
SCAN_PIECES_OUT = 1


def _pieces(x, n):
    out = []
    for i in range(n):
        p = x.astype(BF16)
        out.append(p)
        if i + 1 < n:
            x = x - p.astype(F32)
    return out


def _pdot(ap, bp, dims=NN):
    order = max(len(ap), len(bp))
    acc = None
    for i, x in enumerate(ap):
        for j, y in enumerate(bp):
            if i + j < order:
                t = lax.dot_general(x, y, dims, preferred_element_type=F32)
                acc = t if acc is None else acc + t
    return acc


def _pcat(parts, axis):
    return [jnp.concatenate(ps, axis=axis) for ps in zip(*parts)]


def _scan_stages(C, masks):
    n = 2 * C
    in_h0, strict, incl, eye, tri, ones_c, diag_blocks, off_blocks = masks
    pg, pi, po = SCAN_PIECES_GRAM, SCAN_PIECES_INV, SCAN_PIECES_OUT

    def stack(x):
        return jnp.concatenate([jnp.where(in_h0, x, 0.0), jnp.where(in_h0, 0.0, x)], axis=0)

    def s_cum(d):
        for nm in ("lw", "r", "k", "v", "a", "b"):
            d[nm + "_s"] = stack(d[nm])
        d["cum"] = _pdot([tri], _pieces(d["lw_s"], SCAN_PIECES_LOGW))
        d["tot_col"] = _pdot(_pieces(d["lw"], SCAN_PIECES_LOGW), [ones_c], TN)

    def s_exp(d):
        cum = d["cum"]
        tot = jnp.sum(d["lw"], axis=0, keepdims=True)
        inv, fin = jnp.exp(-cum), jnp.exp(tot - cum)
        d["r_hat"] = d["r_s"] * jnp.exp(cum)
        d["a_hat_p"] = _pieces(d["a_s"] * jnp.exp(cum - d["lw_s"]), max(pg, po))
        d["r_hat_p"] = _pieces(d["r_hat"], pg)
        d["bk_chk_p"] = _pcat([_pieces(d["b_s"] * inv, pg), _pieces(d["k_s"] * inv, pg)], 0)
        d["b_til_p"] = _pieces(d["b_s"] * fin, po)
        d["bk_til_p"] = _pcat([d["b_til_p"], _pieces(d["k_s"] * fin, po)], 0)
        d["v_p"] = _pieces(d["v_s"], po)
        d["decay"] = jnp.exp(d["tot_col"])

    def s_gram(d):
        g = _pdot(_pcat([d["a_hat_p"][:pg], d["r_hat_p"]], 0), d["bk_chk_p"], NT)
        d["n_ab"] = jnp.where(strict, g[:n, :n], 0.0)
        d["m_ak_p"] = _pieces(jnp.where(strict, g[:n, n:], 0.0), po)
        d["m_rb_p"] = _pieces(jnp.where(incl, g[n:, :n], 0.0), po)
        d["m_rk_p"] = _pieces(jnp.where(incl, g[n:, n:], 0.0), po)
        nd = jnp.where(diag_blocks, d["n_ab"], 0.0)
        d["nd_p"] = _pieces(nd, pi)
        d["t"] = eye + nd

    def s_sq1(d):
        d["pw"] = _pdot(d["nd_p"], d["nd_p"])

    def s_ap1(d):
        d["pw_p"] = _pieces(d["pw"], pi)
        d["t"] = d["t"] + _pdot(d["pw_p"], _pieces(d["t"], pi))

    def s_sq2(d):
        d["pw_p"] = _pieces(_pdot(d["pw_p"], d["pw_p"]), pi)

    def s_ap2(d):
        d["t"] = d["t"] + _pdot(d["pw_p"], _pieces(d["t"], pi))

    def s_merge_a(off_mask):
        def f(d):
            d["t_p"] = _pieces(d["t"], pi)
            d["ot_p"] = _pieces(_pdot(_pieces(jnp.where(off_mask, d["n_ab"], 0.0), pi), d["t_p"]), pi)
        return f

    def s_merge_b(d):
        d["t"] = d["t"] + _pdot(d["t_p"], d["ot_p"])

    def s_abar(d):
        d["t_p"] = _pieces(d["t"], po)
        d["abar_p"] = _pieces(_pdot(d["t_p"], d["a_hat_p"][:po]), po)
        d["mv_p"] = _pieces(_pdot(d["m_ak_p"], d["v_p"]), po)

    def s_u0(d):
        d["uv_p"] = _pcat([_pieces(_pdot(d["t_p"], d["mv_p"]), po), d["v_p"]], 0)
        d["rbar_p"] = _pieces(d["r_hat"] + _pdot(d["m_rb_p"], d["abar_p"]), po)
        d["phi_p"] = _pieces(_pdot(d["b_til_p"], d["abar_p"], TN), po)

    def s_out(d):
        d["y0"] = _pdot(_pcat([d["m_rb_p"], d["m_rk_p"]], 1), d["uv_p"])
        d["s0"] = _pdot(d["bk_til_p"], d["uv_p"], TN)

    stages = [s_cum, s_exp, s_gram, s_sq1, s_ap1, s_sq2, s_ap2]
    for off_mask in off_blocks:
        stages += [s_merge_a(off_mask), s_merge_b]
    return stages + [s_abar, s_u0, s_out]


def _scan_apply(d, state):
    C = d["r"].shape[0]
    st_p = _pieces(state, SCAN_PIECES_OUT)
    y = _pdot(d["rbar_p"], st_p) + d["y0"]
    return y[:C] + y[C:], d["decay"] * state + _pdot(d["phi_p"], st_p) + d["s0"]


def _scan_masks(C, reverse):
    n = 2 * C
    lane = lax.broadcasted_iota(jnp.int32, (C, LANES), 1)
    row = lax.broadcasted_iota(jnp.int32, (n, n), 0)
    col = lax.broadcasted_iota(jnp.int32, (n, n), 1)
    same = (row // C) == (col // C)
    before = (col > row) if reverse else (col < row)
    strict = jnp.logical_and(same, before)
    incl = jnp.logical_and(same, jnp.logical_or(before, col == row))
    eye = jnp.where(row == col, 1.0, 0.0)
    tri = jnp.where(incl, 1.0, 0.0).astype(BF16)
    ones_c = jnp.ones((C, LANES), BF16)
    blk = 8
    diag_blocks = (row // blk) == (col // blk)
    off_blocks = []
    while blk < C:
        off_blocks.append(jnp.logical_and((row // (2 * blk)) == (col // (2 * blk)), (row // blk) != (col // blk)))
        blk *= 2
    return lane < RWKV_HEAD, strict, incl, eye, tri, ones_c, diag_blocks, off_blocks


def _scan_body(rf, kf, vf, lwf, icf, rb, kb, vb, lwb, icb, kk_ref, ka_ref, seg_ref, yf_ref, yb_ref, sf_ref, sb_ref, *, tb):
    @pl.when(pl.program_id(1) == 0)
    def _():
        sf_ref[...] = jnp.zeros_like(sf_ref)
        sb_ref[...] = jnp.zeros_like(sb_ref)

    k_k, k_a, seg = kk_ref[...], ka_ref[...], seg_ref[...]
    nch = tb // SCAN_CHUNK
    dirs = []
    for refs, y_ref, s_ref, reverse in ((rf, kf, vf, lwf, icf), yf_ref, sf_ref, False), ((rb, kb, vb, lwb, icb), yb_ref, sb_ref, True):
        r, k, v, lw, ic = (t[...] for t in refs)
        kk = k * k_k
        kk = kk * lax.rsqrt(_dot_hi(kk * kk, seg) + 1e-12)
        a, b = -kk, kk * ic
        kd = k * (1.0 + (ic - 1.0) * k_a)
        chunks = []
        for c in (range(nch - 1, -1, -1) if reverse else range(nch)):
            sl = slice(c * SCAN_CHUNK, (c + 1) * SCAN_CHUNK)
            chunks.append(dict(r=r[sl], k=kd[sl], v=v[sl], a=a[sl], b=b[sl], lw=lw[sl], rows=sl))
        dirs.append((chunks, _scan_stages(SCAN_CHUNK, _scan_masks(SCAN_CHUNK, reverse)), y_ref, s_ref))

    for step in range(len(dirs[0][1])):
        for chunks, stages, _, _ in dirs:
            for d in chunks:
                stages[step](d)
    states = [s_ref[...] for _, _, _, s_ref in dirs]
    for c in range(nch):
        for i, (chunks, _, y_ref, _) in enumerate(dirs):
            y, states[i] = _scan_apply(chunks[c], states[i])
            y_ref[chunks[c]["rows"], :] = y
    for (_, _, _, s_ref), state in zip(dirs, states):
        s_ref[...] = state


def _rwkv_scan(z, r_off, k_off, v_off, lw_f, ic_f, lw_b, ic_b, k_k, k_a, n_ctx):
    T = z.shape[0]
    W = lw_f.shape[1]
    tb = ROW_BLOCK
    assert n_ctx == tb and T % tb == 0
    nb = T // tb
    npair = W // LANES
    fwd = lambda i: i
    bwd = lambda i: jnp.where(i == 0, 0, nb - i)
    zspec = lambda off, o: pl.BlockSpec((tb, LANES), lambda p, i: (o(i), off // LANES + p))
    wspec = lambda o: pl.BlockSpec((tb, LANES), lambda p, i: (o(i), p))
    par = pl.BlockSpec((1, LANES), lambda p, i: (0, p))
    lane = np.arange(LANES)
    seg = jnp.asarray((lane[:, None] // RWKV_HEAD == lane[None, :] // RWKV_HEAD).astype(np.float32))
    return pl.pallas_call(
        functools.partial(_scan_body, tb=tb),
        name="rwkv_scan",
        grid=(npair, nb),
        in_specs=[zspec(r_off, fwd), zspec(k_off, fwd), zspec(v_off, fwd), wspec(fwd), wspec(fwd),
                  zspec(r_off, bwd), zspec(k_off, bwd), zspec(v_off, bwd), wspec(bwd), wspec(bwd),
                  par, par, pl.BlockSpec((LANES, LANES), lambda p, i: (0, 0))],
        out_specs=[wspec(fwd), wspec(bwd)],
        out_shape=[jax.ShapeDtypeStruct((T, W), F32)] * 2,
        scratch_shapes=[pltpu.VMEM((LANES, LANES), F32)] * 2,
        compiler_params=_params("parallel", "arbitrary"),
    )(z, z, z, lw_f, ic_f, z, z, z, lw_b, ic_b, k_k.reshape(1, W), k_a.reshape(1, W), seg)


def _rwkv_post_body(yf, yb, r, k, v, icf, icb, gate, ka, rk, lnx, seg_ref, o_ref):
    seg = seg_ref[...]
    inv_n = 1.0 / RWKV_HEAD
    wkv = yf[...] + yb[...]
    yc = wkv - _dot_hi(wkv, seg) * inv_n
    yn = yc * lax.rsqrt(_dot_hi(yc * yc, seg) * inv_n + GN_EPS) * lnx[...]
    kv, kav = k[...], ka[...]
    kd_sum = kv * (1.0 + (icf[...] - 1.0) * kav) + kv * (1.0 + (icb[...] - 1.0) * kav)
    bonus = _dot_hi(r[...] * kd_sum * rk[...], seg) * v[...]
    o_ref[...] = ((yn + bonus) * gate[...]).astype(o_ref.dtype)


def _rwkv_post(y_f, y_b, z, r_off, k_off, v_off, ic_f, ic_b, gate, k_a, r_k, ln_x):
    T, W = y_f.shape
    tb, tn = ROW_BLOCK, 256
    blk = pl.BlockSpec((tb, tn), lambda i, j: (i, j))
    zspec = lambda off: pl.BlockSpec((tb, tn), lambda i, j: (i, off // tn + j))
    par = pl.BlockSpec((1, tn), lambda i, j: (0, j))
    lane = np.arange(tn)
    seg = jnp.asarray((lane[:, None] // RWKV_HEAD == lane[None, :] // RWKV_HEAD).astype(np.float32))
    return pl.pallas_call(
        _rwkv_post_body,
        name="rwkv_post",
        grid=(T // tb, W // tn),
        in_specs=[blk, blk, zspec(r_off), zspec(k_off), zspec(v_off), blk, blk, blk, par, par, par,
                  pl.BlockSpec((tn, tn), lambda i, j: (0, 0))],
        out_specs=blk,
        out_shape=jax.ShapeDtypeStruct((T, W), BF16),
        compiler_params=_params("parallel", "parallel"),
    )(y_f, y_b, z, z, z, ic_f, ic_b, gate, k_a.reshape(1, W), r_k.reshape(1, W), ln_x.reshape(1, W), seg)


def _dft_cols_body(u_ref, cs_ref, o_ref):
    G = FOURIER_GROUP
    for g in range(u_ref.shape[1] // G):
        pq = jnp.dot(u_ref[:, g * G:(g + 1) * G].astype(BF16), cs_ref[...], preferred_element_type=F32)
        o_ref[0, :, g * G:(g + 1) * G] = pq[:, :G].astype(o_ref.dtype)
        o_ref[1, :, g * G:(g + 1) * G] = pq[:, G:].astype(o_ref.dtype)


def _fourier_mix(z, n_ctx, width):
    T = z.shape[0] - n_ctx
    tm = ROW_BLOCK
    ro = n_ctx // tm
    G = FOURIER_GROUP
    tn = _pick(width, (1024, 512, G))
    c = np.arange(G)
    ang_c = 2.0 * np.pi * ((c[:, None] * c[None, :]) % G) / G
    cs = jnp.asarray(np.concatenate([np.cos(ang_c), np.sin(ang_c)], axis=1), BF16)
    pq = pl.pallas_call(
        _dft_cols_body,
        name="dft_cols",
        grid=(T // tm, width // tn),
        in_specs=[pl.BlockSpec((tm, tn), lambda i, g: (i + ro, g)), pl.BlockSpec((G, 2 * G), lambda i, g: (0, 0))],
        out_specs=pl.BlockSpec((2, tm, tn), lambda i, g: (0, i, g)),
        out_shape=jax.ShapeDtypeStruct((2, T, width), BF16),
        compiler_params=_params("parallel", "parallel"),
    )(z, cs)
    t = jnp.arange(T, dtype=jnp.int32)
    ang = ((t[:, None] * t[None, :]) % T).astype(F32) * (2.0 * math.pi / T)
    dft = jnp.concatenate([jnp.cos(ang), -jnp.sin(ang)], axis=1).astype(BF16)
    scale = 1.0 / math.sqrt(T * G)
    tmm = _pick(T, (1024, 512, 256))
    return _matmul(dft, pq.reshape(2 * T, width), name="dft_rows", tm=tmm, tn=_pick(width, (1024, 512)), tk=_pick(2 * T, (1024, 512)),
                   out_dtype=BF16, epi=lambda acc, row0, rows, fulls: acc * scale)


def _rope_body(x_ref, tab_ref, o_ref, *, scale, keep_dup):
    xt = x_ref[...] * tab_ref[...]
    y = xt + pltpu.roll(xt, QK_ROPE, axis=1)
    if not keep_dup:
        lane = lax.broadcasted_iota(jnp.int32, y.shape, 1)
        y = jnp.where(lane < QK_ROPE, y, 0.0)
    o_ref[...] = (y * scale).astype(o_ref.dtype)


def _rope(z, col_off, n_heads, tab, scale):
    T = z.shape[0]
    tm = ROW_BLOCK
    cb = col_off // LANES
    return pl.pallas_call(
        functools.partial(_rope_body, scale=scale, keep_dup=False),
        name="rope",
        grid=(T // tm, n_heads),
        in_specs=[pl.BlockSpec((tm, LANES), lambda i, h: (i, cb + h)), pl.BlockSpec((tm, LANES), lambda i, h: (i, 0))],
        out_specs=pl.BlockSpec((tm, LANES), lambda i, h: (i, h)),
        out_shape=jax.ShapeDtypeStruct((T, n_heads * LANES), BF16),
        compiler_params=_params("parallel", "parallel"),
    )(z, tab)


def _q_final_body(qn_ref, qr_ref, tab_ref, o_ref, *, scale):
    xt = qr_ref[...] * tab_ref[...]
    y = xt + pltpu.roll(xt, QK_ROPE, axis=1)
    lane = lax.broadcasted_iota(jnp.int32, y.shape, 1)
    o_ref[:, :QK_NOPE] = (qn_ref[...] * scale).astype(o_ref.dtype)
    o_ref[:, QK_NOPE:] = (jnp.where(lane < QK_ROPE, y, 0.0) * scale).astype(o_ref.dtype)


def _q_final(q, tab, scale):
    T = q.shape[0]
    tm = _pick(T, (1024, 512, 256))
    blk = lambda off: pl.BlockSpec((tm, LANES), lambda i, h: (i, off + h))
    return pl.pallas_call(
        functools.partial(_q_final_body, scale=scale),
        name="q_final",
        grid=(T // tm, MLA_HEADS),
        in_specs=[blk(0), blk(MLA_HEADS), pl.BlockSpec((tm, LANES), lambda i, h: (i, 0))],
        out_specs=pl.BlockSpec((tm, 2 * LANES), lambda i, h: (i, h)),
        out_shape=jax.ShapeDtypeStruct((T, MLA_HEADS * 2 * LANES), BF16),
        compiler_params=_params("parallel", "parallel"),
    )(q, q, tab)


ATTN_SUBTILES = 4


def _attn_body(q_ref, kn_ref, kr_ref, v_ref, o_ref, m_ref, acc_ref, *, nk):
    j = pl.program_id(2)

    @pl.when(j == 0)
    def _():
        m_ref[...] = jnp.full_like(m_ref, -jnp.inf)
        acc_ref[...] = jnp.zeros_like(acc_ref)

    k_cat = jnp.concatenate([kn_ref[...], kr_ref[...]], axis=1)
    v_cat = jnp.concatenate([v_ref[...], jnp.ones(v_ref.shape, BF16)], axis=1)
    ts = q_ref.shape[0] // ATTN_SUBTILES

    def logits(i):
        return lax.dot_general(q_ref[i * ts:(i + 1) * ts, :], k_cat, NT, preferred_element_type=F32)

    def update(i, s):
        rows = slice(i * ts, (i + 1) * ts)
        m_old = m_ref[rows, :]
        m_new = jnp.maximum(m_old, jnp.max(s, axis=-1, keepdims=True))
        p = jnp.exp2(s - m_new).astype(BF16)
        acc_ref[rows, :] = jnp.exp2(m_old - m_new) * acc_ref[rows, :] + jnp.dot(p, v_cat, preferred_element_type=F32)
        m_ref[rows, :] = m_new

    s_prev = logits(0)
    for i in range(1, ATTN_SUBTILES):
        s_next = logits(i)
        update(i - 1, s_prev)
        s_prev = s_next
    update(ATTN_SUBTILES - 1, s_prev)

    @pl.when(j == nk - 1)
    def _():
        acc = acc_ref[...]
        o_ref[...] = (acc[:, :V_HEAD] / acc[:, V_HEAD:]).astype(o_ref.dtype)


def _attention(q, kv, kr, n_ctx):
    T = kv.shape[0]
    Tq = q.shape[0]
    tq = _pick(Tq, (1024, 512, 256))
    tk = _pick(T, (1408, 768, 256))
    nk = T // tk
    return pl.pallas_call(
        functools.partial(_attn_body, nk=nk),
        name="mla_attention",
        grid=(MLA_HEADS, Tq // tq, nk),
        in_specs=[pl.BlockSpec((tq, 2 * LANES), lambda h, i, j: (i, h)),
                  pl.BlockSpec((tk, LANES), lambda h, i, j: (j, 2 * h)),
                  pl.BlockSpec((tk, LANES), lambda h, i, j: (j, 0)),
                  pl.BlockSpec((tk, LANES), lambda h, i, j: (j, 2 * h + 1))],
        out_specs=pl.BlockSpec((tq, LANES), lambda h, i, j: (i, h)),
        out_shape=jax.ShapeDtypeStruct((Tq, MLA_HEADS * V_HEAD), BF16),
        scratch_shapes=[pltpu.VMEM((tq, 1), F32), pltpu.VMEM((tq, 2 * V_HEAD), F32)],
        compiler_params=_params("parallel", "parallel", "arbitrary"),
    )(q, kv, kr, kv)


def _moe_up_body(be_ref, x_ref, wg_ref, wu_ref, o_ref, wgb_ref, wub_ref):
    b = pl.program_id(1)
    changed = jnp.logical_or(b == 0, be_ref[b] != be_ref[jnp.maximum(b - 1, 0)])

    @pl.when(changed)
    def _():
        wgb_ref[...] = wg_ref[...].astype(BF16)
        wub_ref[...] = wu_ref[...].astype(BF16)

    xv = x_ref[...]
    gate = jnp.dot(xv, wgb_ref[...], preferred_element_type=F32)
    up = jnp.dot(xv, wub_ref[...], preferred_element_type=F32)
    o_ref[...] = (gate * jax.nn.sigmoid(gate) * up).astype(o_ref.dtype)


def _moe_up(xb, block_e, wg, wu, layer):
    n_rows, D = xb.shape
    DE = wg.shape[3]
    tb, tn = MOE_BLOCK, 512
    w_spec = pl.BlockSpec((None, None, D, tn), lambda n, b, be: (layer, be[b], 0, n))
    return pl.pallas_call(
        _moe_up_body,
        name="moe_up",
        grid_spec=pltpu.PrefetchScalarGridSpec(
            num_scalar_prefetch=1,
            grid=(DE // tn, n_rows // tb),
            in_specs=[pl.BlockSpec((tb, D), lambda n, b, be: (b, 0)), w_spec, w_spec],
            out_specs=pl.BlockSpec((tb, tn), lambda n, b, be: (b, n)),
            scratch_shapes=[pltpu.VMEM((D, tn), BF16), pltpu.VMEM((D, tn), BF16)]),
        out_shape=jax.ShapeDtypeStruct((n_rows, DE), BF16),
        compiler_params=_params("arbitrary", "arbitrary"),
    )(block_e, xb, wg, wu)


def _moe_down_body(be_ref, h_ref, wd_ref, sw_ref, o_ref, wdb_ref):
    b = pl.program_id(1)
    changed = jnp.logical_or(b == 0, be_ref[b] != be_ref[jnp.maximum(b - 1, 0)])

    @pl.when(changed)
    def _():
        wdb_ref[...] = wd_ref[...].astype(BF16)

    o_ref[...] = jnp.dot(h_ref[...], wdb_ref[...], preferred_element_type=F32) * sw_ref[...]


def _moe_down(h, block_e, wd, slot_w, layer):
    n_rows, DE = h.shape
    D = wd.shape[3]
    tb, tn = MOE_BLOCK, 2048
    return pl.pallas_call(
        _moe_down_body,
        name="moe_down",
        grid_spec=pltpu.PrefetchScalarGridSpec(
            num_scalar_prefetch=1,
            grid=(D // tn, n_rows // tb),
            in_specs=[pl.BlockSpec((tb, DE), lambda n, b, be: (b, 0)),
                      pl.BlockSpec((None, None, DE, tn), lambda n, b, be: (layer, be[b], 0, n)),
                      pl.BlockSpec((tb, 1), lambda n, b, be: (b, 0))],
            out_specs=pl.BlockSpec((tb, tn), lambda n, b, be: (b, n)),
            scratch_shapes=[pltpu.VMEM((DE, tn), BF16)]),
        out_shape=jax.ShapeDtypeStruct((n_rows, D), F32),
        compiler_params=_params("arbitrary", "arbitrary"),
    )(block_e, h, wd, slot_w.reshape(n_rows, 1))


def _route(logits, router_b):
    T = logits.shape[0]
    s = jax.nn.sigmoid(logits)
    sel = (s + router_b.astype(F32)).reshape(T, N_GROUPS, EXPERTS_PER_GROUP)

    def top2(v):
        pos = jnp.arange(v.shape[-1])
        i0 = jnp.argmax(v, axis=-1)
        rest = jnp.where(pos == i0[..., None], -jnp.inf, v)
        return i0, jnp.argmax(rest, axis=-1), jnp.max(v, axis=-1), jnp.max(rest, axis=-1)

    _, _, v0, v1 = top2(sel)
    grp = jnp.argmax(v0 + v1, axis=-1)
    in_grp = jnp.arange(N_GROUPS)[None, :, None] == grp[:, None, None]
    loc0, loc1, _, _ = top2(jnp.sum(jnp.where(in_grp, sel, 0.0), axis=1))
    idx = grp[:, None] * EXPERTS_PER_GROUP + jnp.stack([loc0, loc1], axis=-1)
    picked = jnp.arange(N_EXPERTS)[None, None, :] == idx[:, :, None]
    wts = jnp.sum(jnp.where(picked, s[:, None, :], 0.0), axis=-1)
    return idx, wts / jnp.sum(wts, axis=-1, keepdims=True)


def _moe_ffn(x, h, logits, router_b, wg, wu, wd, sg, su, sd, layer, gates, n_ctx, final_g=None):
    T, D = h.shape
    idx, wts = _route(logits[:, :N_EXPERTS], router_b)
    A = T * TOP_K
    flat_e = idx.reshape(A)
    onehot = (flat_e[:, None] == jnp.arange(N_EXPERTS)[None, :]).astype(jnp.int32)
    csum = jnp.cumsum(onehot, axis=0)
    counts = csum[-1]
    rank = jnp.sum(onehot * csum, axis=1) - 1
    padded = (counts + MOE_BLOCK - 1) // MOE_BLOCK * MOE_BLOCK
    pad_end = jnp.cumsum(padded)
    pad_start = pad_end - padded
    dest = (jnp.sum(onehot * pad_start[None, :], axis=1) + rank).astype(jnp.int32)
    n_blocks = -(-A // MOE_BLOCK) + N_EXPERTS
    n_slots = n_blocks * MOE_BLOCK
    slot_tok = jnp.full((n_slots,), T, jnp.int32).at[dest].set(jnp.arange(A, dtype=jnp.int32) // TOP_K)
    slot_w = jnp.zeros((n_slots,), F32).at[dest].set(wts.reshape(A))
    xb = jnp.concatenate([h, jnp.zeros((1, D), h.dtype)], axis=0)[slot_tok]
    block_e = jnp.minimum(jnp.sum(pad_end[None, :] <= (jnp.arange(n_blocks) * MOE_BLOCK)[:, None], axis=1),
                          N_EXPERTS - 1).astype(jnp.int32)
    y_slots = _moe_down(_moe_up(xb, block_e, wg, wu, layer), block_e, wd, slot_w, layer)
    slot_of = dest.reshape(T, TOP_K)
    routed = y_slots[slot_of[:, 0]] + y_slots[slot_of[:, 1]]

    DE = sg.shape[2]
    half = SHARED_TN // 2
    w_gu = jnp.stack([sg[layer].reshape(D, DE // half, half), su[layer].reshape(D, DE // half, half)],
                     axis=2).reshape(D, 2 * DE).astype(BF16)

    def swiglu_epi(acc, row0, rows, fulls):
        g = acc[:, :half]
        return g * jax.nn.sigmoid(g) * acc[:, half:]

    hs = _matmul(h, w_gu, name="shared_up", tm=_pick(T, (1408, 1024, 768, 256)), tn=SHARED_TN, out_tn=half,
                 tk=2048, out_dtype=BF16, epi=swiglu_epi)

    def combine_epi(acc, row0, rows, fulls):
        rid = row0 + lax.broadcasted_iota(jnp.int32, acc.shape, 0)
        gate = jnp.where(rid < n_ctx, rows[0][0:1], rows[0][1:2])
        y = fulls[0] + gate * (fulls[1] + acc)
        if final_g is not None:
            y = y * lax.rsqrt(jnp.mean(y * y, axis=-1, keepdims=True) + EPS) * rows[1]
        return y

    rows = (gates,) if final_g is None else (gates, final_g.reshape(1, D))
    return _matmul(hs, sd[layer].astype(BF16), name="shared_down_combine", tm=ROW_BLOCK, tn=D, tk=DE,
                   rows=rows, fulls=(x, routed), epi=combine_epi)


def _silu(x):
    return x * jax.nn.sigmoid(x)


def _softplus(x):
    return jnp.maximum(x, 0.0) + jnp.log(1.0 + jnp.exp(-jnp.abs(x)))


def _adaln(cond2, w, b, layer):
    D = cond2.shape[1]
    a = jnp.zeros((SUBLANES, D), F32).at[:2].set(cond2)
    out = _matmul(a, w, name="adaln", tm=SUBLANES, tn=512, tk=D, a_act=_silu, b_lead=layer,
                  rows=(b[layer].reshape(1, -1),), epi=lambda acc, row0, rows, fulls: acc + rows[0])
    return out[:2].reshape(2, -1, D)


def _gated_residual_epi(n_ctx):
    def epi(acc, row0, rows, fulls):
        rid = row0 + lax.broadcasted_iota(jnp.int32, acc.shape, 0)
        gate = jnp.where(rid < n_ctx, rows[0][0:1], rows[0][1:2])
        return fulls[0] + gate * acc
    return epi


def _pad_cols(w, width):
    return jnp.pad(w, ((0, 0), (0, width - w.shape[1])))


def _pad_rows(w, height):
    return jnp.pad(w, ((0, height - w.shape[0]), (0, 0)))


def _even_mixer(x, h, mods, n_ctx, w_in, w_out, w0, w2, a0, a2, g2, k_k, k_a, r_k, ln_x, conv_w):
    T, D = x.shape
    W = D // 2
    o = np.cumsum((0, W, W, W, DECAY_LORA, DECAY_LORA, ICLR_LORA, ICLR_LORA, GATE_LORA, W, W))
    lora = [_pad_cols(w_in[:, o[i]:o[i + 1]], LANES) for i in range(3, 7)]
    w_in_p = jnp.concatenate([w_in[:, :o[3]], w_in[:, o[8]:], *lora, w_in[:, o[7]:o[8]]], axis=1).astype(BF16)
    tm = _pick(T, (1408, 768, 256))
    z = _matmul(h, w_in_p, name="even_w_in", tm=tm, tn=768, tk=2048)
    r_off, k_off, v_off, gb_off, gc_off, u_off = (i * W for i in range(6))
    lo = 6 * W

    def lora_mm(col, kdim, w, bias, act, epi):
        rows = () if bias is None else (bias.reshape(1, W),)
        return _matmul(z, _pad_rows(w, kdim), name="rwkv_lora", tm=tm, tn=1024, tk=kdim, a_col_off=col, a_act=act, rows=rows, epi=epi)

    decay_epi = lambda acc, row0, rows, fulls: -jnp.exp(-_softplus(-(rows[0] + acc)) - 0.5)
    iclr_epi = lambda acc, row0, rows, fulls: jax.nn.sigmoid(rows[0] + acc)
    lw_f = lora_mm(lo, LANES, w2[0], w0[0], jnp.tanh, decay_epi)
    lw_b = lora_mm(lo + LANES, LANES, w2[1], w0[1], jnp.tanh, decay_epi)
    ic_f = lora_mm(lo + 2 * LANES, LANES, a2[0], a0[0], None, iclr_epi)
    ic_b = lora_mm(lo + 3 * LANES, LANES, a2[1], a0[1], None, iclr_epi)
    gate = lora_mm(lo + 4 * LANES, GATE_LORA, g2, None, jax.nn.sigmoid, None)

    y_f, y_b = _rwkv_scan(z, r_off, k_off, v_off, lw_f, ic_f, lw_b, ic_b, k_k, k_a, n_ctx)
    o_rwkv = _rwkv_post(y_f, y_b, z, r_off, k_off, v_off, ic_f, ic_b, gate, k_a, r_k, ln_x)
    o_conv = _short_conv(z, gb_off, gc_off, u_off, conv_w, n_ctx)
    y = jnp.concatenate([o_rwkv, o_conv], axis=1)
    return _matmul(y, w_out.astype(BF16), name="even_w_out", tm=tm, tn=512, tk=2048, rows=(mods[:, 2],), fulls=(x,),
                   epi=_gated_residual_epi(n_ctx))


def _rope_tables(n_ctx, n_lat):
    rows = n_lat // GRID_W
    row = jnp.repeat(jnp.arange(rows), GRID_W)
    col = jnp.tile(jnp.arange(GRID_W), rows)
    pos = jnp.stack([row, col], axis=-1).astype(F32)
    inv_freq = ROPE_BASE ** (-jnp.arange(ROPE_PAIRS, dtype=F32) / ROPE_PAIRS)
    ang = pos[:, :, None, None] * inv_freq
    shape = (n_lat, 2, 2, ROPE_PAIRS)
    cos = jnp.broadcast_to(jnp.cos(ang), shape).reshape(n_lat, QK_ROPE)
    sin = jnp.broadcast_to(jnp.sin(ang), shape).reshape(n_lat, QK_ROPE)
    cos = jnp.concatenate([jnp.ones((n_ctx, QK_ROPE), F32), cos], axis=0)
    sin = jnp.concatenate([jnp.zeros((n_ctx, QK_ROPE), F32), sin], axis=0)
    return jnp.concatenate([cos, sin], axis=1)


def _rot_cols(w):
    lead = w.shape[:-1]
    wr = w.reshape(*lead, 2, 2, ROPE_PAIRS)
    return jnp.stack([-wr[..., 1, :], wr[..., 0, :]], axis=-2).reshape(*lead, QK_ROPE)


def _odd_mixer(x, h, mods, n_ctx, w_in, w_out, q_norm, w_uq, kv_norm, w_ukv):
    T, D = x.shape
    W = D // 2
    n_lat = T - n_ctx
    kr_w = w_in[:, W + Q_LORA + KV_LORA:]
    w_in_p = jnp.concatenate([w_in, _rot_cols(kr_w)], axis=1).astype(BF16)
    w_in_p = _pad_cols(w_in_p, -(-w_in_p.shape[1] // 768) * 768)
    tm = _pick(T, (1408, 768, 256))
    z = _matmul(h, w_in_p, name="odd_w_in", tm=tm, tn=768, tk=2048)
    qa_off, kva_off, kr_off = W, W + Q_LORA, W + Q_LORA + KV_LORA

    qn = _rmsnorm_cols(z, qa_off, Q_LORA, q_norm)
    kvn = _rmsnorm_cols(z, kva_off, KV_LORA, kv_norm)
    uq = w_uq.reshape(Q_LORA, MLA_HEADS, QK_NOPE + QK_ROPE)
    uq_rope = uq[:, :, QK_NOPE:]
    w_uq_p = jnp.concatenate([uq[:, :, :QK_NOPE].reshape(Q_LORA, -1),
                              jnp.concatenate([uq_rope, _rot_cols(uq_rope)], axis=-1).reshape(Q_LORA, -1)], axis=1).astype(BF16)
    tml = _pick(n_lat, (1024, 512, 256))
    q_lat = _matmul(qn[n_ctx:], w_uq_p, name="mla_uq", tm=tml, tn=1024, tk=Q_LORA)
    kv = _matmul(kvn, w_ukv.astype(BF16), name="mla_ukv", tm=tm, tn=1024, tk=KV_LORA, out_dtype=BF16)

    tab = _rope_tables(n_ctx, n_lat)
    q_scale = SM_SCALE * math.log2(math.e)
    tab_lat = tab[n_ctx:]
    q_fin = _q_final(q_lat, tab_lat, q_scale)
    kr = _rope(z, kr_off, 1, tab, 1.0)
    att = _attention(q_fin, kv, kr, n_ctx)
    four = _fourier_mix(z, n_ctx, W)
    y = jnp.concatenate([four, att], axis=1)
    return _matmul(y, w_out.astype(BF16), name="odd_w_out", tm=tml, tn=1024, tk=2048, rows=(mods[:, 2],), fulls=(x[n_ctx:],),
                   epi=_gated_residual_epi(0))


def kernel(x, c, ctx, c_ctx, ada_w, ada_b, norm1_g, norm2_g, ev_w_in, ev_w_out, ev_w0, ev_w2, ev_a0, ev_a2, ev_g2, ev_k_k, ev_k_a, ev_r_k, ev_ln_x, ev_conv_w, od_w_in, od_w_out, od_q_norm, od_w_uq, od_kv_norm, od_w_ukv, router_w, router_b, moe_wg, moe_wu, moe_wd, shared_wg, shared_wu, shared_wd, final_g):
    B, n_lat, D = x.shape
    n_ctx = ctx.shape[1]
    depth = ada_w.shape[0]
    assert B == 1 and depth == 2 and n_ctx == ROW_BLOCK
    xs = jnp.concatenate([ctx[0], x[0]], axis=0)
    cond2 = jnp.concatenate([c_ctx[None], c], axis=0)
    router_w_p = _pad_cols(router_w.astype(F32), LANES)

    moe_w = (router_b, moe_wg, moe_wu, moe_wd, shared_wg, shared_wu, shared_wd)
    mods = _adaln(cond2, ada_w, ada_b, 0)
    h = _modulate(xs, norm1_g[0], mods, 0, 1, n_ctx)
    xs = _even_mixer(xs, h, mods, n_ctx, ev_w_in[0], ev_w_out[0], ev_w0[0], ev_w2[0], ev_a0[0], ev_a2[0], ev_g2[0],
                     ev_k_k[0], ev_k_a[0], ev_r_k[0], ev_ln_x[0], ev_conv_w[0])
    h, logits = _modulate(xs, norm2_g[0], mods, 3, 4, n_ctx, router_w=router_w_p)
    xs = _moe_ffn(xs, h, logits, *moe_w, 0, mods[:, 5], n_ctx)

    mods = _adaln(cond2, ada_w, ada_b, 1)
    h = _modulate(xs, norm1_g[1], mods, 0, 1, n_ctx)
    xl = _odd_mixer(xs, h, mods, n_ctx, od_w_in[0], od_w_out[0], od_q_norm[0], od_w_uq[0], od_kv_norm[0], od_w_ukv[0])
    h, logits = _modulate(xl, norm2_g[1], mods, 3, 4, 0, router_w=router_w_p)
    out = _moe_ffn(xl, h, logits, *moe_w, 1, mods[:, 5], 0, final_g=final_g)
    return out[None]
```

```python
import functools
import math

import numpy as np
import jax
import jax.numpy as jnp
from jax import lax
from jax.experimental import pallas as pl
from jax.experimental.pallas import tpu as pltpu

F32 = jnp.float32
BF16 = jnp.bfloat16
HIGHEST = lax.Precision.HIGHEST

LANES = 128
SUBLANES = 8
VMEM_LIMIT_BYTES = 56 * 1024 * 1024

EPS = 1e-6
GN_EPS = 64e-5
RWKV_HEAD = 64
DECAY_LORA = 96
ICLR_LORA = 96
GATE_LORA = 256
CONV_K = 3
FOURIER_GROUP = 128
FFT_N2 = 128
MLA_HEADS = 16
QK_NOPE = 128
QK_ROPE = 64
V_HEAD = 128
Q_LORA = 1024
KV_LORA = 512
ROPE_PAIRS = QK_ROPE // 4
ROPE_BASE = 10000.0
GRID_W = 64
SM_SCALE = (QK_NOPE + QK_ROPE) ** -0.5
N_EXPERTS = 16
N_GROUPS = 4
EXPERTS_PER_GROUP = N_EXPERTS // N_GROUPS
TOP_K = 2
MOE_BLOCK = 256
SHARED_TN = 1024
ROW_BLOCK = 256
SCAN_CHUNK = 64


def _params(*sem):
    return pltpu.CompilerParams(dimension_semantics=sem, vmem_limit_bytes=VMEM_LIMIT_BYTES)


def _pick(n, candidates):
    for c in candidates:
        if n % c == 0:
            return c
    raise ValueError(f"no tile for {n} among {candidates}")


def _mm_body(*refs, nk, a_act, epi, n_rows, n_fulls, tm):
    a_ref, b_ref = refs[0], refs[1]
    row_refs = refs[2:2 + n_rows]
    full_refs = refs[2 + n_rows:2 + n_rows + n_fulls]
    o_ref = refs[2 + n_rows + n_fulls]
    acc_ref = refs[3 + n_rows + n_fulls]
    k = pl.program_id(2)
    av = a_ref[...]
    if a_act is not None:
        av = a_act(av.astype(F32))
    part = jnp.dot(av.astype(BF16), b_ref[...].astype(BF16), preferred_element_type=F32)

    def finish(acc):
        if epi is not None:
            row0 = pl.program_id(0) * tm
            acc = epi(acc, row0, [r[...] for r in row_refs], [f[...] for f in full_refs])
        o_ref[...] = acc.astype(o_ref.dtype)

    if nk == 1:
        finish(part)
        return

    @pl.when(k == 0)
    def _():
        acc_ref[...] = part

    @pl.when(jnp.logical_and(k > 0, k < nk - 1))
    def _():
        acc_ref[...] += part

    @pl.when(k == nk - 1)
    def _():
        finish(acc_ref[...] + part)


def _matmul(a, b, *, name, tm, tn, tk, out_dtype=F32, out_tn=None, a_col_off=0, a_act=None, b_lead=None,
            rows=(), fulls=(), epi=None):
    K, N = b.shape[-2:]
    M = a.shape[0]
    out_tn = tn if out_tn is None else out_tn
    assert M % tm == 0 and N % tn == 0 and K % tk == 0 and a_col_off % tk == 0
    nk = K // tk
    ko = a_col_off // tk
    if b.ndim == 3:
        b_spec = pl.BlockSpec((None, tk, tn), lambda i, j, k: (b_lead, k, j))
    else:
        b_spec = pl.BlockSpec((tk, tn), lambda i, j, k: (k, j))
    in_specs = [pl.BlockSpec((tm, tk), lambda i, j, k: (i, k + ko)), b_spec]
    for r in rows:
        in_specs.append(pl.BlockSpec((r.shape[0], tn), lambda i, j, k: (0, j)))
    for _ in fulls:
        in_specs.append(pl.BlockSpec((tm, out_tn), lambda i, j, k: (i, j)))
    body = functools.partial(_mm_body, nk=nk, a_act=a_act, epi=epi, n_rows=len(rows),
                             n_fulls=len(fulls), tm=tm)
    return pl.pallas_call(
        body,
        grid=(M // tm, N // tn, nk),
        in_specs=in_specs,
        out_specs=pl.BlockSpec((tm, out_tn), lambda i, j, k: (i, j)),
        out_shape=jax.ShapeDtypeStruct((M, N // tn * out_tn), out_dtype),
        scratch_shapes=[pltpu.VMEM((tm, tn) if nk > 1 else (SUBLANES, LANES), F32)],
        compiler_params=_params("parallel", "parallel", "arbitrary"),
        name=name,
    )(a, b, *rows, *fulls)


def _modulate_body(x_ref, g_ref, mod_ref, *rest, shift_idx, scale_idx, with_router):
    xv = x_ref[...]
    y = xv * lax.rsqrt(jnp.mean(xv * xv, axis=-1, keepdims=True) + EPS) * g_ref[...]
    h = y * (1.0 + mod_ref[0, scale_idx:scale_idx + 1, :]) + mod_ref[0, shift_idx:shift_idx + 1, :]
    if with_router:
        rw_ref, o_ref, lg_ref = rest
        lg_ref[...] = jnp.dot(h, rw_ref[...], precision=HIGHEST, preferred_element_type=F32)
    else:
        (o_ref,) = rest
    o_ref[...] = h.astype(o_ref.dtype)


def _modulate(x, g, mods, shift_idx, scale_idx, n_ctx, router_w=None):
    T, D = x.shape
    tm = ROW_BLOCK
    assert T % tm == 0 and n_ctx % tm == 0
    nc = n_ctx // tm
    in_specs = [
        pl.BlockSpec((tm, D), lambda i: (i, 0)),
        pl.BlockSpec((1, D), lambda i: (0, 0)),
        pl.BlockSpec((1, mods.shape[1], D), lambda i: (jnp.where(i < nc, 0, 1), 0, 0)),
    ]
    out_specs = [pl.BlockSpec((tm, D), lambda i: (i, 0))]
    out_shape = [jax.ShapeDtypeStruct((T, D), BF16)]
    args = [x, g.reshape(1, D), mods]
    if router_w is not None:
        in_specs.append(pl.BlockSpec((D, LANES), lambda i: (0, 0)))
        out_specs.append(pl.BlockSpec((tm, LANES), lambda i: (i, 0)))
        out_shape.append(jax.ShapeDtypeStruct((T, LANES), F32))
        args.append(router_w)
    body = functools.partial(_modulate_body, shift_idx=shift_idx, scale_idx=scale_idx,
                             with_router=router_w is not None)
    outs = pl.pallas_call(body, grid=(T // tm,), in_specs=in_specs, out_specs=out_specs,
                          out_shape=out_shape, compiler_params=_params("parallel"), name="modulate")(*args)
    return outs if router_w is not None else outs[0]


def _rmsnorm_cols_body(x_ref, g_ref, o_ref):
    xv = x_ref[...]
    y = xv * lax.rsqrt(jnp.mean(xv * xv, axis=-1, keepdims=True) + EPS) * g_ref[...]
    o_ref[...] = y.astype(o_ref.dtype)


def _rmsnorm_cols(z, col_off, width, g, out_dtype=BF16):
    T = z.shape[0]
    tm = ROW_BLOCK
    assert col_off % width == 0 and T % tm == 0
    cb = col_off // width
    return pl.pallas_call(
        _rmsnorm_cols_body,
        name="rmsnorm_cols",
        grid=(T // tm,),
        in_specs=[pl.BlockSpec((tm, width), lambda i: (i, cb)), pl.BlockSpec((1, width), lambda i: (0, 0))],
        out_specs=pl.BlockSpec((tm, width), lambda i: (i, 0)),
        out_shape=jax.ShapeDtypeStruct((T, width), out_dtype),
        compiler_params=_params("parallel"),
    )(z, g.reshape(1, width))


def _conv_body(gb_ref, gc_ref, u_ref, gcp_ref, up_ref, gcn_ref, un_ref, w_ref, o_ref, *, tb, nb, nc):
    i = pl.program_id(0)
    p = gc_ref[...] * u_ref[...]
    prev_row = (gcp_ref[...] * up_ref[...])[SUBLANES - 1:SUBLANES, :]
    next_row = (gcn_ref[...] * un_ref[...])[0:1, :]
    starts = jnp.logical_or(i == 0, i == nc)
    ends = jnp.logical_or(i == nc - 1, i == nb - 1)
    prev_row = jnp.where(starts, 0.0, prev_row)
    next_row = jnp.where(ends, 0.0, next_row)
    rid = lax.broadcasted_iota(jnp.int32, p.shape, 0)
    xm1 = jnp.where(rid == 0, prev_row, pltpu.roll(p, 1, axis=0))
    xp1 = jnp.where(rid == tb - 1, next_row, pltpu.roll(p, tb - 1, axis=0))
    w = w_ref[...]
    o_ref[...] = (gb_ref[...] * (w[0:1] * xm1 + w[1:2] * p + w[2:3] * xp1)).astype(o_ref.dtype)


def _short_conv(z, gb_off, gc_off, u_off, conv_w, n_ctx):
    T = z.shape[0]
    C = conv_w.shape[1]
    tb, tn = ROW_BLOCK, 512
    nb, nc = T // tb, n_ctx // tb
    hb = tb // SUBLANES
    last_h = T // SUBLANES - 1
    cur = lambda off: pl.BlockSpec((tb, tn), lambda i, j: (i, off // tn + j))
    prv = lambda off: pl.BlockSpec((SUBLANES, tn), lambda i, j: (jnp.maximum(i * hb - 1, 0), off // tn + j))
    nxt = lambda off: pl.BlockSpec((SUBLANES, tn), lambda i, j: (jnp.minimum((i + 1) * hb, last_h), off // tn + j))
    return pl.pallas_call(
        functools.partial(_conv_body, tb=tb, nb=nb, nc=nc),
        name="short_conv",
        grid=(nb, C // tn),
        in_specs=[cur(gb_off), cur(gc_off), cur(u_off), prv(gc_off), prv(u_off), nxt(gc_off), nxt(u_off),
                  pl.BlockSpec((CONV_K, tn), lambda i, j: (0, j))],
        out_specs=pl.BlockSpec((tb, tn), lambda i, j: (i, j)),
        out_shape=jax.ShapeDtypeStruct((T, C), BF16),
        compiler_params=_params("parallel", "parallel"),
    )(z, z, z, z, z, z, z, conv_w)


def _dot_hi(a, b):
    return jnp.dot(a, b, precision=HIGHEST, preferred_element_type=F32)


def _dot_nt_hi(a, b):
    return lax.dot_general(a, b, (((1,), (1,)), ((), ())), precision=HIGHEST, preferred_element_type=F32)


def _dot_tn_hi(a, b):
    return lax.dot_general(a, b, (((0,), (0,)), ((), ())), precision=HIGHEST, preferred_element_type=F32)


NN = (((1,), (0,)), ((), ()))
NT = (((1,), (1,)), ((), ()))
TN = (((0,), (0,)), ((), ()))

SCAN_PIECES_GRAM = 1
SCAN_PIECES_INV = 1
SCAN_PIECES_OUT = 1


def _pieces(x, n):
    out = []
    for i in range(n):
        p = x.astype(BF16)
        out.append(p)
        if i + 1 < n:
            x = x - p.astype(F32)
    return out


def _pdot(ap, bp, dims=NN):
    order = max(len(ap), len(bp))
    acc = None
    for i, x in enumerate(ap):
        for j, y in enumerate(bp):
            if i + j < order:
                t = lax.dot_general(x, y, dims, preferred_element_type=F32)
                acc = t if acc is None else acc + t
    return acc


def _pcat(parts, axis):
    return [jnp.concatenate(ps, axis=axis) for ps in zip(*parts)]


def _scan_stages(C, masks):
    n = 2 * C
    reverse, in_h0, row_c, strict, incl, eye, diag_blocks, off_blocks = masks
    pg, pi, po = SCAN_PIECES_GRAM, SCAN_PIECES_INV, SCAN_PIECES_OUT

    def stack(x):
        return jnp.concatenate([jnp.where(in_h0, x, 0.0), jnp.where(in_h0, 0.0, x)], axis=0)

    def s_cum(d):
        for nm in ("lw", "r", "k", "v", "a", "b"):
            d[nm + "_s"] = stack(d[nm])
        x = d["lw"]
        sh = 1
        while sh < C:
            if reverse:
                x = x + jnp.where(row_c < C - sh, pltpu.roll(x, C - sh, axis=0), 0.0)
            else:
                x = x + jnp.where(row_c >= sh, pltpu.roll(x, sh, axis=0), 0.0)
            sh *= 2
        d["cum"] = stack(x)
        d["tot"] = x[0:1] if reverse else x[C - 1:C]

    def s_exp(d):
        cum, tot = d["cum"], d["tot"]
        inv, fin = jnp.exp(-cum), jnp.exp(tot - cum)
        d["r_hat"] = d["r_s"] * jnp.exp(cum)
        d["a_hat_p"] = _pieces(d["a_s"] * jnp.exp(cum - d["lw_s"]), max(pg, po))
        d["r_hat_p"] = _pieces(d["r_hat"], pg)
        d["bk_chk_p"] = _pcat([_pieces(d["b_s"] * inv, pg), _pieces(d["k_s"] * inv, pg)], 0)
        d["b_til_p"] = _pieces(d["b_s"] * fin, po)
        d["bk_til_p"] = _pcat([d["b_til_p"], _pieces(d["k_s"] * fin, po)], 0)
        d["v_p"] = _pieces(d["v_s"], po)
        d["decay"] = jnp.broadcast_to(jnp.exp(tot), (LANES, LANES)).T

    def s_gram(d):
        g = _pdot(_pcat([d["a_hat_p"][:pg], d["r_hat_p"]], 0), d["bk_chk_p"], NT)
        d["n_ab"] = jnp.where(strict, g[:n, :n], 0.0)
        d["m_ak_p"] = _pieces(jnp.where(strict, g[:n, n:], 0.0), po)
        d["m_rb_p"] = _pieces(jnp.where(incl, g[n:, :n], 0.0), po)
        d["m_rk_p"] = _pieces(jnp.where(incl, g[n:, n:], 0.0), po)
        nd = jnp.where(diag_blocks, d["n_ab"], 0.0)
        d["nd_p"] = _pieces(nd, pi)
        d["t"] = eye + nd

    def s_sq1(d):
        d["pw"] = _pdot(d["nd_p"], d["nd_p"])

    def s_ap1(d):
        d["pw_p"] = _pieces(d["pw"], pi)
        d["t"] = d["t"] + _pdot(d["pw_p"], _pieces(d["t"], pi))

    def s_sq2(d):
        d["pw_p"] = _pieces(_pdot(d["pw_p"], d["pw_p"]), pi)

    def s_ap2(d):
        d["t"] = d["t"] + _pdot(d["pw_p"], _pieces(d["t"], pi))

    def s_merge_a(off_mask):
        def f(d):
            d["t_p"] = _pieces(d["t"], pi)
            d["ot_p"] = _pieces(_pdot(_pieces(jnp.where(off_mask, d["n_ab"], 0.0), pi), d["t_p"]), pi)
        return f

    def s_merge_b(d):
        d["t"] = d["t"] + _pdot(d["t_p"], d["ot_p"])

    def s_abar(d):
        d["t_p"] = _pieces(d["t"], po)
        d["abar_p"] = _pieces(_pdot(d["t_p"], d["a_hat_p"][:po]), po)
        d["mv_p"] = _pieces(_pdot(d["m_ak_p"], d["v_p"]), po)

    def s_u0(d):
        d["uv_p"] = _pcat([_pieces(_pdot(d["t_p"], d["mv_p"]), po), d["v_p"]], 0)
        d["rbar_p"] = _pieces(d["r_hat"] + _pdot(d["m_rb_p"], d["abar_p"]), po)
        d["phi_p"] = _pieces(_pdot(d["b_til_p"], d["abar_p"], TN), po)

    def s_out(d):
        d["y0"] = _pdot(_pcat([d["m_rb_p"], d["m_rk_p"]], 1), d["uv_p"])
        d["s0"] = _pdot(d["bk_til_p"], d["uv_p"], TN)

    stages = [s_cum, s_exp, s_gram, s_sq1, s_ap1, s_sq2, s_ap2]
    for off_mask in off_blocks:
        stages += [s_merge_a(off_mask), s_merge_b]
    return stages + [s_abar, s_u0, s_out]


def _scan_apply(d, state):
    C = d["r"].shape[0]
    st_p = _pieces(state, SCAN_PIECES_OUT)
    y = _pdot(d["rbar_p"], st_p) + d["y0"]
    return y[:C] + y[C:], d["decay"] * state + _pdot(d["phi_p"], st_p) + d["s0"]


def _scan_masks(C, reverse):
    n = 2 * C
    lane = lax.broadcasted_iota(jnp.int32, (C, LANES), 1)
    row = lax.broadcasted_iota(jnp.int32, (n, n), 0)
    col = lax.broadcasted_iota(jnp.int32, (n, n), 1)
    same = (row // C) == (col // C)
    before = (col > row) if reverse else (col < row)
    strict = jnp.logical_and(same, before)
    incl = jnp.logical_and(same, jnp.logical_or(before, col == row))
    eye = jnp.where(row == col, 1.0, 0.0)
    row_c = lax.broadcasted_iota(jnp.int32, (C, LANES), 0)
    blk = 8
    diag_blocks = (row // blk) == (col // blk)
    off_blocks = []
    while blk < C:
        off_blocks.append(jnp.logical_and((row // (2 * blk)) == (col // (2 * blk)), (row // blk) != (col // blk)))
        blk *= 2
    return reverse, lane < RWKV_HEAD, row_c, strict, incl, eye, diag_blocks, off_blocks


def _scan_body(rf, kf, vf, lwf, icf, rb, kb, vb, lwb, icb, kk_ref, ka_ref, seg_ref, yf_ref, yb_ref, sf_ref, sb_ref, *, tb):
    @pl.when(pl.program_id(1) == 0)
    def _():
        sf_ref[...] = jnp.zeros_like(sf_ref)
        sb_ref[...] = jnp.zeros_like(sb_ref)

    k_k, k_a, seg = kk_ref[...], ka_ref[...], seg_ref[...]
    nch = tb // SCAN_CHUNK
    dirs = []
    for refs, y_ref, s_ref, reverse in ((rf, kf, vf, lwf, icf), yf_ref, sf_ref, False), ((rb, kb, vb, lwb, icb), yb_ref, sb_ref, True):
        r, k, v, lw, ic = (t[...] for t in refs)
        kk = k * k_k
        kk = kk * lax.rsqrt(_dot_hi(kk * kk, seg) + 1e-12)
        a, b = -kk, kk * ic
        kd = k * (1.0 + (ic - 1.0) * k_a)
        chunks = []
        for c in (range(nch - 1, -1, -1) if reverse else range(nch)):
            sl = slice(c * SCAN_CHUNK, (c + 1) * SCAN_CHUNK)
            chunks.append(dict(r=r[sl], k=kd[sl], v=v[sl], a=a[sl], b=b[sl], lw=lw[sl], rows=sl))
        dirs.append((chunks, _scan_stages(SCAN_CHUNK, _scan_masks(SCAN_CHUNK, reverse)), y_ref, s_ref))

    for step in range(len(dirs[0][1])):
        for chunks, stages, _, _ in dirs:
            for d in chunks:
                stages[step](d)
    states = [s_ref[...] for _, _, _, s_ref in dirs]
    for c in range(nch):
        for i, (chunks, _, y_ref, _) in enumerate(dirs):
            y, states[i] = _scan_apply(chunks[c], states[i])
            y_ref[chunks[c]["rows"], :] = y
    for (_, _, _, s_ref), state in zip(dirs, states):
        s_ref[...] = state


def _rwkv_scan(z, r_off, k_off, v_off, lw_f, ic_f, lw_b, ic_b, k_k, k_a, n_ctx):
    T = z.shape[0]
    W = lw_f.shape[1]
    tb = ROW_BLOCK
    assert n_ctx == tb and T % tb == 0
    nb = T // tb
    npair = W // LANES
    fwd = lambda i: i
    bwd = lambda i: jnp.where(i == 0, 0, nb - i)
    zspec = lambda off, o: pl.BlockSpec((tb, LANES), lambda p, i: (o(i), off // LANES + p))
    wspec = lambda o: pl.BlockSpec((tb, LANES), lambda p, i: (o(i), p))
    par = pl.BlockSpec((1, LANES), lambda p, i: (0, p))
    lane = np.arange(LANES)
    seg = jnp.asarray((lane[:, None] // RWKV_HEAD == lane[None, :] // RWKV_HEAD).astype(np.float32))
    return pl.pallas_call(
        functools.partial(_scan_body, tb=tb),
        name="rwkv_scan",
        grid=(npair, nb),
        in_specs=[zspec(r_off, fwd), zspec(k_off, fwd), zspec(v_off, fwd), wspec(fwd), wspec(fwd),
                  zspec(r_off, bwd), zspec(k_off, bwd), zspec(v_off, bwd), wspec(bwd), wspec(bwd),
                  par, par, pl.BlockSpec((LANES, LANES), lambda p, i: (0, 0))],
        out_specs=[wspec(fwd), wspec(bwd)],
        out_shape=[jax.ShapeDtypeStruct((T, W), F32)] * 2,
        scratch_shapes=[pltpu.VMEM((LANES, LANES), F32)] * 2,
        compiler_params=_params("parallel", "arbitrary"),
    )(z, z, z, lw_f, ic_f, z, z, z, lw_b, ic_b, k_k.reshape(1, W), k_a.reshape(1, W), seg)


def _rwkv_post_body(yf, yb, r, k, v, icf, icb, gate, ka, rk, lnx, seg_ref, o_ref):
    seg = seg_ref[...]
    inv_n = 1.0 / RWKV_HEAD
    wkv = yf[...] + yb[...]
    yc = wkv - _dot_hi(wkv, seg) * inv_n
    yn = yc * lax.rsqrt(_dot_hi(yc * yc, seg) * inv_n + GN_EPS) * lnx[...]
    kv, kav = k[...], ka[...]
    kd_sum = kv * (1.0 + (icf[...] - 1.0) * kav) + kv * (1.0 + (icb[...] - 1.0) * kav)
    bonus = _dot_hi(r[...] * kd_sum * rk[...], seg) * v[...]
    o_ref[...] = ((yn + bonus) * gate[...]).astype(o_ref.dtype)


def _rwkv_post(y_f, y_b, z, r_off, k_off, v_off, ic_f, ic_b, gate, k_a, r_k, ln_x):
    T, W = y_f.shape
    tb, tn = ROW_BLOCK, 256
    blk = pl.BlockSpec((tb, tn), lambda i, j: (i, j))
    zspec = lambda off: pl.BlockSpec((tb, tn), lambda i, j: (i, off // tn + j))
    par = pl.BlockSpec((1, tn), lambda i, j: (0, j))
    lane = np.arange(tn)
    seg = jnp.asarray((lane[:, None] // RWKV_HEAD == lane[None, :] // RWKV_HEAD).astype(np.float32))
    return pl.pallas_call(
        _rwkv_post_body,
        name="rwkv_post",
        grid=(T // tb, W // tn),
        in_specs=[blk, blk, zspec(r_off), zspec(k_off), zspec(v_off), blk, blk, blk, par, par, par,
                  pl.BlockSpec((tn, tn), lambda i, j: (0, 0))],
        out_specs=blk,
        out_shape=jax.ShapeDtypeStruct((T, W), BF16),
        compiler_params=_params("parallel", "parallel"),
    )(y_f, y_b, z, z, z, ic_f, ic_b, gate, k_a.reshape(1, W), r_k.reshape(1, W), ln_x.reshape(1, W), seg)


def _dft_cols_body(u_ref, cs_ref, o_ref):
    G = FOURIER_GROUP
    for g in range(u_ref.shape[1] // G):
        pq = jnp.dot(u_ref[:, g * G:(g + 1) * G].astype(BF16), cs_ref[...], preferred_element_type=F32)
        o_ref[0, :, g * G:(g + 1) * G] = pq[:, :G].astype(o_ref.dtype)
        o_ref[1, :, g * G:(g + 1) * G] = pq[:, G:].astype(o_ref.dtype)


def _fft_stage1_body(z_ref, e_ref, o_ref):
    n2 = z_ref.shape[1]
    w = jnp.dot(e_ref[...], jnp.concatenate([z_ref[0], z_ref[1]], axis=0), preferred_element_type=F32)
    o_ref[0] = w[:n2].astype(o_ref.dtype)
    o_ref[1] = w[n2:].astype(o_ref.dtype)


def _fourier_mix(z, n_ctx, width):
    T = z.shape[0] - n_ctx
    tm = ROW_BLOCK
    ro = n_ctx // tm
    G = FOURIER_GROUP
    tn = _pick(width, (1024, 512, G))
    c = np.arange(G)
    ang_c = 2.0 * np.pi * ((c[:, None] * c[None, :]) % G) / G
    cs = jnp.asarray(np.concatenate([np.cos(ang_c), np.sin(ang_c)], axis=1), BF16)
    pq = pl.pallas_call(
        _dft_cols_body,
        name="dft_cols",
        grid=(T // tm, width // tn),
        in_specs=[pl.BlockSpec((tm, tn), lambda i, g: (i + ro, g)), pl.BlockSpec((G, 2 * G), lambda i, g: (0, 0))],
        out_specs=pl.BlockSpec((2, tm, tn), lambda i, g: (0, i, g)),
        out_shape=jax.ShapeDtypeStruct((2, T, width), BF16),
        compiler_params=_params("parallel", "parallel"),
    )(z, cs)
    n2 = FFT_N2
    n1 = T // n2
    zp = pq.reshape(2, n2, n1, width).transpose(0, 2, 1, 3)
    t = jnp.arange(n1, dtype=jnp.int32)[:, None, None] + n1 * jnp.arange(n2, dtype=jnp.int32)[None, None, :]
    ang = ((jnp.arange(n2, dtype=jnp.int32)[None, :, None] * t) % T).astype(F32) * (2.0 * math.pi / T)
    ec, es = jnp.cos(ang), jnp.sin(ang)
    e1 = jnp.concatenate([jnp.concatenate([ec, -es], axis=2), jnp.concatenate([-es, -ec], axis=2)], axis=1).astype(BF16)
    w = pl.pallas_call(
        _fft_stage1_body,
        name="fft_stage1",
        grid=(n1,),
        in_specs=[pl.BlockSpec((2, None, n2, width), lambda i: (0, i, 0, 0)),
                  pl.BlockSpec((None, 2 * n2, 2 * n2), lambda i: (i, 0, 0))],
        out_specs=pl.BlockSpec((2, None, n2, width), lambda i: (0, i, 0, 0)),
        out_shape=jax.ShapeDtypeStruct((2, n1, n2, width), BF16),
        compiler_params=_params("parallel"),
    )(zp, e1)
    a1 = np.arange(n1)
    ang1 = 2.0 * np.pi * ((a1[:, None] * a1[None, :]) % n1) / n1
    f1 = jnp.asarray(np.concatenate([np.cos(ang1), np.sin(ang1)], axis=1), BF16)
    scale = 1.0 / math.sqrt(T * G)
    out = _matmul(f1, w.reshape(2 * n1, n2 * width), name="fft_stage2", tm=n1, tn=_pick(n2 * width, (8192, 1024)),
                  tk=2 * n1, out_dtype=BF16, epi=lambda acc, row0, rows, fulls: acc * scale)
    return out.reshape(T, width)


def _rope_body(x_ref, tab_ref, o_ref, *, scale, keep_dup):
    xt = x_ref[...] * tab_ref[...]
    y = xt + pltpu.roll(xt, QK_ROPE, axis=1)
    if not keep_dup:
        lane = lax.broadcasted_iota(jnp.int32, y.shape, 1)
        y = jnp.where(lane < QK_ROPE, y, 0.0)
    o_ref[...] = (y * scale).astype(o_ref.dtype)


def _rope(z, col_off, n_heads, tab, scale):
    T = z.shape[0]
    tm = ROW_BLOCK
    cb = col_off // LANES
    return pl.pallas_call(
        functools.partial(_rope_body, scale=scale, keep_dup=False),
        name="rope",
        grid=(T // tm, n_heads),
        in_specs=[pl.BlockSpec((tm, LANES), lambda i, h: (i, cb + h)), pl.BlockSpec((tm, LANES), lambda i, h: (i, 0))],
        out_specs=pl.BlockSpec((tm, LANES), lambda i, h: (i, h)),
        out_shape=jax.ShapeDtypeStruct((T, n_heads * LANES), BF16),
        compiler_params=_params("parallel", "parallel"),
    )(z, tab)


def _q_final_body(qn_ref, qr_ref, tab_ref, o_ref, *, scale):
    xt = qr_ref[...] * tab_ref[...]
    y = xt + pltpu.roll(xt, QK_ROPE, axis=1)
    lane = lax.broadcasted_iota(jnp.int32, y.shape, 1)
    o_ref[:, :QK_NOPE] = (qn_ref[...] * scale).astype(o_ref.dtype)
    o_ref[:, QK_NOPE:] = (jnp.where(lane < QK_ROPE, y, 0.0) * scale).astype(o_ref.dtype)


def _q_final(q, tab, scale):
    T = q.shape[0]
    tm = _pick(T, (1024, 512, 256))
    blk = lambda off: pl.BlockSpec((tm, LANES), lambda i, h: (i, off + h))
    return pl.pallas_call(
        functools.partial(_q_final_body, scale=scale),
        name="q_final",
        grid=(T // tm, MLA_HEADS),
        in_specs=[blk(0), blk(MLA_HEADS), pl.BlockSpec((tm, LANES), lambda i, h: (i, 0))],
        out_specs=pl.BlockSpec((tm, 2 * LANES), lambda i, h: (i, h)),
        out_shape=jax.ShapeDtypeStruct((T, MLA_HEADS * 2 * LANES), BF16),
        compiler_params=_params("parallel", "parallel"),
    )(q, q, tab)


ATTN_SUBTILES = 4


def _attn_body(q_ref, kn_ref, kr_ref, v_ref, o_ref, m_ref, acc_ref, *, nk):
    j = pl.program_id(2)

    @pl.when(j == 0)
    def _():
        m_ref[...] = jnp.full_like(m_ref, -jnp.inf)
        acc_ref[...] = jnp.zeros_like(acc_ref)

    k_cat = jnp.concatenate([kn_ref[...], kr_ref[...]], axis=1)
    v_cat = jnp.concatenate([v_ref[...], jnp.ones(v_ref.shape, BF16)], axis=1)
    ts = q_ref.shape[0] // ATTN_SUBTILES

    def logits(i):
        return lax.dot_general(q_ref[i * ts:(i + 1) * ts, :], k_cat, NT, preferred_element_type=F32)

    def update(i, s):
        rows = slice(i * ts, (i + 1) * ts)
        m_old = m_ref[rows, :]
        m_new = jnp.maximum(m_old, jnp.max(s, axis=-1, keepdims=True))
        p = jnp.exp2(s - m_new).astype(BF16)
        acc_ref[rows, :] = jnp.exp2(m_old - m_new) * acc_ref[rows, :] + jnp.dot(p, v_cat, preferred_element_type=F32)
        m_ref[rows, :] = m_new

    s_prev = logits(0)
    for i in range(1, ATTN_SUBTILES):
        s_next = logits(i)
        update(i - 1, s_prev)
        s_prev = s_next
    update(ATTN_SUBTILES - 1, s_prev)

    @pl.when(j == nk - 1)
    def _():
        acc = acc_ref[...]
        o_ref[...] = (acc[:, :V_HEAD] / acc[:, V_HEAD:]).astype(o_ref.dtype)


def _attention(q, kv, kr, n_ctx):
    T = kv.shape[0]
    Tq = q.shape[0]
    tq = _pick(Tq, (1024, 512, 256))
    tk = _pick(T, (1408, 768, 256))
    nk = T // tk
    return pl.pallas_call(
        functools.partial(_attn_body, nk=nk),
        name="mla_attention",
        grid=(MLA_HEADS, Tq // tq, nk),
        in_specs=[pl.BlockSpec((tq, 2 * LANES), lambda h, i, j: (i, h)),
                  pl.BlockSpec((tk, LANES), lambda h, i, j: (j, 2 * h)),
                  pl.BlockSpec((tk, LANES), lambda h, i, j: (j, 0)),
                  pl.BlockSpec((tk, LANES), lambda h, i, j: (j, 2 * h + 1))],
        out_specs=pl.BlockSpec((tq, LANES), lambda h, i, j: (i, h)),
        out_shape=jax.ShapeDtypeStruct((Tq, MLA_HEADS * V_HEAD), BF16),
        scratch_shapes=[pltpu.VMEM((tq, 1), F32), pltpu.VMEM((tq, 2 * V_HEAD), F32)],
        compiler_params=_params("parallel", "parallel", "arbitrary"),
    )(q, kv, kr, kv)


def _moe_up_body(be_ref, x_ref, wg_ref, wu_ref, o_ref, wgb_ref, wub_ref):
    b = pl.program_id(1)
    changed = jnp.logical_or(b == 0, be_ref[b] != be_ref[jnp.maximum(b - 1, 0)])

    @pl.when(changed)
    def _():
        wgb_ref[...] = wg_ref[...].astype(BF16)
        wub_ref[...] = wu_ref[...].astype(BF16)

    xv = x_ref[...]
    gate = jnp.dot(xv, wgb_ref[...], preferred_element_type=F32)
    up = jnp.dot(xv, wub_ref[...], preferred_element_type=F32)
    o_ref[...] = (gate * jax.nn.sigmoid(gate) * up).astype(o_ref.dtype)


def _moe_up(xb, block_e, wg, wu, layer):
    n_rows, D = xb.shape
    DE = wg.shape[3]
    tb, tn = MOE_BLOCK, 512
    w_spec = pl.BlockSpec((None, None, D, tn), lambda n, b, be: (layer, be[b], 0, n))
    return pl.pallas_call(
        _moe_up_body,
        name="moe_up",
        grid_spec=pltpu.PrefetchScalarGridSpec(
            num_scalar_prefetch=1,
            grid=(DE // tn, n_rows // tb),
            in_specs=[pl.BlockSpec((tb, D), lambda n, b, be: (b, 0)), w_spec, w_spec],
            out_specs=pl.BlockSpec((tb, tn), lambda n, b, be: (b, n)),
            scratch_shapes=[pltpu.VMEM((D, tn), BF16), pltpu.VMEM((D, tn), BF16)]),
        out_shape=jax.ShapeDtypeStruct((n_rows, DE), BF16),
        compiler_params=_params("arbitrary", "arbitrary"),
    )(block_e, xb, wg, wu)


def _moe_down_body(be_ref, h_ref, wd_ref, sw_ref, o_ref, wdb_ref):
    b = pl.program_id(1)
    changed = jnp.logical_or(b == 0, be_ref[b] != be_ref[jnp.maximum(b - 1, 0)])

    @pl.when(changed)
    def _():
        wdb_ref[...] = wd_ref[...].astype(BF16)

    o_ref[...] = jnp.dot(h_ref[...], wdb_ref[...], preferred_element_type=F32) * sw_ref[...]


def _moe_down(h, block_e, wd, slot_w, layer):
    n_rows, DE = h.shape
    D = wd.shape[3]
    tb, tn = MOE_BLOCK, 2048
    return pl.pallas_call(
        _moe_down_body,
        name="moe_down",
        grid_spec=pltpu.PrefetchScalarGridSpec(
            num_scalar_prefetch=1,
            grid=(D // tn, n_rows // tb),
            in_specs=[pl.BlockSpec((tb, DE), lambda n, b, be: (b, 0)),
                      pl.BlockSpec((None, None, DE, tn), lambda n, b, be: (layer, be[b], 0, n)),
                      pl.BlockSpec((tb, 1), lambda n, b, be: (b, 0))],
            out_specs=pl.BlockSpec((tb, tn), lambda n, b, be: (b, n)),
            scratch_shapes=[pltpu.VMEM((DE, tn), BF16)]),
        out_shape=jax.ShapeDtypeStruct((n_rows, D), F32),
        compiler_params=_params("arbitrary", "arbitrary"),
    )(block_e, h, wd, slot_w.reshape(n_rows, 1))


def _route(logits, router_b):
    T = logits.shape[0]
    s = jax.nn.sigmoid(logits)
    sel = (s + router_b.astype(F32)).reshape(T, N_GROUPS, EXPERTS_PER_GROUP)

    def top2(v):
        pos = jnp.arange(v.shape[-1])
        i0 = jnp.argmax(v, axis=-1)
        rest = jnp.where(pos == i0[..., None], -jnp.inf, v)
        return i0, jnp.argmax(rest, axis=-1), jnp.max(v, axis=-1), jnp.max(rest, axis=-1)

    _, _, v0, v1 = top2(sel)
    grp = jnp.argmax(v0 + v1, axis=-1)
    in_grp = jnp.arange(N_GROUPS)[None, :, None] == grp[:, None, None]
    loc0, loc1, _, _ = top2(jnp.sum(jnp.where(in_grp, sel, 0.0), axis=1))
    idx = grp[:, None] * EXPERTS_PER_GROUP + jnp.stack([loc0, loc1], axis=-1)
    picked = jnp.arange(N_EXPERTS)[None, None, :] == idx[:, :, None]
    wts = jnp.sum(jnp.where(picked, s[:, None, :], 0.0), axis=-1)
    return idx, wts / jnp.sum(wts, axis=-1, keepdims=True)


def _moe_ffn(x, h, logits, router_b, wg, wu, wd, sg, su, sd, layer, gates, n_ctx, final_g=None):
    T, D = h.shape
    idx, wts = _route(logits[:, :N_EXPERTS], router_b)
    A = T * TOP_K
    flat_e = idx.reshape(A)
    onehot = (flat_e[:, None] == jnp.arange(N_EXPERTS)[None, :]).astype(jnp.int32)
    csum = jnp.cumsum(onehot, axis=0)
    counts = csum[-1]
    rank = jnp.sum(onehot * csum, axis=1) - 1
    padded = (counts + MOE_BLOCK - 1) // MOE_BLOCK * MOE_BLOCK
    pad_end = jnp.cumsum(padded)
    pad_start = pad_end - padded
    dest = (jnp.sum(onehot * pad_start[None, :], axis=1) + rank).astype(jnp.int32)
    n_blocks = -(-A // MOE_BLOCK) + N_EXPERTS
    n_slots = n_blocks * MOE_BLOCK
    slot_tok = jnp.full((n_slots,), T, jnp.int32).at[dest].set(jnp.arange(A, dtype=jnp.int32) // TOP_K)
    slot_w = jnp.zeros((n_slots,), F32).at[dest].set(wts.reshape(A))
    xb = jnp.concatenate([h, jnp.zeros((1, D), h.dtype)], axis=0)[slot_tok]
    block_e = jnp.minimum(jnp.sum(pad_end[None, :] <= (jnp.arange(n_blocks) * MOE_BLOCK)[:, None], axis=1),
                          N_EXPERTS - 1).astype(jnp.int32)
    y_slots = _moe_down(_moe_up(xb, block_e, wg, wu, layer), block_e, wd, slot_w, layer)
    slot_of = dest.reshape(T, TOP_K)
    routed = y_slots[slot_of[:, 0]] + y_slots[slot_of[:, 1]]

    DE = sg.shape[2]
    half = SHARED_TN // 2
    w_gu = jnp.stack([sg[layer].reshape(D, DE // half, half), su[layer].reshape(D, DE // half, half)],
                     axis=2).reshape(D, 2 * DE).astype(BF16)

    def swiglu_epi(acc, row0, rows, fulls):
        g = acc[:, :half]
        return g * jax.nn.sigmoid(g) * acc[:, half:]

    hs = _matmul(h, w_gu, name="shared_up", tm=_pick(T, (1408, 1024, 768, 256)), tn=SHARED_TN, out_tn=half,
                 tk=2048, out_dtype=BF16, epi=swiglu_epi)

    def combine_epi(acc, row0, rows, fulls):
        rid = row0 + lax.broadcasted_iota(jnp.int32, acc.shape, 0)
        gate = jnp.where(rid < n_ctx, rows[0][0:1], rows[0][1:2])
        y = fulls[0] + gate * (fulls[1] + acc)
        if final_g is not None:
            y = y * lax.rsqrt(jnp.mean(y * y, axis=-1, keepdims=True) + EPS) * rows[1]
        return y

    rows = (gates,) if final_g is None else (gates, final_g.reshape(1, D))
    return _matmul(hs, sd[layer].astype(BF16), name="shared_down_combine", tm=ROW_BLOCK, tn=D, tk=DE,
                   rows=rows, fulls=(x, routed), epi=combine_epi)


def _silu(x):
    return x * jax.nn.sigmoid(x)


def _softplus(x):
    return jnp.maximum(x, 0.0) + jnp.log(1.0 + jnp.exp(-jnp.abs(x)))


def _adaln(cond2, w, b, layer):
    D = cond2.shape[1]
    a = jnp.zeros((SUBLANES, D), F32).at[:2].set(cond2)
    out = _matmul(a, w, name="adaln", tm=SUBLANES, tn=512, tk=D, a_act=_silu, b_lead=layer,
                  rows=(b[layer].reshape(1, -1),), epi=lambda acc, row0, rows, fulls: acc + rows[0])
    return out[:2].reshape(2, -1, D)


def _gated_residual_epi(n_ctx):
    def epi(acc, row0, rows, fulls):
        rid = row0 + lax.broadcasted_iota(jnp.int32, acc.shape, 0)
        gate = jnp.where(rid < n_ctx, rows[0][0:1], rows[0][1:2])
        return fulls[0] + gate * acc
    return epi


def _pad_cols(w, width):
    return jnp.pad(w, ((0, 0), (0, width - w.shape[1])))


def _pad_rows(w, height):
    return jnp.pad(w, ((0, height - w.shape[0]), (0, 0)))


def _even_mixer(x, h, mods, n_ctx, w_in, w_out, w0, w2, a0, a2, g2, k_k, k_a, r_k, ln_x, conv_w):
    T, D = x.shape
    W = D // 2
    o = np.cumsum((0, W, W, W, DECAY_LORA, DECAY_LORA, ICLR_LORA, ICLR_LORA, GATE_LORA, W, W))
    lora = [_pad_cols(w_in[:, o[i]:o[i + 1]], LANES) for i in range(3, 7)]
    w_in_p = jnp.concatenate([w_in[:, :o[3]], w_in[:, o[8]:], *lora, w_in[:, o[7]:o[8]]], axis=1).astype(BF16)
    tm = _pick(T, (1408, 768, 256))
    z = _matmul(h, w_in_p, name="even_w_in", tm=tm, tn=768, tk=2048)
    r_off, k_off, v_off, gb_off, gc_off, u_off = (i * W for i in range(6))
    lo = 6 * W

    def lora_mm(col, kdim, w, bias, act, epi):
        rows = () if bias is None else (bias.reshape(1, W),)
        return _matmul(z, _pad_rows(w, kdim), name="rwkv_lora", tm=tm, tn=1024, tk=kdim, a_col_off=col, a_act=act, rows=rows, epi=epi)

    decay_epi = lambda acc, row0, rows, fulls: -jnp.exp(-_softplus(-(rows[0] + acc)) - 0.5)
    iclr_epi = lambda acc, row0, rows, fulls: jax.nn.sigmoid(rows[0] + acc)
    lw_f = lora_mm(lo, LANES, w2[0], w0[0], jnp.tanh, decay_epi)
    lw_b = lora_mm(lo + LANES, LANES, w2[1], w0[1], jnp.tanh, decay_epi)
    ic_f = lora_mm(lo + 2 * LANES, LANES, a2[0], a0[0], None, iclr_epi)
    ic_b = lora_mm(lo + 3 * LANES, LANES, a2[1], a0[1], None, iclr_epi)
    gate = lora_mm(lo + 4 * LANES, GATE_LORA, g2, None, jax.nn.sigmoid, None)

    y_f, y_b = _rwkv_scan(z, r_off, k_off, v_off, lw_f, ic_f, lw_b, ic_b, k_k, k_a, n_ctx)
    o_rwkv = _rwkv_post(y_f, y_b, z, r_off, k_off, v_off, ic_f, ic_b, gate, k_a, r_k, ln_x)
    o_conv = _short_conv(z, gb_off, gc_off, u_off, conv_w, n_ctx)
    y = jnp.concatenate([o_rwkv, o_conv], axis=1)
    return _matmul(y, w_out.astype(BF16), name="even_w_out", tm=tm, tn=512, tk=2048, rows=(mods[:, 2],), fulls=(x,),
                   epi=_gated_residual_epi(n_ctx))


def _rope_tables(n_ctx, n_lat):
    rows = n_lat // GRID_W
    row = jnp.repeat(jnp.arange(rows), GRID_W)
    col = jnp.tile(jnp.arange(GRID_W), rows)
    pos = jnp.stack([row, col], axis=-1).astype(F32)
    inv_freq = ROPE_BASE ** (-jnp.arange(ROPE_PAIRS, dtype=F32) / ROPE_PAIRS)
    ang = pos[:, :, None, None] * inv_freq
    shape = (n_lat, 2, 2, ROPE_PAIRS)
    cos = jnp.broadcast_to(jnp.cos(ang), shape).reshape(n_lat, QK_ROPE)
    sin = jnp.broadcast_to(jnp.sin(ang), shape).reshape(n_lat, QK_ROPE)
    cos = jnp.concatenate([jnp.ones((n_ctx, QK_ROPE), F32), cos], axis=0)
    sin = jnp.concatenate([jnp.zeros((n_ctx, QK_ROPE), F32), sin], axis=0)
    return jnp.concatenate([cos, sin], axis=1)


def _rot_cols(w):
    lead = w.shape[:-1]
    wr = w.reshape(*lead, 2, 2, ROPE_PAIRS)
    return jnp.stack([-wr[..., 1, :], wr[..., 0, :]], axis=-2).reshape(*lead, QK_ROPE)


def _odd_mixer(x, h, mods, n_ctx, w_in, w_out, q_norm, w_uq, kv_norm, w_ukv):
    T, D = x.shape
    W = D // 2
    n_lat = T - n_ctx
    kr_w = w_in[:, W + Q_LORA + KV_LORA:]
    w_in_p = jnp.concatenate([w_in, _rot_cols(kr_w)], axis=1).astype(BF16)
    w_in_p = _pad_cols(w_in_p, -(-w_in_p.shape[1] // 768) * 768)
    tm = _pick(T, (1408, 768, 256))
    z = _matmul(h, w_in_p, name="odd_w_in", tm=tm, tn=768, tk=2048)
    qa_off, kva_off, kr_off = W, W + Q_LORA, W + Q_LORA + KV_LORA

    qn = _rmsnorm_cols(z, qa_off, Q_LORA, q_norm)
    kvn = _rmsnorm_cols(z, kva_off, KV_LORA, kv_norm)
    uq = w_uq.reshape(Q_LORA, MLA_HEADS, QK_NOPE + QK_ROPE)
    uq_rope = uq[:, :, QK_NOPE:]
    w_uq_p = jnp.concatenate([uq[:, :, :QK_NOPE].reshape(Q_LORA, -1),
                              jnp.concatenate([uq_rope, _rot_cols(uq_rope)], axis=-1).reshape(Q_LORA, -1)], axis=1).astype(BF16)
    tml = _pick(n_lat, (1024, 512, 256))
    q_lat = _matmul(qn[n_ctx:], w_uq_p, name="mla_uq", tm=tml, tn=1024, tk=Q_LORA)
    kv = _matmul(kvn, w_ukv.astype(BF16), name="mla_ukv", tm=tm, tn=1024, tk=KV_LORA, out_dtype=BF16)

    tab = _rope_tables(n_ctx, n_lat)
    q_scale = SM_SCALE * math.log2(math.e)
    tab_lat = tab[n_ctx:]
    q_fin = _q_final(q_lat, tab_lat, q_scale)
    kr = _rope(z, kr_off, 1, tab, 1.0)
    att = _attention(q_fin, kv, kr, n_ctx)
    four = _fourier_mix(z, n_ctx, W)
    y = jnp.concatenate([four, att], axis=1)
    return _matmul(y, w_out.astype(BF16), name="odd_w_out", tm=tml, tn=1024, tk=2048, rows=(mods[:, 2],), fulls=(x[n_ctx:],),
                   epi=_gated_residual_epi(0))


def kernel(x, c, ctx, c_ctx, ada_w, ada_b, norm1_g, norm2_g, ev_w_in, ev_w_out, ev_w0, ev_w2, ev_a0, ev_a2, ev_g2, ev_k_k, ev_k_a, ev_r_k, ev_ln_x, ev_conv_w, od_w_in, od_w_out, od_q_norm, od_w_uq, od_kv_norm, od_w_ukv, router_w, router_b, moe_wg, moe_wu, moe_wd, shared_wg, shared_wu, shared_wd, final_g):
    B, n_lat, D = x.shape
    n_ctx = ctx.shape[1]
    depth = ada_w.shape[0]
    assert B == 1 and depth == 2 and n_ctx == ROW_BLOCK
    xs = jnp.concatenate([ctx[0], x[0]], axis=0)
    cond2 = jnp.concatenate([c_ctx[None], c], axis=0)
    router_w_p = _pad_cols(router_w.astype(F32), LANES)

    moe_w = (router_b, moe_wg, moe_wu, moe_wd, shared_wg, shared_wu, shared_wd)
    mods = _adaln(cond2, ada_w, ada_b, 0)
    mods_odd = _adaln(cond2, ada_w, ada_b, 1)
    h = _modulate(xs, norm1_g[0], mods, 0, 1, n_ctx)
    xs = _even_mixer(xs, h, mods, n_ctx, ev_w_in[0], ev_w_out[0], ev_w0[0], ev_w2[0], ev_a0[0], ev_a2[0], ev_g2[0],
                     ev_k_k[0], ev_k_a[0], ev_r_k[0], ev_ln_x[0], ev_conv_w[0])
    h, logits = _modulate(xs, norm2_g[0], mods, 3, 4, n_ctx, router_w=router_w_p)
    xs = _moe_ffn(xs, h, logits, *moe_w, 0, mods[:, 5], n_ctx)

    mods = mods_odd
    h = _modulate(xs, norm1_g[1], mods, 0, 1, n_ctx)
    xl = _odd_mixer(xs, h, mods, n_ctx, od_w_in[0], od_w_out[0], od_q_norm[0], od_w_uq[0], od_kv_norm[0], od_w_ukv[0])
    h, logits = _modulate(xl, norm2_g[1], mods, 3, 4, 0, router_w=router_w_p)
    out = _moe_ffn(xl, h, logits, *moe_w, 1, mods[:, 5], 0, final_g=final_g)
    return out[None]
```

```python
import functools
import math

import numpy as np
import jax
import jax.numpy as jnp
from jax import lax
from jax.experimental import pallas as pl
from jax.experimental.pallas import tpu as pltpu

F32 = jnp.float32
BF16 = jnp.bfloat16
HIGHEST = lax.Precision.HIGHEST

LANES = 128
SUBLANES = 8
VMEM_LIMIT_BYTES = 56 * 1024 * 1024

EPS = 1e-6
GN_EPS = 64e-5
RWKV_HEAD = 64
DECAY_LORA = 96
ICLR_LORA = 96
GATE_LORA = 256
CONV_K = 3
FOURIER_GROUP = 128
FFT_N2 = 128
MLA_HEADS = 16
QK_NOPE = 128
QK_ROPE = 64
V_HEAD = 128
Q_LORA = 1024
KV_LORA = 512
ROPE_PAIRS = QK_ROPE // 4
ROPE_BASE = 10000.0
GRID_W = 64
SM_SCALE = (QK_NOPE + QK_ROPE) ** -0.5
N_EXPERTS = 16
N_GROUPS = 4
EXPERTS_PER_GROUP = N_EXPERTS // N_GROUPS
TOP_K = 2
MOE_BLOCK = 256
SHARED_TN = 1024
ROW_BLOCK = 256
SCAN_CHUNK = 64


def _params(*sem):
    return pltpu.CompilerParams(dimension_semantics=sem, vmem_limit_bytes=VMEM_LIMIT_BYTES)


def _pick(n, candidates):
    for c in candidates:
        if n % c == 0:
            return c
    raise ValueError(f"no tile for {n} among {candidates}")


def _mm_body(*refs, nk, a_act, epi, n_rows, n_fulls, tm):
    a_ref, b_ref = refs[0], refs[1]
    row_refs = refs[2:2 + n_rows]
    full_refs = refs[2 + n_rows:2 + n_rows + n_fulls]
    o_ref = refs[2 + n_rows + n_fulls]
    acc_ref = refs[3 + n_rows + n_fulls]
    k = pl.program_id(2)
    av = a_ref[...]
    if a_act is not None:
        av = a_act(av.astype(F32))
    part = jnp.dot(av.astype(BF16), b_ref[...].astype(BF16), preferred_element_type=F32)

    def finish(acc):
        if epi is not None:
            row0 = pl.program_id(0) * tm
            acc = epi(acc, row0, [r[...] for r in row_refs], [f[...] for f in full_refs])
        o_ref[...] = acc.astype(o_ref.dtype)

    if nk == 1:
        finish(part)
        return

    @pl.when(k == 0)
    def _():
        acc_ref[...] = part

    @pl.when(jnp.logical_and(k > 0, k < nk - 1))
    def _():
        acc_ref[...] += part

    @pl.when(k == nk - 1)
    def _():
        finish(acc_ref[...] + part)


def _matmul(a, b, *, name, tm, tn, tk, out_dtype=F32, out_tn=None, a_col_off=0, a_act=None, b_lead=None,
            rows=(), fulls=(), epi=None):
    K, N = b.shape[-2:]
    M = a.shape[0]
    out_tn = tn if out_tn is None else out_tn
    assert M % tm == 0 and N % tn == 0 and K % tk == 0 and a_col_off % tk == 0
    nk = K // tk
    ko = a_col_off // tk
    if b.ndim == 3:
        b_spec = pl.BlockSpec((None, tk, tn), lambda i, j, k: (b_lead, k, j))
    else:
        b_spec = pl.BlockSpec((tk, tn), lambda i, j, k: (k, j))
    in_specs = [pl.BlockSpec((tm, tk), lambda i, j, k: (i, k + ko)), b_spec]
    for r in rows:
        in_specs.append(pl.BlockSpec((r.shape[0], tn), lambda i, j, k: (0, j)))
    for _ in fulls:
        in_specs.append(pl.BlockSpec((tm, out_tn), lambda i, j, k: (i, j)))
    body = functools.partial(_mm_body, nk=nk, a_act=a_act, epi=epi, n_rows=len(rows),
                             n_fulls=len(fulls), tm=tm)
    return pl.pallas_call(
        body,
        grid=(M // tm, N // tn, nk),
        in_specs=in_specs,
        out_specs=pl.BlockSpec((tm, out_tn), lambda i, j, k: (i, j)),
        out_shape=jax.ShapeDtypeStruct((M, N // tn * out_tn), out_dtype),
        scratch_shapes=[pltpu.VMEM((tm, tn) if nk > 1 else (SUBLANES, LANES), F32)],
        compiler_params=_params("parallel", "parallel", "arbitrary"),
        name=name,
    )(a, b, *rows, *fulls)


def _modulate_body(x_ref, g_ref, mod_ref, *rest, shift_idx, scale_idx, with_router):
    xv = x_ref[...]
    y = xv * lax.rsqrt(jnp.mean(xv * xv, axis=-1, keepdims=True) + EPS) * g_ref[...]
    h = y * (1.0 + mod_ref[0, scale_idx:scale_idx + 1, :]) + mod_ref[0, shift_idx:shift_idx + 1, :]
    hb = h.astype(BF16)
    if with_router:
        rw_ref, o_ref, lg_ref, pk_ref = rest
        lg_ref[...] = jnp.dot(h, rw_ref[...], precision=HIGHEST, preferred_element_type=F32)
        half = hb.shape[1] // 2
        bits = lambda t: lax.bitcast_convert_type(t.astype(F32), jnp.uint32)
        pk_ref[...] = (bits(hb[:, half:]) & jnp.uint32(0xFFFF0000)) | (bits(hb[:, :half]) >> 16)
    else:
        (o_ref,) = rest
    o_ref[...] = hb


def _modulate(x, g, mods, shift_idx, scale_idx, n_ctx, router_w=None):
    T, D = x.shape
    tm = ROW_BLOCK
    assert T % tm == 0 and n_ctx % tm == 0
    nc = n_ctx // tm
    in_specs = [
        pl.BlockSpec((tm, D), lambda i: (i, 0)),
        pl.BlockSpec((1, D), lambda i: (0, 0)),
        pl.BlockSpec((1, mods.shape[1], D), lambda i: (jnp.where(i < nc, 0, 1), 0, 0)),
    ]
    out_specs = [pl.BlockSpec((tm, D), lambda i: (i, 0))]
    out_shape = [jax.ShapeDtypeStruct((T, D), BF16)]
    args = [x, g.reshape(1, D), mods]
    if router_w is not None:
        in_specs.append(pl.BlockSpec((D, LANES), lambda i: (0, 0)))
        out_specs += [pl.BlockSpec((tm, LANES), lambda i: (i, 0)), pl.BlockSpec((tm, D // 2), lambda i: (i, 0))]
        out_shape += [jax.ShapeDtypeStruct((T, LANES), F32), jax.ShapeDtypeStruct((T, D // 2), jnp.uint32)]
        args.append(router_w)
    body = functools.partial(_modulate_body, shift_idx=shift_idx, scale_idx=scale_idx,
                             with_router=router_w is not None)
    outs = pl.pallas_call(body, grid=(T // tm,), in_specs=in_specs, out_specs=out_specs,
                          out_shape=out_shape, compiler_params=_params("parallel"), name="modulate")(*args)
    return outs if router_w is not None else outs[0]


def _rmsnorm_cols_body(x_ref, g_ref, o_ref):
    xv = x_ref[...]
    y = xv * lax.rsqrt(jnp.mean(xv * xv, axis=-1, keepdims=True) + EPS) * g_ref[...]
    o_ref[...] = y.astype(o_ref.dtype)


def _rmsnorm_cols(z, col_off, width, g, out_dtype=BF16):
    T = z.shape[0]
    tm = ROW_BLOCK
    assert col_off % width == 0 and T % tm == 0
    cb = col_off // width
    return pl.pallas_call(
        _rmsnorm_cols_body,
        name="rmsnorm_cols",
        grid=(T // tm,),
        in_specs=[pl.BlockSpec((tm, width), lambda i: (i, cb)), pl.BlockSpec((1, width), lambda i: (0, 0))],
        out_specs=pl.BlockSpec((tm, width), lambda i: (i, 0)),
        out_shape=jax.ShapeDtypeStruct((T, width), out_dtype),
        compiler_params=_params("parallel"),
    )(z, g.reshape(1, width))


def _conv_body(gb_ref, gc_ref, u_ref, gcp_ref, up_ref, gcn_ref, un_ref, w_ref, o_ref, *, tb, nb, nc):
    i = pl.program_id(0)
    p = gc_ref[...] * u_ref[...]
    prev_row = (gcp_ref[...] * up_ref[...])[SUBLANES - 1:SUBLANES, :]
    next_row = (gcn_ref[...] * un_ref[...])[0:1, :]
    starts = jnp.logical_or(i == 0, i == nc)
    ends = jnp.logical_or(i == nc - 1, i == nb - 1)
    prev_row = jnp.where(starts, 0.0, prev_row)
    next_row = jnp.where(ends, 0.0, next_row)
    rid = lax.broadcasted_iota(jnp.int32, p.shape, 0)
    xm1 = jnp.where(rid == 0, prev_row, pltpu.roll(p, 1, axis=0))
    xp1 = jnp.where(rid == tb - 1, next_row, pltpu.roll(p, tb - 1, axis=0))
    w = w_ref[...]
    o_ref[...] = (gb_ref[...] * (w[0:1] * xm1 + w[1:2] * p + w[2:3] * xp1)).astype(o_ref.dtype)


def _short_conv(z, gb_off, gc_off, u_off, conv_w, n_ctx):
    T = z.shape[0]
    C = conv_w.shape[1]
    tb, tn = ROW_BLOCK, 512
    nb, nc = T // tb, n_ctx // tb
    hb = tb // SUBLANES
    last_h = T // SUBLANES - 1
    cur = lambda off: pl.BlockSpec((tb, tn), lambda i, j: (i, off // tn + j))
    prv = lambda off: pl.BlockSpec((SUBLANES, tn), lambda i, j: (jnp.maximum(i * hb - 1, 0), off // tn + j))
    nxt = lambda off: pl.BlockSpec((SUBLANES, tn), lambda i, j: (jnp.minimum((i + 1) * hb, last_h), off // tn + j))
    return pl.pallas_call(
        functools.partial(_conv_body, tb=tb, nb=nb, nc=nc),
        name="short_conv",
        grid=(nb, C // tn),
        in_specs=[cur(gb_off), cur(gc_off), cur(u_off), prv(gc_off), prv(u_off), nxt(gc_off), nxt(u_off),
                  pl.BlockSpec((CONV_K, tn), lambda i, j: (0, j))],
        out_specs=pl.BlockSpec((tb, tn), lambda i, j: (i, j)),
        out_shape=jax.ShapeDtypeStruct((T, C), BF16),
        compiler_params=_params("parallel", "parallel"),
    )(z, z, z, z, z, z, z, conv_w)


NN = (((1,), (0,)), ((), ()))
NT = (((1,), (1,)), ((), ()))
TN = (((0,), (0,)), ((), ()))

SCAN_PIECES_GRAM = 1
SCAN_PIECES_INV = 1
SCAN_PIECES_OUT = 1


def _pieces(x, n):
    out = []
    for i in range(n):
        p = x.astype(BF16)
        out.append(p)
        if i + 1 < n:
            x = x - p.astype(F32)
    return out


def _pdot(ap, bp, dims=NN):
    order = max(len(ap), len(bp))
    acc = None
    for i, x in enumerate(ap):
        for j, y in enumerate(bp):
            if i + j < order:
                t = lax.dot_general(x, y, dims, preferred_element_type=F32)
                acc = t if acc is None else acc + t
    return acc


def _pcat(parts, axis):
    return [jnp.concatenate(ps, axis=axis) for ps in zip(*parts)]


def _scan_stages(C, masks):
    n = 2 * C
    reverse, in_h0, row_c, strict, incl, eye, diag_blocks, off_blocks = masks
    pg, pi, po = SCAN_PIECES_GRAM, SCAN_PIECES_INV, SCAN_PIECES_OUT

    def stack(x):
        return jnp.concatenate([jnp.where(in_h0, x, 0.0), jnp.where(in_h0, 0.0, x)], axis=0)

    def s_cum(d):
        for nm in ("lw", "r", "k", "v", "a", "b"):
            d[nm + "_s"] = stack(d[nm])
        x = d["lw"]
        sh = 1
        while sh < C:
            if reverse:
                x = x + jnp.where(row_c < C - sh, pltpu.roll(x, C - sh, axis=0), 0.0)
            else:
                x = x + jnp.where(row_c >= sh, pltpu.roll(x, sh, axis=0), 0.0)
            sh *= 2
        d["cum"] = stack(x)
        d["tot"] = x[0:1] if reverse else x[C - 1:C]

    def s_exp(d):
        cum, tot = d["cum"], d["tot"]
        inv, fin = jnp.exp(-cum), jnp.exp(tot - cum)
        d["r_hat"] = d["r_s"] * jnp.exp(cum)
        d["a_hat_p"] = _pieces(d["a_s"] * jnp.exp(cum - d["lw_s"]), max(pg, po))
        d["r_hat_p"] = _pieces(d["r_hat"], pg)
        d["bk_chk_p"] = _pcat([_pieces(d["b_s"] * inv, pg), _pieces(d["k_s"] * inv, pg)], 0)
        d["b_til_p"] = _pieces(d["b_s"] * fin, po)
        d["bk_til_p"] = _pcat([d["b_til_p"], _pieces(d["k_s"] * fin, po)], 0)
        d["v_p"] = _pieces(d["v_s"], po)
        d["decay"] = jnp.broadcast_to(jnp.exp(tot), (LANES, LANES)).T

    def s_gram(d):
        g = _pdot(_pcat([d["a_hat_p"][:pg], d["r_hat_p"]], 0), d["bk_chk_p"], NT)
        d["n_ab"] = jnp.where(strict, g[:n, :n], 0.0)
        d["m_ak_p"] = _pieces(jnp.where(strict, g[:n, n:], 0.0), po)
        d["m_rb_p"] = _pieces(jnp.where(incl, g[n:, :n], 0.0), po)
        d["m_rk_p"] = _pieces(jnp.where(incl, g[n:, n:], 0.0), po)
        nd = jnp.where(diag_blocks, d["n_ab"], 0.0)
        d["nd_p"] = _pieces(nd, pi)
        d["t"] = eye + nd

    def s_sq1(d):
        d["pw"] = _pdot(d["nd_p"], d["nd_p"])

    def s_ap1(d):
        d["pw_p"] = _pieces(d["pw"], pi)
        d["t"] = d["t"] + _pdot(d["pw_p"], _pieces(d["t"], pi))

    def s_sq2(d):
        d["pw_p"] = _pieces(_pdot(d["pw_p"], d["pw_p"]), pi)

    def s_ap2(d):
        d["t"] = d["t"] + _pdot(d["pw_p"], _pieces(d["t"], pi))

    def s_merge_a(off_mask):
        def f(d):
            d["t_p"] = _pieces(d["t"], pi)
            d["ot_p"] = _pieces(_pdot(_pieces(jnp.where(off_mask, d["n_ab"], 0.0), pi), d["t_p"]), pi)
        return f

    def s_merge_b(d):
        d["t"] = d["t"] + _pdot(d["t_p"], d["ot_p"])

    def s_abar(d):
        d["t_p"] = _pieces(d["t"], po)
        d["abar_p"] = _pieces(_pdot(d["t_p"], d["a_hat_p"][:po]), po)
        d["mv_p"] = _pieces(_pdot(d["m_ak_p"], d["v_p"]), po)

    def s_u0(d):
        d["uv_p"] = _pcat([_pieces(_pdot(d["t_p"], d["mv_p"]), po), d["v_p"]], 0)
        d["rbar_p"] = _pieces(d["r_hat"] + _pdot(d["m_rb_p"], d["abar_p"]), po)
        d["phi_p"] = _pieces(_pdot(d["b_til_p"], d["abar_p"], TN), po)

    def s_out(d):
        d["y0"] = _pdot(_pcat([d["m_rb_p"], d["m_rk_p"]], 1), d["uv_p"])
        d["s0"] = _pdot(d["bk_til_p"], d["uv_p"], TN)

    stages = [s_cum, s_exp, s_gram, s_sq1, s_ap1, s_sq2, s_ap2]
    for off_mask in off_blocks:
        stages += [s_merge_a(off_mask), s_merge_b]
    return stages + [s_abar, s_u0, s_out]


def _scan_apply(d, state):
    C = d["r"].shape[0]
    st_p = _pieces(state, SCAN_PIECES_OUT)
    y = _pdot(d["rbar_p"], st_p) + d["y0"]
    return y[:C] + y[C:], d["decay"] * state + _pdot(d["phi_p"], st_p) + d["s0"]


def _scan_masks(C, reverse):
    n = 2 * C
    lane = lax.broadcasted_iota(jnp.int32, (C, LANES), 1)
    row = lax.broadcasted_iota(jnp.int32, (n, n), 0)
    col = lax.broadcasted_iota(jnp.int32, (n, n), 1)
    same = (row // C) == (col // C)
    before = (col > row) if reverse else (col < row)
    strict = jnp.logical_and(same, before)
    incl = jnp.logical_and(same, jnp.logical_or(before, col == row))
    eye = jnp.where(row == col, 1.0, 0.0)
    row_c = lax.broadcasted_iota(jnp.int32, (C, LANES), 0)
    blk = 8
    diag_blocks = (row // blk) == (col // blk)
    off_blocks = []
    while blk < C:
        off_blocks.append(jnp.logical_and((row // (2 * blk)) == (col // (2 * blk)), (row // blk) != (col // blk)))
        blk *= 2
    return reverse, lane < RWKV_HEAD, row_c, strict, incl, eye, diag_blocks, off_blocks


def _scan_body(rf, kf, vf, lwf, icf, rb, kb, vb, lwb, icb, kk_ref, ka_ref, seg_ref, yf_ref, yb_ref, sf_ref, sb_ref, *, tb):
    @pl.when(pl.program_id(1) == 0)
    def _():
        sf_ref[...] = jnp.zeros_like(sf_ref)
        sb_ref[...] = jnp.zeros_like(sb_ref)

    k_k, k_a, seg = kk_ref[...], ka_ref[...], seg_ref[...].astype(BF16)
    nch = tb // SCAN_CHUNK
    dirs = []
    for refs, y_ref, s_ref, reverse in ((rf, kf, vf, lwf, icf), yf_ref, sf_ref, False), ((rb, kb, vb, lwb, icb), yb_ref, sb_ref, True):
        r, k, v, lw, ic = (t[...] for t in refs)
        kk = k * k_k
        kk = kk * lax.rsqrt(_pdot(_pieces(kk * kk, 3), [seg]) + 1e-12)
        a, b = -kk, kk * ic
        kd = k * (1.0 + (ic - 1.0) * k_a)
        chunks = []
        for c in (range(nch - 1, -1, -1) if reverse else range(nch)):
            sl = slice(c * SCAN_CHUNK, (c + 1) * SCAN_CHUNK)
            chunks.append(dict(r=r[sl], k=kd[sl], v=v[sl], a=a[sl], b=b[sl], lw=lw[sl], rows=sl))
        dirs.append((chunks, _scan_stages(SCAN_CHUNK, _scan_masks(SCAN_CHUNK, reverse)), y_ref, s_ref))

    for step in range(len(dirs[0][1])):
        for chunks, stages, _, _ in dirs:
            for d in chunks:
                stages[step](d)
    states = [s_ref[...] for _, _, _, s_ref in dirs]
    for c in range(nch):
        for i, (chunks, _, y_ref, _) in enumerate(dirs):
            y, states[i] = _scan_apply(chunks[c], states[i])
            y_ref[chunks[c]["rows"], :] = y
    for (_, _, _, s_ref), state in zip(dirs, states):
        s_ref[...] = state


def _rwkv_scan(z, r_off, k_off, v_off, lw_f, ic_f, lw_b, ic_b, k_k, k_a, n_ctx):
    T = z.shape[0]
    W = lw_f.shape[1]
    tb = ROW_BLOCK
    assert n_ctx == tb and T % tb == 0
    nb = T // tb
    npair = W // LANES
    fwd = lambda i: i
    bwd = lambda i: jnp.where(i == 0, 0, nb - i)
    zspec = lambda off, o: pl.BlockSpec((tb, LANES), lambda p, i: (o(i), off // LANES + p))
    wspec = lambda o: pl.BlockSpec((tb, LANES), lambda p, i: (o(i), p))
    par = pl.BlockSpec((1, LANES), lambda p, i: (0, p))
    lane = np.arange(LANES)
    seg = jnp.asarray((lane[:, None] // RWKV_HEAD == lane[None, :] // RWKV_HEAD).astype(np.float32))
    return pl.pallas_call(
        functools.partial(_scan_body, tb=tb),
        name="rwkv_scan",
        grid=(npair, nb),
        in_specs=[zspec(r_off, fwd), zspec(k_off, fwd), zspec(v_off, fwd), wspec(fwd), wspec(fwd),
                  zspec(r_off, bwd), zspec(k_off, bwd), zspec(v_off, bwd), wspec(bwd), wspec(bwd),
                  par, par, pl.BlockSpec((LANES, LANES), lambda p, i: (0, 0))],
        out_specs=[wspec(fwd), wspec(bwd)],
        out_shape=[jax.ShapeDtypeStruct((T, W), F32)] * 2,
        scratch_shapes=[pltpu.VMEM((LANES, LANES), F32)] * 2,
        compiler_params=_params("parallel", "arbitrary"),
    )(z, z, z, lw_f, ic_f, z, z, z, lw_b, ic_b, k_k.reshape(1, W), k_a.reshape(1, W), seg)


def _rwkv_post_body(yf, yb, r, k, v, icf, icb, gate, ka, rk, lnx, seg_ref, o_ref):
    seg = [seg_ref[...].astype(BF16)]
    head_sum = lambda t: _pdot(_pieces(t, 3), seg)
    inv_n = 1.0 / RWKV_HEAD
    wkv = yf[...] + yb[...]
    yc = wkv - head_sum(wkv) * inv_n
    yn = yc * lax.rsqrt(head_sum(yc * yc) * inv_n + GN_EPS) * lnx[...]
    kv, kav = k[...], ka[...]
    kd_sum = kv * (1.0 + (icf[...] - 1.0) * kav) + kv * (1.0 + (icb[...] - 1.0) * kav)
    bonus = head_sum(r[...] * kd_sum * rk[...]) * v[...]
    o_ref[...] = ((yn + bonus) * gate[...]).astype(o_ref.dtype)


def _rwkv_post(y_f, y_b, z, r_off, k_off, v_off, ic_f, ic_b, gate, k_a, r_k, ln_x):
    T, W = y_f.shape
    tb, tn = ROW_BLOCK, 256
    blk = pl.BlockSpec((tb, tn), lambda i, j: (i, j))
    zspec = lambda off: pl.BlockSpec((tb, tn), lambda i, j: (i, off // tn + j))
    par = pl.BlockSpec((1, tn), lambda i, j: (0, j))
    lane = np.arange(tn)
    seg = jnp.asarray((lane[:, None] // RWKV_HEAD == lane[None, :] // RWKV_HEAD).astype(np.float32))
    return pl.pallas_call(
        _rwkv_post_body,
        name="rwkv_post",
        grid=(T // tb, W // tn),
        in_specs=[blk, blk, zspec(r_off), zspec(k_off), zspec(v_off), blk, blk, blk, par, par, par,
                  pl.BlockSpec((tn, tn), lambda i, j: (0, 0))],
        out_specs=blk,
        out_shape=jax.ShapeDtypeStruct((T, W), BF16),
        compiler_params=_params("parallel", "parallel"),
    )(y_f, y_b, z, z, z, ic_f, ic_b, gate, k_a.reshape(1, W), r_k.reshape(1, W), ln_x.reshape(1, W), seg)


def _dft_cols_body(u_ref, cs_ref, o_ref):
    G = FOURIER_GROUP
    for g in range(u_ref.shape[1] // G):
        pq = jnp.dot(u_ref[:, g * G:(g + 1) * G].astype(BF16), cs_ref[...], preferred_element_type=F32)
        o_ref[0, :, g * G:(g + 1) * G] = pq[:, :G].astype(o_ref.dtype)
        o_ref[1, :, g * G:(g + 1) * G] = pq[:, G:].astype(o_ref.dtype)


def _fft_stage1_body(z_ref, e_ref, o_ref):
    n2 = z_ref.shape[1]
    w = jnp.dot(e_ref[...], jnp.concatenate([z_ref[0], z_ref[1]], axis=0), preferred_element_type=F32)
    o_ref[0] = w[:n2].astype(o_ref.dtype)
    o_ref[1] = w[n2:].astype(o_ref.dtype)


def _fourier_mix(z, n_ctx, width):
    T = z.shape[0] - n_ctx
    tm = ROW_BLOCK
    ro = n_ctx // tm
    G = FOURIER_GROUP
    tn = _pick(width, (1024, 512, G))
    c = np.arange(G)
    ang_c = 2.0 * np.pi * ((c[:, None] * c[None, :]) % G) / G
    cs = jnp.asarray(np.concatenate([np.cos(ang_c), np.sin(ang_c)], axis=1), BF16)
    pq = pl.pallas_call(
        _dft_cols_body,
        name="dft_cols",
        grid=(T // tm, width // tn),
        in_specs=[pl.BlockSpec((tm, tn), lambda i, g: (i + ro, g)), pl.BlockSpec((G, 2 * G), lambda i, g: (0, 0))],
        out_specs=pl.BlockSpec((2, tm, tn), lambda i, g: (0, i, g)),
        out_shape=jax.ShapeDtypeStruct((2, T, width), BF16),
        compiler_params=_params("parallel", "parallel"),
    )(z, cs)
    n2 = FFT_N2
    n1 = T // n2
    zp = pq.reshape(2, n2, n1, width).transpose(0, 2, 1, 3)
    t = jnp.arange(n1, dtype=jnp.int32)[:, None, None] + n1 * jnp.arange(n2, dtype=jnp.int32)[None, None, :]
    ang = ((jnp.arange(n2, dtype=jnp.int32)[None, :, None] * t) % T).astype(F32) * (2.0 * math.pi / T)
    ec, es = jnp.cos(ang), jnp.sin(ang)
    e1 = jnp.concatenate([jnp.concatenate([ec, -es], axis=2), jnp.concatenate([-es, -ec], axis=2)], axis=1).astype(BF16)
    w = pl.pallas_call(
        _fft_stage1_body,
        name="fft_stage1",
        grid=(n1,),
        in_specs=[pl.BlockSpec((2, None, n2, width), lambda i: (0, i, 0, 0)),
                  pl.BlockSpec((None, 2 * n2, 2 * n2), lambda i: (i, 0, 0))],
        out_specs=pl.BlockSpec((2, None, n2, width), lambda i: (0, i, 0, 0)),
        out_shape=jax.ShapeDtypeStruct((2, n1, n2, width), BF16),
        compiler_params=_params("parallel"),
    )(zp, e1)
    a1 = np.arange(n1)
    ang1 = 2.0 * np.pi * ((a1[:, None] * a1[None, :]) % n1) / n1
    f1 = jnp.asarray(np.concatenate([np.cos(ang1), np.sin(ang1)], axis=1), BF16)
    scale = 1.0 / math.sqrt(T * G)
    out = _matmul(f1, w.reshape(2 * n1, n2 * width), name="fft_stage2", tm=n1, tn=_pick(n2 * width, (8192, 1024)),
                  tk=2 * n1, out_dtype=BF16, epi=lambda acc, row0, rows, fulls: acc * scale)
    return out.reshape(T, width)


def _rope_body(x_ref, tab_ref, o_ref, *, scale, keep_dup):
    xt = x_ref[...] * tab_ref[...]
    y = xt + pltpu.roll(xt, QK_ROPE, axis=1)
    if not keep_dup:
        lane = lax.broadcasted_iota(jnp.int32, y.shape, 1)
        y = jnp.where(lane < QK_ROPE, y, 0.0)
    o_ref[...] = (y * scale).astype(o_ref.dtype)


def _rope(z, col_off, n_heads, tab, scale):
    T = z.shape[0]
    tm = ROW_BLOCK
    cb = col_off // LANES
    return pl.pallas_call(
        functools.partial(_rope_body, scale=scale, keep_dup=False),
        name="rope",
        grid=(T // tm, n_heads),
        in_specs=[pl.BlockSpec((tm, LANES), lambda i, h: (i, cb + h)), pl.BlockSpec((tm, LANES), lambda i, h: (i, 0))],
        out_specs=pl.BlockSpec((tm, LANES), lambda i, h: (i, h)),
        out_shape=jax.ShapeDtypeStruct((T, n_heads * LANES), BF16),
        compiler_params=_params("parallel", "parallel"),
    )(z, tab)


def _q_final_body(qn_ref, qr_ref, tab_ref, o_ref, *, scale):
    xt = qr_ref[...] * tab_ref[...]
    y = xt + pltpu.roll(xt, QK_ROPE, axis=1)
    lane = lax.broadcasted_iota(jnp.int32, y.shape, 1)
    o_ref[:, :QK_NOPE] = (qn_ref[...] * scale).astype(o_ref.dtype)
    o_ref[:, QK_NOPE:] = (jnp.where(lane < QK_ROPE, y, 0.0) * scale).astype(o_ref.dtype)


def _q_final(q, tab, scale):
    T = q.shape[0]
    tm = _pick(T, (1024, 512, 256))
    blk = lambda off: pl.BlockSpec((tm, LANES), lambda i, h: (i, off + h))
    return pl.pallas_call(
        functools.partial(_q_final_body, scale=scale),
        name="q_final",
        grid=(T // tm, MLA_HEADS),
        in_specs=[blk(0), blk(MLA_HEADS), pl.BlockSpec((tm, LANES), lambda i, h: (i, 0))],
        out_specs=pl.BlockSpec((tm, 2 * LANES), lambda i, h: (i, h)),
        out_shape=jax.ShapeDtypeStruct((T, MLA_HEADS * 2 * LANES), BF16),
        compiler_params=_params("parallel", "parallel"),
    )(q, q, tab)


def _attn_body(q_ref, kn_ref, kr_ref, v_ref, o_ref, kc_ref, vc_ref, *, ts):
    @pl.when(pl.program_id(1) == 0)
    def _():
        kc_ref[:, :QK_NOPE] = kn_ref[...]
        kc_ref[:, QK_NOPE:] = kr_ref[...]
        vc_ref[:, :V_HEAD] = v_ref[...]
        vc_ref[:, V_HEAD:] = jnp.ones(v_ref.shape, BF16)

    q = q_ref[...]
    n_sub = kc_ref.shape[0] // ts

    def logits(c):
        return lax.dot_general(q, kc_ref[c * ts:(c + 1) * ts, :], NT, preferred_element_type=F32)

    def update(c, s, m_old, acc):
        m_new = jnp.maximum(m_old, jnp.max(s, axis=-1, keepdims=True))
        p = jnp.exp2(s - m_new).astype(BF16)
        pv = jnp.dot(p, vc_ref[c * ts:(c + 1) * ts, :], preferred_element_type=F32)
        return m_new, (pv if acc is None else jnp.exp2(m_old - m_new) * acc + pv)

    m = jnp.full((q.shape[0], 1), -jnp.inf, F32)
    acc = None
    s_prev = logits(0)
    for c in range(1, n_sub):
        s_next = logits(c)
        m, acc = update(c - 1, s_prev, m, acc)
        s_prev = s_next
    m, acc = update(n_sub - 1, s_prev, m, acc)
    o_ref[...] = (acc[:, :V_HEAD] / acc[:, V_HEAD:]).astype(o_ref.dtype)


def _attention(q, kv, kr, n_ctx):
    T = kv.shape[0]
    Tq = q.shape[0]
    tq = _pick(Tq, (1024, 512, 256))
    ts = _pick(T, (768, 256))
    kv_spec = lambda off: pl.BlockSpec((T, LANES), lambda h, i: (0, 2 * h + off))
    return pl.pallas_call(
        functools.partial(_attn_body, ts=ts),
        name="mla_attention",
        grid=(MLA_HEADS, Tq // tq),
        in_specs=[pl.BlockSpec((tq, 2 * LANES), lambda h, i: (i, h)), kv_spec(0),
                  pl.BlockSpec((T, LANES), lambda h, i: (0, 0)), kv_spec(1)],
        out_specs=pl.BlockSpec((tq, LANES), lambda h, i: (i, h)),
        out_shape=jax.ShapeDtypeStruct((Tq, MLA_HEADS * V_HEAD), BF16),
        scratch_shapes=[pltpu.VMEM((T, 2 * LANES), BF16), pltpu.VMEM((T, 2 * LANES), BF16)],
        compiler_params=_params("arbitrary", "arbitrary"),
    )(q, kv, kr, kv)


def _moe_up_body(be_ref, x_ref, wg_ref, wu_ref, o_ref, wgb_ref, wub_ref):
    b = pl.program_id(1)
    changed = jnp.logical_or(b == 0, be_ref[b] != be_ref[jnp.maximum(b - 1, 0)])

    @pl.when(changed)
    def _():
        wgb_ref[...] = wg_ref[...].astype(BF16)
        wub_ref[...] = wu_ref[...].astype(BF16)

    used = b < be_ref[pl.num_programs(1)]

    @pl.when(used)
    def _():
        xw = x_ref[...]
        half = xw.shape[1]
        as_bf16 = lambda bits: lax.bitcast_convert_type(bits, F32).astype(BF16)
        x_lo, x_hi = as_bf16(xw << 16), as_bf16(xw & jnp.uint32(0xFFFF0000))

        def proj(w_ref):
            return (jnp.dot(x_lo, w_ref[:half, :], preferred_element_type=F32)
                    + jnp.dot(x_hi, w_ref[half:, :], preferred_element_type=F32))

        gate, up = proj(wgb_ref), proj(wub_ref)
        o_ref[...] = (gate * jax.nn.sigmoid(gate) * up).astype(o_ref.dtype)

    @pl.when(jnp.logical_not(used))
    def _():
        o_ref[...] = jnp.zeros_like(o_ref)


def _moe_up(xb, block_e, wg, wu, layer):
    n_rows = xb.shape[0]
    D, DE = wg.shape[2:]
    tb, tn = MOE_BLOCK, 512
    w_spec = pl.BlockSpec((None, None, D, tn), lambda n, b, be: (layer, be[b], 0, n))
    return pl.pallas_call(
        _moe_up_body,
        name="moe_up",
        grid_spec=pltpu.PrefetchScalarGridSpec(
            num_scalar_prefetch=1,
            grid=(DE // tn, n_rows // tb),
            in_specs=[pl.BlockSpec((tb, D // 2), lambda n, b, be: (b, 0)), w_spec, w_spec],
            out_specs=pl.BlockSpec((tb, tn), lambda n, b, be: (b, n)),
            scratch_shapes=[pltpu.VMEM((D, tn), BF16), pltpu.VMEM((D, tn), BF16)]),
        out_shape=jax.ShapeDtypeStruct((n_rows, DE), BF16),
        compiler_params=_params("arbitrary", "arbitrary"),
    )(block_e, xb, wg, wu)


def _moe_down_body(be_ref, h_ref, wd_ref, sw_ref, o_ref, wdb_ref):
    b = pl.program_id(1)
    changed = jnp.logical_or(b == 0, be_ref[b] != be_ref[jnp.maximum(b - 1, 0)])

    @pl.when(changed)
    def _():
        wdb_ref[...] = wd_ref[...].astype(BF16)

    used = b < be_ref[pl.num_programs(1)]

    @pl.when(used)
    def _():
        o_ref[...] = jnp.dot(h_ref[...], wdb_ref[...], preferred_element_type=F32) * sw_ref[...]

    @pl.when(jnp.logical_not(used))
    def _():
        o_ref[...] = jnp.zeros_like(o_ref)


def _moe_down(h, block_e, wd, slot_w, layer):
    n_rows, DE = h.shape
    D = wd.shape[3]
    tb, tn = MOE_BLOCK, 2048
    return pl.pallas_call(
        _moe_down_body,
        name="moe_down",
        grid_spec=pltpu.PrefetchScalarGridSpec(
            num_scalar_prefetch=1,
            grid=(D // tn, n_rows // tb),
            in_specs=[pl.BlockSpec((tb, DE), lambda n, b, be: (b, 0)),
                      pl.BlockSpec((None, None, DE, tn), lambda n, b, be: (layer, be[b], 0, n)),
                      pl.BlockSpec((tb, 1), lambda n, b, be: (b, 0))],
            out_specs=pl.BlockSpec((tb, tn), lambda n, b, be: (b, n)),
            scratch_shapes=[pltpu.VMEM((DE, tn), BF16)]),
        out_shape=jax.ShapeDtypeStruct((n_rows, D), F32),
        compiler_params=_params("arbitrary", "arbitrary"),
    )(block_e, h, wd, slot_w.reshape(n_rows, 1))


def _route(logits, router_b):
    T = logits.shape[0]
    s = jax.nn.sigmoid(logits)
    sel = (s + router_b.astype(F32)).reshape(T, N_GROUPS, EXPERTS_PER_GROUP)

    def top2(v):
        pos = jnp.arange(v.shape[-1])
        i0 = jnp.argmax(v, axis=-1)
        rest = jnp.where(pos == i0[..., None], -jnp.inf, v)
        return i0, jnp.argmax(rest, axis=-1), jnp.max(v, axis=-1), jnp.max(rest, axis=-1)

    _, _, v0, v1 = top2(sel)
    grp = jnp.argmax(v0 + v1, axis=-1)
    in_grp = jnp.arange(N_GROUPS)[None, :, None] == grp[:, None, None]
    loc0, loc1, _, _ = top2(jnp.sum(jnp.where(in_grp, sel, 0.0), axis=1))
    idx = grp[:, None] * EXPERTS_PER_GROUP + jnp.stack([loc0, loc1], axis=-1)
    picked = jnp.arange(N_EXPERTS)[None, None, :] == idx[:, :, None]
    wts = jnp.sum(jnp.where(picked, s[:, None, :], 0.0), axis=-1)
    return idx, wts / jnp.sum(wts, axis=-1, keepdims=True)


def _moe_ffn(x, h, logits, h_packed, router_b, wg, wu, wd, sg, su, sd, layer, gates, n_ctx, final_g=None):
    T, D = h.shape
    DE = sg.shape[2]
    half = SHARED_TN // 2
    w_gu = jnp.stack([sg[layer].reshape(D, DE // half, half), su[layer].reshape(D, DE // half, half)],
                     axis=2).reshape(D, 2 * DE).astype(BF16)

    def swiglu_epi(acc, row0, rows, fulls):
        g = acc[:, :half]
        return g * jax.nn.sigmoid(g) * acc[:, half:]

    hs = _matmul(h, w_gu, name="shared_up", tm=_pick(T, (1408, 1024, 768, 256)), tn=SHARED_TN, out_tn=half,
                 tk=2048, out_dtype=BF16, epi=swiglu_epi)

    idx, wts = _route(logits[:, :N_EXPERTS], router_b)
    A = T * TOP_K
    flat_e = idx.reshape(A)
    onehot = (flat_e[:, None] == jnp.arange(N_EXPERTS)[None, :]).astype(jnp.int32)
    csum = jnp.cumsum(onehot, axis=0)
    counts = csum[-1]
    rank = jnp.sum(onehot * csum, axis=1) - 1
    padded = (counts + MOE_BLOCK - 1) // MOE_BLOCK * MOE_BLOCK
    pad_end = jnp.cumsum(padded)
    pad_start = pad_end - padded
    dest = (jnp.sum(onehot * pad_start[None, :], axis=1) + rank).astype(jnp.int32)
    n_blocks = -(-A // MOE_BLOCK) + N_EXPERTS
    n_slots = n_blocks * MOE_BLOCK
    slot_tok = (jnp.arange(n_slots, dtype=jnp.int32) % T).at[dest].set(jnp.arange(A, dtype=jnp.int32) // TOP_K)
    slot_w = jnp.zeros((n_slots,), F32).at[dest].set(wts.reshape(A))
    xb = h_packed[slot_tok]
    block_e = jnp.minimum(jnp.sum(pad_end[None, :] <= (jnp.arange(n_blocks) * MOE_BLOCK)[:, None], axis=1),
                          N_EXPERTS - 1).astype(jnp.int32)
    block_e = jnp.concatenate([block_e, (pad_end[-1:] // MOE_BLOCK).astype(jnp.int32)])
    y_slots = _moe_down(_moe_up(xb, block_e, wg, wu, layer), block_e, wd, slot_w, layer)
    slot_of = dest.reshape(T, TOP_K)
    routed = y_slots[slot_of[:, 0]] + y_slots[slot_of[:, 1]]

    def combine_epi(acc, row0, rows, fulls):
        rid = row0 + lax.broadcasted_iota(jnp.int32, acc.shape, 0)
        gate = jnp.where(rid < n_ctx, rows[0][0:1], rows[0][1:2])
        y = fulls[0] + gate * (fulls[1] + acc)
        if final_g is not None:
            y = y * lax.rsqrt(jnp.mean(y * y, axis=-1, keepdims=True) + EPS) * rows[1]
        return y

    rows = (gates,) if final_g is None else (gates, final_g.reshape(1, D))
    return _matmul(hs, sd[layer].astype(BF16), name="shared_down_combine", tm=ROW_BLOCK, tn=D, tk=DE,
                   rows=rows, fulls=(x, routed), epi=combine_epi)


def _silu(x):
    return x * jax.nn.sigmoid(x)


def _softplus(x):
    return jnp.maximum(x, 0.0) + jnp.log(1.0 + jnp.exp(-jnp.abs(x)))


def _adaln(cond2, w, b, layer):
    D = cond2.shape[1]
    a = jnp.zeros((SUBLANES, D), F32).at[:2].set(cond2)
    out = _matmul(a, w, name="adaln", tm=SUBLANES, tn=512, tk=D, a_act=_silu, b_lead=layer,
                  rows=(b[layer].reshape(1, -1),), epi=lambda acc, row0, rows, fulls: acc + rows[0])
    return out[:2].reshape(2, -1, D)


def _gated_residual_epi(n_ctx):
    def epi(acc, row0, rows, fulls):
        rid = row0 + lax.broadcasted_iota(jnp.int32, acc.shape, 0)
        gate = jnp.where(rid < n_ctx, rows[0][0:1], rows[0][1:2])
        return fulls[0] + gate * acc
    return epi


def _pad_cols(w, width):
    return jnp.pad(w, ((0, 0), (0, width - w.shape[1])))


def _pad_rows(w, height):
    return jnp.pad(w, ((0, height - w.shape[0]), (0, 0)))


def _even_mixer(x, h, mods, n_ctx, w_in, w_out, w0, w2, a0, a2, g2, k_k, k_a, r_k, ln_x, conv_w):
    T, D = x.shape
    W = D // 2
    o = np.cumsum((0, W, W, W, DECAY_LORA, DECAY_LORA, ICLR_LORA, ICLR_LORA, GATE_LORA, W, W))
    lora = [_pad_cols(w_in[:, o[i]:o[i + 1]], LANES) for i in range(3, 7)]
    w_in_p = jnp.concatenate([w_in[:, :o[3]], w_in[:, o[8]:], *lora, w_in[:, o[7]:o[8]]], axis=1).astype(BF16)
    tm = _pick(T, (1408, 768, 256))
    z = _matmul(h, w_in_p, name="even_w_in", tm=tm, tn=768, tk=2048)
    r_off, k_off, v_off, gb_off, gc_off, u_off = (i * W for i in range(6))
    lo = 6 * W

    def lora_mm(col, kdim, w, bias, act, epi):
        rows = () if bias is None else (bias.reshape(1, W),)
        return _matmul(z, _pad_rows(w, kdim), name="rwkv_lora", tm=tm, tn=1024, tk=kdim, a_col_off=col, a_act=act, rows=rows, epi=epi)

    decay_epi = lambda acc, row0, rows, fulls: -jnp.exp(-_softplus(-(rows[0] + acc)) - 0.5)
    iclr_epi = lambda acc, row0, rows, fulls: jax.nn.sigmoid(rows[0] + acc)
    lw_f = lora_mm(lo, LANES, w2[0], w0[0], jnp.tanh, decay_epi)
    lw_b = lora_mm(lo + LANES, LANES, w2[1], w0[1], jnp.tanh, decay_epi)
    ic_f = lora_mm(lo + 2 * LANES, LANES, a2[0], a0[0], None, iclr_epi)
    ic_b = lora_mm(lo + 3 * LANES, LANES, a2[1], a0[1], None, iclr_epi)
    gate = lora_mm(lo + 4 * LANES, GATE_LORA, g2, None, jax.nn.sigmoid, None)

    y_f, y_b = _rwkv_scan(z, r_off, k_off, v_off, lw_f, ic_f, lw_b, ic_b, k_k, k_a, n_ctx)
    o_rwkv = _rwkv_post(y_f, y_b, z, r_off, k_off, v_off, ic_f, ic_b, gate, k_a, r_k, ln_x)
    o_conv = _short_conv(z, gb_off, gc_off, u_off, conv_w, n_ctx)
    y = jnp.concatenate([o_rwkv, o_conv], axis=1)
    return _matmul(y, w_out.astype(BF16), name="even_w_out", tm=tm, tn=512, tk=2048, rows=(mods[:, 2],), fulls=(x,),
                   epi=_gated_residual_epi(n_ctx))


def _rope_tables(n_ctx, n_lat):
    rows = n_lat // GRID_W
    row = jnp.repeat(jnp.arange(rows), GRID_W)
    col = jnp.tile(jnp.arange(GRID_W), rows)
    pos = jnp.stack([row, col], axis=-1).astype(F32)
    inv_freq = ROPE_BASE ** (-jnp.arange(ROPE_PAIRS, dtype=F32) / ROPE_PAIRS)
    ang = pos[:, :, None, None] * inv_freq
    shape = (n_lat, 2, 2, ROPE_PAIRS)
    cos = jnp.broadcast_to(jnp.cos(ang), shape).reshape(n_lat, QK_ROPE)
    sin = jnp.broadcast_to(jnp.sin(ang), shape).reshape(n_lat, QK_ROPE)
    cos = jnp.concatenate([jnp.ones((n_ctx, QK_ROPE), F32), cos], axis=0)
    sin = jnp.concatenate([jnp.zeros((n_ctx, QK_ROPE), F32), sin], axis=0)
    return jnp.concatenate([cos, sin], axis=1)


def _rot_cols(w):
    lead = w.shape[:-1]
    wr = w.reshape(*lead, 2, 2, ROPE_PAIRS)
    return jnp.stack([-wr[..., 1, :], wr[..., 0, :]], axis=-2).reshape(*lead, QK_ROPE)


def _odd_mixer(x, h, mods, n_ctx, w_in, w_out, q_norm, w_uq, kv_norm, w_ukv):
    T, D = x.shape
    W = D // 2
    n_lat = T - n_ctx
    kr_w = w_in[:, W + Q_LORA + KV_LORA:]
    w_in_p = jnp.concatenate([w_in, _rot_cols(kr_w)], axis=1).astype(BF16)
    w_in_p = _pad_cols(w_in_p, -(-w_in_p.shape[1] // 768) * 768)
    tm = _pick(T, (1408, 768, 256))
    z = _matmul(h, w_in_p, name="odd_w_in", tm=tm, tn=768, tk=2048)
    qa_off, kva_off, kr_off = W, W + Q_LORA, W + Q_LORA + KV_LORA

    qn = _rmsnorm_cols(z, qa_off, Q_LORA, q_norm)
    kvn = _rmsnorm_cols(z, kva_off, KV_LORA, kv_norm)
    uq = w_uq.reshape(Q_LORA, MLA_HEADS, QK_NOPE + QK_ROPE)
    uq_rope = uq[:, :, QK_NOPE:]
    w_uq_p = jnp.concatenate([uq[:, :, :QK_NOPE].reshape(Q_LORA, -1),
                              jnp.concatenate([uq_rope, _rot_cols(uq_rope)], axis=-1).reshape(Q_LORA, -1)], axis=1).astype(BF16)
    tml = _pick(n_lat, (1024, 512, 256))
    q_lat = _matmul(qn[n_ctx:], w_uq_p, name="mla_uq", tm=tml, tn=1024, tk=Q_LORA)
    kv = _matmul(kvn, w_ukv.astype(BF16), name="mla_ukv", tm=tm, tn=1024, tk=KV_LORA, out_dtype=BF16)

    tab = _rope_tables(n_ctx, n_lat)
    q_scale = SM_SCALE * math.log2(math.e)
    tab_lat = tab[n_ctx:]
    q_fin = _q_final(q_lat, tab_lat, q_scale)
    kr = _rope(z, kr_off, 1, tab, 1.0)
    att = _attention(q_fin, kv, kr, n_ctx)
    four = _fourier_mix(z, n_ctx, W)
    y = jnp.concatenate([four, att], axis=1)
    return _matmul(y, w_out.astype(BF16), name="odd_w_out", tm=tml, tn=1024, tk=2048, rows=(mods[:, 2],), fulls=(x[n_ctx:],),
                   epi=_gated_residual_epi(0))


def kernel(x, c, ctx, c_ctx, ada_w, ada_b, norm1_g, norm2_g, ev_w_in, ev_w_out, ev_w0, ev_w2, ev_a0, ev_a2, ev_g2, ev_k_k, ev_k_a, ev_r_k, ev_ln_x, ev_conv_w, od_w_in, od_w_out, od_q_norm, od_w_uq, od_kv_norm, od_w_ukv, router_w, router_b, moe_wg, moe_wu, moe_wd, shared_wg, shared_wu, shared_wd, final_g):
    B, n_lat, D = x.shape
    n_ctx = ctx.shape[1]
    depth = ada_w.shape[0]
    assert B == 1 and depth == 2 and n_ctx == ROW_BLOCK
    xs = jnp.concatenate([ctx[0], x[0]], axis=0)
    cond2 = jnp.concatenate([c_ctx[None], c], axis=0)
    router_w_p = _pad_cols(router_w.astype(F32), LANES)

    moe_w = (router_b, moe_wg, moe_wu, moe_wd, shared_wg, shared_wu, shared_wd)
    mods = _adaln(cond2, ada_w, ada_b, 0)
    mods_odd = _adaln(cond2, ada_w, ada_b, 1)
    h = _modulate(xs, norm1_g[0], mods, 0, 1, n_ctx)
    xs = _even_mixer(xs, h, mods, n_ctx, ev_w_in[0], ev_w_out[0], ev_w0[0], ev_w2[0], ev_a0[0], ev_a2[0], ev_g2[0],
                     ev_k_k[0], ev_k_a[0], ev_r_k[0], ev_ln_x[0], ev_conv_w[0])
    h, logits, h_packed = _modulate(xs, norm2_g[0], mods, 3, 4, n_ctx, router_w=router_w_p)
    xs = _moe_ffn(xs, h, logits, h_packed, *moe_w, 0, mods[:, 5], n_ctx)

    mods = mods_odd
    h = _modulate(xs, norm1_g[1], mods, 0, 1, n_ctx)
    xl = _odd_mixer(xs, h, mods, n_ctx, od_w_in[0], od_w_out[0], od_q_norm[0], od_w_uq[0], od_kv_norm[0], od_w_ukv[0])
    h, logits, h_packed = _modulate(xl, norm2_g[1], mods, 3, 4, 0, router_w=router_w_p)
    out = _moe_ffn(xl, h, logits, h_packed, *moe_w, 1, mods[:, 5], 0, final_g=final_g)
    return out[None]
```

```python
import functools
import math

import numpy as np
import jax
import jax.numpy as jnp
from jax import lax
from jax.experimental import pallas as pl
from jax.experimental.pallas import tpu as pltpu

F32 = jnp.float32
BF16 = jnp.bfloat16
HIGHEST = lax.Precision.HIGHEST

LANES = 128
SUBLANES = 8
VMEM_LIMIT_BYTES = 56 * 1024 * 1024

EPS = 1e-6
GN_EPS = 64e-5
RWKV_HEAD = 64
DECAY_LORA = 96
ICLR_LORA = 96
GATE_LORA = 256
CONV_K = 3
FOURIER_GROUP = 128
FFT_N2 = 128
MLA_HEADS = 16
QK_NOPE = 128
QK_ROPE = 64
V_HEAD = 128
Q_LORA = 1024
KV_LORA = 512
ROPE_PAIRS = QK_ROPE // 4
ROPE_BASE = 10000.0
GRID_W = 64
SM_SCALE = (QK_NOPE + QK_ROPE) ** -0.5
N_EXPERTS = 16
N_GROUPS = 4
EXPERTS_PER_GROUP = N_EXPERTS // N_GROUPS
TOP_K = 2
MOE_BLOCK = 256
ROW_BLOCK = 256
SCAN_CHUNK = 64
SCAN_PAIRS = 2


def _params(*sem):
    return pltpu.CompilerParams(dimension_semantics=sem, vmem_limit_bytes=VMEM_LIMIT_BYTES)


def _pick(n, candidates):
    for c in candidates:
        if n % c == 0:
            return c
    raise ValueError(f"no tile for {n} among {candidates}")


def _mm_body(*refs, nk, a_act, epi, n_rows, n_fulls, tm):
    a_ref, b_ref = refs[0], refs[1]
    row_refs = refs[2:2 + n_rows]
    full_refs = refs[2 + n_rows:2 + n_rows + n_fulls]
    o_ref = refs[2 + n_rows + n_fulls]
    acc_ref = refs[3 + n_rows + n_fulls]
    k = pl.program_id(2)
    av = a_ref[...]
    if a_act is not None:
        av = a_act(av.astype(F32))
    part = jnp.dot(av.astype(BF16), b_ref[...].astype(BF16), preferred_element_type=F32)

    def finish(acc):
        if epi is not None:
            row0 = pl.program_id(0) * tm
            acc = epi(acc, row0, [r[...] for r in row_refs], [f[...] for f in full_refs])
        o_ref[...] = acc.astype(o_ref.dtype)

    if nk == 1:
        finish(part)
        return

    @pl.when(k == 0)
    def _():
        acc_ref[...] = part

    @pl.when(jnp.logical_and(k > 0, k < nk - 1))
    def _():
        acc_ref[...] += part

    @pl.when(k == nk - 1)
    def _():
        finish(acc_ref[...] + part)


def _matmul(a, b, *, name, tm, tn, tk, out_dtype=F32, out_tn=None, a_col_off=0, a_act=None, b_lead=None,
            rows=(), fulls=(), epi=None):
    K, N = b.shape[-2:]
    M = a.shape[0]
    out_tn = tn if out_tn is None else out_tn
    assert M % tm == 0 and N % tn == 0 and K % tk == 0 and a_col_off % tk == 0
    nk = K // tk
    ko = a_col_off // tk
    if b.ndim == 3:
        b_spec = pl.BlockSpec((None, tk, tn), lambda i, j, k: (b_lead, k, j))
    else:
        b_spec = pl.BlockSpec((tk, tn), lambda i, j, k: (k, j))
    in_specs = [pl.BlockSpec((tm, tk), lambda i, j, k: (i, k + ko)), b_spec]
    for r in rows:
        in_specs.append(pl.BlockSpec((r.shape[0], tn), lambda i, j, k: (0, j)))
    for _ in fulls:
        in_specs.append(pl.BlockSpec((tm, out_tn), lambda i, j, k: (i, j)))
    body = functools.partial(_mm_body, nk=nk, a_act=a_act, epi=epi, n_rows=len(rows),
                             n_fulls=len(fulls), tm=tm)
    return pl.pallas_call(
        body,
        grid=(M // tm, N // tn, nk),
        in_specs=in_specs,
        out_specs=pl.BlockSpec((tm, out_tn), lambda i, j, k: (i, j)),
        out_shape=jax.ShapeDtypeStruct((M, N // tn * out_tn), out_dtype),
        scratch_shapes=[pltpu.VMEM((tm, tn) if nk > 1 else (SUBLANES, LANES), F32)],
        compiler_params=_params("parallel", "parallel", "arbitrary"),
        name=name,
    )(a, b, *rows, *fulls)


def _modulate_body(x_ref, g_ref, mod_ref, *rest, shift_idx, scale_idx, with_router):
    xv = x_ref[...]
    y = xv * lax.rsqrt(jnp.mean(xv * xv, axis=-1, keepdims=True) + EPS) * g_ref[...]
    h = y * (1.0 + mod_ref[0, scale_idx:scale_idx + 1, :]) + mod_ref[0, shift_idx:shift_idx + 1, :]
    hb = h.astype(BF16)
    if with_router:
        rw_ref, o_ref, lg_ref, pk_ref = rest
        lg_ref[...] = jnp.dot(h, rw_ref[...], precision=HIGHEST, preferred_element_type=F32)
        half = hb.shape[1] // 2
        bits = lambda t: lax.bitcast_convert_type(t.astype(F32), jnp.uint32)
        pk_ref[...] = (bits(hb[:, half:]) & jnp.uint32(0xFFFF0000)) | (bits(hb[:, :half]) >> 16)
    else:
        (o_ref,) = rest
    o_ref[...] = hb


def _modulate(x, g, mods, shift_idx, scale_idx, n_ctx, router_w=None):
    T, D = x.shape
    tm = ROW_BLOCK
    assert T % tm == 0 and n_ctx % tm == 0
    nc = n_ctx // tm
    in_specs = [
        pl.BlockSpec((tm, D), lambda i: (i, 0)),
        pl.BlockSpec((1, D), lambda i: (0, 0)),
        pl.BlockSpec((1, mods.shape[1], D), lambda i: (jnp.where(i < nc, 0, 1), 0, 0)),
    ]
    out_specs = [pl.BlockSpec((tm, D), lambda i: (i, 0))]
    out_shape = [jax.ShapeDtypeStruct((T, D), BF16)]
    args = [x, g.reshape(1, D), mods]
    if router_w is not None:
        in_specs.append(pl.BlockSpec((D, LANES), lambda i: (0, 0)))
        out_specs += [pl.BlockSpec((tm, LANES), lambda i: (i, 0)), pl.BlockSpec((tm, D // 2), lambda i: (i, 0))]
        out_shape += [jax.ShapeDtypeStruct((T, LANES), F32), jax.ShapeDtypeStruct((T, D // 2), jnp.uint32)]
        args.append(router_w)
    body = functools.partial(_modulate_body, shift_idx=shift_idx, scale_idx=scale_idx,
                             with_router=router_w is not None)
    outs = pl.pallas_call(body, grid=(T // tm,), in_specs=in_specs, out_specs=out_specs,
                          out_shape=out_shape, compiler_params=_params("parallel"), name="modulate")(*args)
    return outs if router_w is not None else outs[0]


def _rmsnorm_cols_body(x_ref, g_ref, o_ref):
    xv = x_ref[...]
    y = xv * lax.rsqrt(jnp.mean(xv * xv, axis=-1, keepdims=True) + EPS) * g_ref[...]
    o_ref[...] = y.astype(o_ref.dtype)


def _rmsnorm_cols(z, col_off, width, g, out_dtype=BF16):
    T = z.shape[0]
    tm = ROW_BLOCK
    assert col_off % width == 0 and T % tm == 0
    cb = col_off // width
    return pl.pallas_call(
        _rmsnorm_cols_body,
        name="rmsnorm_cols",
        grid=(T // tm,),
        in_specs=[pl.BlockSpec((tm, width), lambda i: (i, cb)), pl.BlockSpec((1, width), lambda i: (0, 0))],
        out_specs=pl.BlockSpec((tm, width), lambda i: (i, 0)),
        out_shape=jax.ShapeDtypeStruct((T, width), out_dtype),
        compiler_params=_params("parallel"),
    )(z, g.reshape(1, width))


def _conv_body(gb_ref, gc_ref, u_ref, gcp_ref, up_ref, gcn_ref, un_ref, w_ref, o_ref, *, tb, nb, nc):
    i = pl.program_id(0)
    p = gc_ref[...] * u_ref[...]
    prev_row = (gcp_ref[...] * up_ref[...])[SUBLANES - 1:SUBLANES, :]
    next_row = (gcn_ref[...] * un_ref[...])[0:1, :]
    starts = jnp.logical_or(i == 0, i == nc)
    ends = jnp.logical_or(i == nc - 1, i == nb - 1)
    prev_row = jnp.where(starts, 0.0, prev_row)
    next_row = jnp.where(ends, 0.0, next_row)
    rid = lax.broadcasted_iota(jnp.int32, p.shape, 0)
    xm1 = jnp.where(rid == 0, prev_row, pltpu.roll(p, 1, axis=0))
    xp1 = jnp.where(rid == tb - 1, next_row, pltpu.roll(p, tb - 1, axis=0))
    w = w_ref[...]
    o_ref[...] = (gb_ref[...] * (w[0:1] * xm1 + w[1:2] * p + w[2:3] * xp1)).astype(o_ref.dtype)


def _short_conv(z, gb_off, gc_off, u_off, conv_w, n_ctx):
    T = z.shape[0]
    C = conv_w.shape[1]
    tb, tn = ROW_BLOCK, 512
    nb, nc = T // tb, n_ctx // tb
    hb = tb // SUBLANES
    last_h = T // SUBLANES - 1
    cur = lambda off: pl.BlockSpec((tb, tn), lambda i, j: (i, off // tn + j))
    prv = lambda off: pl.BlockSpec((SUBLANES, tn), lambda i, j: (jnp.maximum(i * hb - 1, 0), off // tn + j))
    nxt = lambda off: pl.BlockSpec((SUBLANES, tn), lambda i, j: (jnp.minimum((i + 1) * hb, last_h), off // tn + j))
    return pl.pallas_call(
        functools.partial(_conv_body, tb=tb, nb=nb, nc=nc),
        name="short_conv",
        grid=(nb, C // tn),
        in_specs=[cur(gb_off), cur(gc_off), cur(u_off), prv(gc_off), prv(u_off), nxt(gc_off), nxt(u_off),
                  pl.BlockSpec((CONV_K, tn), lambda i, j: (0, j))],
        out_specs=pl.BlockSpec((tb, tn), lambda i, j: (i, j)),
        out_shape=jax.ShapeDtypeStruct((T, C), BF16),
        compiler_params=_params("parallel", "parallel"),
    )(z, z, z, z, z, z, z, conv_w)


NN = (((1,), (0,)), ((), ()))
NT = (((1,), (1,)), ((), ()))
TN = (((0,), (0,)), ((), ()))

SCAN_PIECES_GRAM = 1
SCAN_PIECES_INV = 1
SCAN_PIECES_OUT = 1


def _pieces(x, n):
    out = []
    for i in range(n):
        p = x.astype(BF16)
        out.append(p)
        if i + 1 < n:
            x = x - p.astype(F32)
    return out


def _pdot(ap, bp, dims=NN):
    order = max(len(ap), len(bp))
    acc = None
    for i, x in enumerate(ap):
        for j, y in enumerate(bp):
            if i + j < order:
                t = lax.dot_general(x, y, dims, preferred_element_type=F32)
                acc = t if acc is None else acc + t
    return acc


def _pcat(parts, axis):
    return [jnp.concatenate(ps, axis=axis) for ps in zip(*parts)]


def _scan_stages(C, masks):
    n = 2 * C
    reverse, in_h0, row_c, strict, incl, eye, diag_blocks, off_blocks = masks
    pg, pi, po = SCAN_PIECES_GRAM, SCAN_PIECES_INV, SCAN_PIECES_OUT

    def stack(x):
        return jnp.concatenate([jnp.where(in_h0, x, 0.0), jnp.where(in_h0, 0.0, x)], axis=0)

    def s_cum(d):
        for nm in ("lw", "r", "k", "v", "a", "b"):
            d[nm + "_s"] = stack(d[nm])
        x = d["lw"]
        sh = 1
        while sh < C:
            if reverse:
                x = x + jnp.where(row_c < C - sh, pltpu.roll(x, C - sh, axis=0), 0.0)
            else:
                x = x + jnp.where(row_c >= sh, pltpu.roll(x, sh, axis=0), 0.0)
            sh *= 2
        d["cum"] = stack(x)
        d["tot"] = x[0:1] if reverse else x[C - 1:C]

    def s_exp(d):
        cum, tot = d["cum"], d["tot"]
        inv, fin = jnp.exp(-cum), jnp.exp(tot - cum)
        d["r_hat"] = d["r_s"] * jnp.exp(cum)
        d["a_hat_p"] = _pieces(d["a_s"] * jnp.exp(cum - d["lw_s"]), max(pg, po))
        d["r_hat_p"] = _pieces(d["r_hat"], pg)
        d["bk_chk_p"] = _pcat([_pieces(d["b_s"] * inv, pg), _pieces(d["k_s"] * inv, pg)], 0)
        d["b_til_p"] = _pieces(d["b_s"] * fin, po)
        d["bk_til_p"] = _pcat([d["b_til_p"], _pieces(d["k_s"] * fin, po)], 0)
        d["v_p"] = _pieces(d["v_s"], po)
        d["decay"] = jnp.broadcast_to(jnp.exp(tot), (LANES, LANES)).T

    def s_gram(d):
        g = _pdot(_pcat([d["a_hat_p"][:pg], d["r_hat_p"]], 0), d["bk_chk_p"], NT)
        d["n_ab"] = jnp.where(strict, g[:n, :n], 0.0)
        d["m_ak_p"] = _pieces(jnp.where(strict, g[:n, n:], 0.0), po)
        d["m_rb_p"] = _pieces(jnp.where(incl, g[n:, :n], 0.0), po)
        d["m_rk_p"] = _pieces(jnp.where(incl, g[n:, n:], 0.0), po)
        nd = jnp.where(diag_blocks, d["n_ab"], 0.0)
        d["nd_p"] = _pieces(nd, pi)
        d["t"] = eye + nd

    def s_sq1(d):
        d["pw"] = _pdot(d["nd_p"], d["nd_p"])

    def s_ap1(d):
        d["pw_p"] = _pieces(d["pw"], pi)
        d["t"] = d["t"] + _pdot(d["pw_p"], _pieces(d["t"], pi))

    def s_sq2(d):
        d["pw_p"] = _pieces(_pdot(d["pw_p"], d["pw_p"]), pi)

    def s_ap2(d):
        d["t"] = d["t"] + _pdot(d["pw_p"], _pieces(d["t"], pi))

    def s_merge_a(off_mask):
        def f(d):
            d["t_p"] = _pieces(d["t"], pi)
            d["ot_p"] = _pieces(_pdot(_pieces(jnp.where(off_mask, d["n_ab"], 0.0), pi), d["t_p"]), pi)
        return f

    def s_merge_b(d):
        d["t"] = d["t"] + _pdot(d["t_p"], d["ot_p"])

    def s_abar(d):
        d["t_p"] = _pieces(d["t"], po)
        d["abar_p"] = _pieces(_pdot(d["t_p"], d["a_hat_p"][:po]), po)
        d["mv_p"] = _pieces(_pdot(d["m_ak_p"], d["v_p"]), po)

    def s_u0(d):
        d["uv_p"] = _pcat([_pieces(_pdot(d["t_p"], d["mv_p"]), po), d["v_p"]], 0)
        d["rbar_p"] = _pieces(d["r_hat"] + _pdot(d["m_rb_p"], d["abar_p"]), po)
        d["phi_p"] = _pieces(_pdot(d["b_til_p"], d["abar_p"], TN), po)

    def s_out(d):
        d["y0"] = _pdot(_pcat([d["m_rb_p"], d["m_rk_p"]], 1), d["uv_p"])
        d["s0"] = _pdot(d["bk_til_p"], d["uv_p"], TN)

    stages = [s_cum, s_exp, s_gram, s_sq1, s_ap1, s_sq2, s_ap2]
    for off_mask in off_blocks:
        stages += [s_merge_a(off_mask), s_merge_b]
    return stages + [s_abar, s_u0, s_out]


def _scan_apply(d, state):
    C = d["r"].shape[0]
    st_p = _pieces(state, SCAN_PIECES_OUT)
    y = _pdot(d["rbar_p"], st_p) + d["y0"]
    return y[:C] + y[C:], d["decay"] * state + _pdot(d["phi_p"], st_p) + d["s0"]


def _scan_masks(C, reverse):
    n = 2 * C
    lane = lax.broadcasted_iota(jnp.int32, (C, LANES), 1)
    row = lax.broadcasted_iota(jnp.int32, (n, n), 0)
    col = lax.broadcasted_iota(jnp.int32, (n, n), 1)
    same = (row // C) == (col // C)
    before = (col > row) if reverse else (col < row)
    strict = jnp.logical_and(same, before)
    incl = jnp.logical_and(same, jnp.logical_or(before, col == row))
    eye = jnp.where(row == col, 1.0, 0.0)
    row_c = lax.broadcasted_iota(jnp.int32, (C, LANES), 0)
    blk = 8
    diag_blocks = (row // blk) == (col // blk)
    off_blocks = []
    while blk < C:
        off_blocks.append(jnp.logical_and((row // (2 * blk)) == (col // (2 * blk)), (row // blk) != (col // blk)))
        blk *= 2
    return reverse, lane < RWKV_HEAD, row_c, strict, incl, eye, diag_blocks, off_blocks


def _scan_body(rf, kf, vf, lwf, icf, rb, kb, vb, lwb, icb, kk_ref, ka_ref, seg_ref, yf_ref, yb_ref, sf_ref, sb_ref, *, tb):
    @pl.when(pl.program_id(1) == 0)
    def _():
        sf_ref[...] = jnp.zeros_like(sf_ref)
        sb_ref[...] = jnp.zeros_like(sb_ref)

    seg = seg_ref[...].astype(BF16)
    nch = tb // SCAN_CHUNK
    scans = []
    for refs, y_ref, s_ref, reverse in ((rf, kf, vf, lwf, icf), yf_ref, sf_ref, False), ((rb, kb, vb, lwb, icb), yb_ref, sb_ref, True):
        stages = _scan_stages(SCAN_CHUNK, _scan_masks(SCAN_CHUNK, reverse))
        for p in range(SCAN_PAIRS):
            lanes = slice(p * LANES, (p + 1) * LANES)
            r, k, v, lw, ic = (t[:, lanes] for t in refs)
            k_k, k_a = kk_ref[:, lanes], ka_ref[:, lanes]
            kk = k * k_k
            kk = kk * lax.rsqrt(_pdot(_pieces(kk * kk, 3), [seg]) + 1e-12)
            a, b = -kk, kk * ic
            kd = k * (1.0 + (ic - 1.0) * k_a)
            chunks = []
            for c in (range(nch - 1, -1, -1) if reverse else range(nch)):
                sl = slice(c * SCAN_CHUNK, (c + 1) * SCAN_CHUNK)
                chunks.append(dict(r=r[sl], k=kd[sl], v=v[sl], a=a[sl], b=b[sl], lw=lw[sl], rows=sl))
            scans.append((chunks, stages, y_ref, s_ref, p, lanes))

    for step in range(len(scans[0][1])):
        for chunks, stages, *_ in scans:
            for d in chunks:
                stages[step](d)
    states = [s_ref[p] for _, _, _, s_ref, p, _ in scans]
    for c in range(nch):
        for i, (chunks, _, y_ref, _, _, lanes) in enumerate(scans):
            y, states[i] = _scan_apply(chunks[c], states[i])
            y_ref[chunks[c]["rows"], lanes] = y
    for (_, _, _, s_ref, p, _), state in zip(scans, states):
        s_ref[p] = state


def _rwkv_scan(z, r_off, k_off, v_off, lw_f, ic_f, lw_b, ic_b, k_k, k_a, n_ctx):
    T = z.shape[0]
    W = lw_f.shape[1]
    tb = ROW_BLOCK
    assert n_ctx == tb and T % tb == 0
    nb = T // tb
    tw = SCAN_PAIRS * LANES
    fwd = lambda i: i
    bwd = lambda i: jnp.where(i == 0, 0, nb - i)
    zspec = lambda off, o: pl.BlockSpec((tb, tw), lambda p, i: (o(i), off // tw + p))
    wspec = lambda o: pl.BlockSpec((tb, tw), lambda p, i: (o(i), p))
    par = pl.BlockSpec((1, tw), lambda p, i: (0, p))
    lane = np.arange(LANES)
    seg = jnp.asarray((lane[:, None] // RWKV_HEAD == lane[None, :] // RWKV_HEAD).astype(np.float32))
    return pl.pallas_call(
        functools.partial(_scan_body, tb=tb),
        name="rwkv_scan",
        grid=(W // tw, nb),
        in_specs=[zspec(r_off, fwd), zspec(k_off, fwd), zspec(v_off, fwd), wspec(fwd), wspec(fwd),
                  zspec(r_off, bwd), zspec(k_off, bwd), zspec(v_off, bwd), wspec(bwd), wspec(bwd),
                  par, par, pl.BlockSpec((LANES, LANES), lambda p, i: (0, 0))],
        out_specs=[wspec(fwd), wspec(bwd)],
        out_shape=[jax.ShapeDtypeStruct((T, W), F32)] * 2,
        scratch_shapes=[pltpu.VMEM((SCAN_PAIRS, LANES, LANES), F32)] * 2,
        compiler_params=_params("parallel", "arbitrary"),
    )(z, z, z, lw_f, ic_f, z, z, z, lw_b, ic_b, k_k.reshape(1, W), k_a.reshape(1, W), seg)


def _rwkv_post_body(yf, yb, r, k, v, icf, icb, gate, ka, rk, lnx, seg_ref, o_ref):
    seg = [seg_ref[...].astype(BF16)]
    head_sum = lambda t: _pdot(_pieces(t, 3), seg)
    inv_n = 1.0 / RWKV_HEAD
    wkv = yf[...] + yb[...]
    yc = wkv - head_sum(wkv) * inv_n
    yn = yc * lax.rsqrt(head_sum(yc * yc) * inv_n + GN_EPS) * lnx[...]
    kv, kav = k[...], ka[...]
    kd_sum = kv * (1.0 + (icf[...] - 1.0) * kav) + kv * (1.0 + (icb[...] - 1.0) * kav)
    bonus = head_sum(r[...] * kd_sum * rk[...]) * v[...]
    o_ref[...] = ((yn + bonus) * gate[...]).astype(o_ref.dtype)


def _rwkv_post(y_f, y_b, z, r_off, k_off, v_off, ic_f, ic_b, gate, k_a, r_k, ln_x):
    T, W = y_f.shape
    tb, tn = _pick(T, (768, 256)), 256
    blk = pl.BlockSpec((tb, tn), lambda i, j: (i, j))
    zspec = lambda off: pl.BlockSpec((tb, tn), lambda i, j: (i, off // tn + j))
    par = pl.BlockSpec((1, tn), lambda i, j: (0, j))
    lane = np.arange(tn)
    seg = jnp.asarray((lane[:, None] // RWKV_HEAD == lane[None, :] // RWKV_HEAD).astype(np.float32))
    return pl.pallas_call(
        _rwkv_post_body,
        name="rwkv_post",
        grid=(T // tb, W // tn),
        in_specs=[blk, blk, zspec(r_off), zspec(k_off), zspec(v_off), blk, blk, blk, par, par, par,
                  pl.BlockSpec((tn, tn), lambda i, j: (0, 0))],
        out_specs=blk,
        out_shape=jax.ShapeDtypeStruct((T, W), BF16),
        compiler_params=_params("parallel", "parallel"),
    )(y_f, y_b, z, z, z, ic_f, ic_b, gate, k_a.reshape(1, W), r_k.reshape(1, W), ln_x.reshape(1, W), seg)


def _dft_cols_body(u_ref, cs_ref, o_ref):
    G = FOURIER_GROUP
    for g in range(u_ref.shape[1] // G):
        pq = jnp.dot(u_ref[:, g * G:(g + 1) * G].astype(BF16), cs_ref[...], preferred_element_type=F32)
        o_ref[0, :, g * G:(g + 1) * G] = pq[:, :G].astype(o_ref.dtype)
        o_ref[1, :, g * G:(g + 1) * G] = pq[:, G:].astype(o_ref.dtype)


def _fft_stage1_body(z_ref, e_ref, o_ref):
    n2 = z_ref.shape[1]
    w = jnp.dot(e_ref[...], jnp.concatenate([z_ref[0], z_ref[1]], axis=0), preferred_element_type=F32)
    o_ref[0] = w[:n2].astype(o_ref.dtype)
    o_ref[1] = w[n2:].astype(o_ref.dtype)


def _fourier_mix(z, n_ctx, width):
    T = z.shape[0] - n_ctx
    tm = ROW_BLOCK
    ro = n_ctx // tm
    G = FOURIER_GROUP
    tn = _pick(width, (1024, 512, G))
    c = np.arange(G)
    ang_c = 2.0 * np.pi * ((c[:, None] * c[None, :]) % G) / G
    cs = jnp.asarray(np.concatenate([np.cos(ang_c), np.sin(ang_c)], axis=1), BF16)
    pq = pl.pallas_call(
        _dft_cols_body,
        name="dft_cols",
        grid=(T // tm, width // tn),
        in_specs=[pl.BlockSpec((tm, tn), lambda i, g: (i + ro, g)), pl.BlockSpec((G, 2 * G), lambda i, g: (0, 0))],
        out_specs=pl.BlockSpec((2, tm, tn), lambda i, g: (0, i, g)),
        out_shape=jax.ShapeDtypeStruct((2, T, width), BF16),
        compiler_params=_params("parallel", "parallel"),
    )(z, cs)
    n2 = FFT_N2
    n1 = T // n2
    zp = pq.reshape(2, n2, n1, width).transpose(0, 2, 1, 3)
    t = jnp.arange(n1, dtype=jnp.int32)[:, None, None] + n1 * jnp.arange(n2, dtype=jnp.int32)[None, None, :]
    ang = ((jnp.arange(n2, dtype=jnp.int32)[None, :, None] * t) % T).astype(F32) * (2.0 * math.pi / T)
    ec, es = jnp.cos(ang), jnp.sin(ang)
    e1 = jnp.concatenate([jnp.concatenate([ec, -es], axis=2), jnp.concatenate([-es, -ec], axis=2)], axis=1).astype(BF16)
    w = pl.pallas_call(
        _fft_stage1_body,
        name="fft_stage1",
        grid=(n1,),
        in_specs=[pl.BlockSpec((2, None, n2, width), lambda i: (0, i, 0, 0)),
                  pl.BlockSpec((None, 2 * n2, 2 * n2), lambda i: (i, 0, 0))],
        out_specs=pl.BlockSpec((2, None, n2, width), lambda i: (0, i, 0, 0)),
        out_shape=jax.ShapeDtypeStruct((2, n1, n2, width), BF16),
        compiler_params=_params("parallel"),
    )(zp, e1)
    a1 = np.arange(n1)
    ang1 = 2.0 * np.pi * ((a1[:, None] * a1[None, :]) % n1) / n1
    f1 = jnp.asarray(np.concatenate([np.cos(ang1), np.sin(ang1)], axis=1), BF16)
    scale = 1.0 / math.sqrt(T * G)
    out = _matmul(f1, w.reshape(2 * n1, n2 * width), name="fft_stage2", tm=n1, tn=_pick(n2 * width, (8192, 1024)),
                  tk=2 * n1, out_dtype=BF16, epi=lambda acc, row0, rows, fulls: acc * scale)
    return out.reshape(T, width)


def _rope_body(x_ref, tab_ref, o_ref, *, scale, keep_dup):
    xt = x_ref[...] * tab_ref[...]
    y = xt + pltpu.roll(xt, QK_ROPE, axis=1)
    if not keep_dup:
        lane = lax.broadcasted_iota(jnp.int32, y.shape, 1)
        y = jnp.where(lane < QK_ROPE, y, 0.0)
    o_ref[...] = (y * scale).astype(o_ref.dtype)


def _rope(z, col_off, n_heads, tab, scale):
    T = z.shape[0]
    tm = ROW_BLOCK
    cb = col_off // LANES
    return pl.pallas_call(
        functools.partial(_rope_body, scale=scale, keep_dup=False),
        name="rope",
        grid=(T // tm, n_heads),
        in_specs=[pl.BlockSpec((tm, LANES), lambda i, h: (i, cb + h)), pl.BlockSpec((tm, LANES), lambda i, h: (i, 0))],
        out_specs=pl.BlockSpec((tm, LANES), lambda i, h: (i, h)),
        out_shape=jax.ShapeDtypeStruct((T, n_heads * LANES), BF16),
        compiler_params=_params("parallel", "parallel"),
    )(z, tab)


def _q_final_body(qn_ref, qr_ref, tab_ref, o_ref, *, scale):
    xt = qr_ref[...] * tab_ref[...]
    y = xt + pltpu.roll(xt, QK_ROPE, axis=1)
    lane = lax.broadcasted_iota(jnp.int32, y.shape, 1)
    o_ref[:, :QK_NOPE] = (qn_ref[...] * scale).astype(o_ref.dtype)
    o_ref[:, QK_NOPE:] = (jnp.where(lane < QK_ROPE, y, 0.0) * scale).astype(o_ref.dtype)


def _q_final(q, tab, scale):
    T = q.shape[0]
    tm = _pick(T, (1024, 512, 256))
    blk = lambda off: pl.BlockSpec((tm, LANES), lambda i, h: (i, off + h))
    return pl.pallas_call(
        functools.partial(_q_final_body, scale=scale),
        name="q_final",
        grid=(T // tm, MLA_HEADS),
        in_specs=[blk(0), blk(MLA_HEADS), pl.BlockSpec((tm, LANES), lambda i, h: (i, 0))],
        out_specs=pl.BlockSpec((tm, 2 * LANES), lambda i, h: (i, h)),
        out_shape=jax.ShapeDtypeStruct((T, MLA_HEADS * 2 * LANES), BF16),
        compiler_params=_params("parallel", "parallel"),
    )(q, q, tab)


def _attn_body(q_ref, kn_ref, kr_ref, v_ref, o_ref, kc_ref, vc_ref, *, ts):
    @pl.when(pl.program_id(1) == 0)
    def _():
        kc_ref[:, :QK_NOPE] = kn_ref[...]
        kc_ref[:, QK_NOPE:] = kr_ref[...]
        vc_ref[:, :V_HEAD] = v_ref[...]
        vc_ref[:, V_HEAD:] = jnp.ones(v_ref.shape, BF16)

    q = q_ref[...]
    n_sub = kc_ref.shape[0] // ts

    def logits(c):
        return lax.dot_general(q, kc_ref[c * ts:(c + 1) * ts, :], NT, preferred_element_type=F32)

    def update(c, s, m_old, acc):
        m_new = jnp.maximum(m_old, jnp.max(s, axis=-1, keepdims=True))
        p = jnp.exp2(s - m_new).astype(BF16)
        pv = jnp.dot(p, vc_ref[c * ts:(c + 1) * ts, :], preferred_element_type=F32)
        return m_new, (pv if acc is None else jnp.exp2(m_old - m_new) * acc + pv)

    m = jnp.full((q.shape[0], 1), -jnp.inf, F32)
    acc = None
    s_prev = logits(0)
    for c in range(1, n_sub):
        s_next = logits(c)
        m, acc = update(c - 1, s_prev, m, acc)
        s_prev = s_next
    m, acc = update(n_sub - 1, s_prev, m, acc)
    o_ref[...] = (acc[:, :V_HEAD] / acc[:, V_HEAD:]).astype(o_ref.dtype)


def _attention(q, kv, kr, n_ctx):
    T = kv.shape[0]
    Tq = q.shape[0]
    tq = _pick(Tq, (1024, 512, 256))
    ts = _pick(T, (768, 256))
    kv_spec = lambda off: pl.BlockSpec((T, LANES), lambda h, i: (0, 2 * h + off))
    return pl.pallas_call(
        functools.partial(_attn_body, ts=ts),
        name="mla_attention",
        grid=(MLA_HEADS, Tq // tq),
        in_specs=[pl.BlockSpec((tq, 2 * LANES), lambda h, i: (i, h)), kv_spec(0),
                  pl.BlockSpec((T, LANES), lambda h, i: (0, 0)), kv_spec(1)],
        out_specs=pl.BlockSpec((tq, LANES), lambda h, i: (i, h)),
        out_shape=jax.ShapeDtypeStruct((Tq, MLA_HEADS * V_HEAD), BF16),
        scratch_shapes=[pltpu.VMEM((T, 2 * LANES), BF16), pltpu.VMEM((T, 2 * LANES), BF16)],
        compiler_params=_params("arbitrary", "arbitrary"),
    )(q, kv, kr, kv)


def _moe_up_body(be_ref, x_ref, wg_ref, wu_ref, o_ref, wgb_ref, wub_ref):
    b = pl.program_id(1)
    changed = jnp.logical_or(b == 0, be_ref[b] != be_ref[jnp.maximum(b - 1, 0)])

    @pl.when(changed)
    def _():
        wgb_ref[...] = wg_ref[...].astype(BF16)
        wub_ref[...] = wu_ref[...].astype(BF16)

    used = b < be_ref[pl.num_programs(1)]

    @pl.when(used)
    def _():
        xw = x_ref[...]
        half = xw.shape[1]
        as_bf16 = lambda bits: lax.bitcast_convert_type(bits, F32).astype(BF16)
        x_lo, x_hi = as_bf16(xw << 16), as_bf16(xw & jnp.uint32(0xFFFF0000))

        def proj(w_ref):
            return (jnp.dot(x_lo, w_ref[:half, :], preferred_element_type=F32)
                    + jnp.dot(x_hi, w_ref[half:, :], preferred_element_type=F32))

        gate, up = proj(wgb_ref), proj(wub_ref)
        o_ref[...] = (gate * jax.nn.sigmoid(gate) * up).astype(o_ref.dtype)

    @pl.when(jnp.logical_not(used))
    def _():
        o_ref[...] = jnp.zeros_like(o_ref)


def _moe_up(xb, block_e, wg, wu, layer):
    n_rows = xb.shape[0]
    D, DE = wg.shape[2:]
    tb, tn = MOE_BLOCK, 512
    w_spec = pl.BlockSpec((None, None, D, tn), lambda n, b, be: (layer, be[b], 0, n))
    return pl.pallas_call(
        _moe_up_body,
        name="moe_up",
        grid_spec=pltpu.PrefetchScalarGridSpec(
            num_scalar_prefetch=1,
            grid=(DE // tn, n_rows // tb),
            in_specs=[pl.BlockSpec((tb, D // 2), lambda n, b, be: (b, 0)), w_spec, w_spec],
            out_specs=pl.BlockSpec((tb, tn), lambda n, b, be: (b, n)),
            scratch_shapes=[pltpu.VMEM((D, tn), BF16), pltpu.VMEM((D, tn), BF16)]),
        out_shape=jax.ShapeDtypeStruct((n_rows, DE), BF16),
        compiler_params=_params("arbitrary", "arbitrary"),
    )(block_e, xb, wg, wu)


def _moe_down_body(be_ref, h_ref, wd_ref, sw_ref, o_ref, wdb_ref):
    b = pl.program_id(1)
    changed = jnp.logical_or(b == 0, be_ref[b] != be_ref[jnp.maximum(b - 1, 0)])

    @pl.when(changed)
    def _():
        wdb_ref[...] = wd_ref[...].astype(BF16)

    used = b < be_ref[pl.num_programs(1)]

    @pl.when(used)
    def _():
        o_ref[...] = jnp.dot(h_ref[...], wdb_ref[...], preferred_element_type=F32) * sw_ref[...]

    @pl.when(jnp.logical_not(used))
    def _():
        o_ref[...] = jnp.zeros_like(o_ref)


def _moe_down(h, block_e, wd, slot_w, layer):
    n_rows, DE = h.shape
    D = wd.shape[3]
    tb, tn = MOE_BLOCK, 2048
    return pl.pallas_call(
        _moe_down_body,
        name="moe_down",
        grid_spec=pltpu.PrefetchScalarGridSpec(
            num_scalar_prefetch=1,
            grid=(D // tn, n_rows // tb),
            in_specs=[pl.BlockSpec((tb, DE), lambda n, b, be: (b, 0)),
                      pl.BlockSpec((None, None, DE, tn), lambda n, b, be: (layer, be[b], 0, n)),
                      pl.BlockSpec((tb, 1), lambda n, b, be: (b, 0))],
            out_specs=pl.BlockSpec((tb, tn), lambda n, b, be: (b, n)),
            scratch_shapes=[pltpu.VMEM((DE, tn), BF16)]),
        out_shape=jax.ShapeDtypeStruct((n_rows, D), F32),
        compiler_params=_params("arbitrary", "arbitrary"),
    )(block_e, h, wd, slot_w.reshape(n_rows, 1))


def _shared_up_body(a_ref, wg_ref, wu_ref, o_ref, wgb_ref, wub_ref):
    @pl.when(pl.program_id(1) == 0)
    def _():
        wgb_ref[...] = wg_ref[...].astype(BF16)
        wub_ref[...] = wu_ref[...].astype(BF16)

    a = a_ref[...]
    gate = jnp.dot(a, wgb_ref[...], preferred_element_type=F32)
    up = jnp.dot(a, wub_ref[...], preferred_element_type=F32)
    o_ref[...] = (gate * jax.nn.sigmoid(gate) * up).astype(o_ref.dtype)


def _shared_up(h, sg, su, layer):
    T, D = h.shape
    DE = sg.shape[2]
    tm, tn = _pick(T, (1408, 1024, 768, 256)), 256
    w_spec = pl.BlockSpec((None, D, tn), lambda j, i: (layer, 0, j))
    return pl.pallas_call(
        _shared_up_body,
        name="shared_up",
        grid=(DE // tn, T // tm),
        in_specs=[pl.BlockSpec((tm, D), lambda j, i: (i, 0)), w_spec, w_spec],
        out_specs=pl.BlockSpec((tm, tn), lambda j, i: (i, j)),
        out_shape=jax.ShapeDtypeStruct((T, DE), BF16),
        scratch_shapes=[pltpu.VMEM((D, tn), BF16), pltpu.VMEM((D, tn), BF16)],
        compiler_params=_params("arbitrary", "arbitrary"),
    )(h, sg, su)


def _route(logits, router_b):
    T = logits.shape[0]
    s = jax.nn.sigmoid(logits)
    sel = (s + router_b.astype(F32)).reshape(T, N_GROUPS, EXPERTS_PER_GROUP)

    def top2(v):
        pos = jnp.arange(v.shape[-1])
        i0 = jnp.argmax(v, axis=-1)
        rest = jnp.where(pos == i0[..., None], -jnp.inf, v)
        return i0, jnp.argmax(rest, axis=-1), jnp.max(v, axis=-1), jnp.max(rest, axis=-1)

    _, _, v0, v1 = top2(sel)
    grp = jnp.argmax(v0 + v1, axis=-1)
    in_grp = jnp.arange(N_GROUPS)[None, :, None] == grp[:, None, None]
    loc0, loc1, _, _ = top2(jnp.sum(jnp.where(in_grp, sel, 0.0), axis=1))
    idx = grp[:, None] * EXPERTS_PER_GROUP + jnp.stack([loc0, loc1], axis=-1)
    picked = jnp.arange(N_EXPERTS)[None, None, :] == idx[:, :, None]
    wts = jnp.sum(jnp.where(picked, s[:, None, :], 0.0), axis=-1)
    return idx, wts / jnp.sum(wts, axis=-1, keepdims=True)


def _moe_ffn(x, h, logits, h_packed, router_b, wg, wu, wd, sg, su, sd, layer, gates, n_ctx, final_g=None):
    T, D = h.shape
    DE = sg.shape[2]
    hs = _shared_up(h, sg, su, layer)

    idx, wts = _route(logits[:, :N_EXPERTS], router_b)
    A = T * TOP_K
    flat_e = idx.reshape(A)
    onehot = (flat_e[:, None] == jnp.arange(N_EXPERTS)[None, :]).astype(jnp.int32)
    csum = jnp.cumsum(onehot, axis=0)
    counts = csum[-1]
    rank = jnp.sum(onehot * csum, axis=1) - 1
    padded = (counts + MOE_BLOCK - 1) // MOE_BLOCK * MOE_BLOCK
    pad_end = jnp.cumsum(padded)
    pad_start = pad_end - padded
    dest = (jnp.sum(onehot * pad_start[None, :], axis=1) + rank).astype(jnp.int32)
    n_blocks = -(-A // MOE_BLOCK) + N_EXPERTS
    n_slots = n_blocks * MOE_BLOCK
    slot_tok = (jnp.arange(n_slots, dtype=jnp.int32) % T).at[dest].set(jnp.arange(A, dtype=jnp.int32) // TOP_K)
    slot_w = jnp.zeros((n_slots,), F32).at[dest].set(wts.reshape(A))
    xb = h_packed[slot_tok]
    block_e = jnp.minimum(jnp.sum(pad_end[None, :] <= (jnp.arange(n_blocks) * MOE_BLOCK)[:, None], axis=1),
                          N_EXPERTS - 1).astype(jnp.int32)
    block_e = jnp.concatenate([block_e, (pad_end[-1:] // MOE_BLOCK).astype(jnp.int32)])
    y_slots = _moe_down(_moe_up(xb, block_e, wg, wu, layer), block_e, wd, slot_w, layer)
    slot_of = dest.reshape(T, TOP_K)
    routed = y_slots[slot_of[:, 0]] + y_slots[slot_of[:, 1]]

    def combine_epi(acc, row0, rows, fulls):
        rid = row0 + lax.broadcasted_iota(jnp.int32, acc.shape, 0)
        gate = jnp.where(rid < n_ctx, rows[0][0:1], rows[0][1:2])
        y = fulls[0] + gate * (fulls[1] + acc)
        if final_g is not None:
            y = y * lax.rsqrt(jnp.mean(y * y, axis=-1, keepdims=True) + EPS) * rows[1]
        return y

    rows = (gates,) if final_g is None else (gates, final_g.reshape(1, D))
    return _matmul(hs, sd[layer].astype(BF16), name="shared_down_combine", tm=ROW_BLOCK, tn=D, tk=DE,
                   rows=rows, fulls=(x, routed), epi=combine_epi)


def _silu(x):
    return x * jax.nn.sigmoid(x)


def _softplus(x):
    return jnp.maximum(x, 0.0) + jnp.log(1.0 + jnp.exp(-jnp.abs(x)))


def _adaln(cond2, w, b, layer):
    D = cond2.shape[1]
    a = jnp.zeros((SUBLANES, D), F32).at[:2].set(cond2)
    out = _matmul(a, w, name="adaln", tm=SUBLANES, tn=512, tk=D, a_act=_silu, b_lead=layer,
                  rows=(b[layer].reshape(1, -1),), epi=lambda acc, row0, rows, fulls: acc + rows[0])
    return out[:2].reshape(2, -1, D)


def _gated_residual_epi(n_ctx):
    def epi(acc, row0, rows, fulls):
        rid = row0 + lax.broadcasted_iota(jnp.int32, acc.shape, 0)
        gate = jnp.where(rid < n_ctx, rows[0][0:1], rows[0][1:2])
        return fulls[0] + gate * acc
    return epi


def _pad_cols(w, width):
    return jnp.pad(w, ((0, 0), (0, width - w.shape[1])))


def _pad_rows(w, height):
    return jnp.pad(w, ((0, height - w.shape[0]), (0, 0)))


def _even_mixer(x, h, mods, n_ctx, w_in, w_out, w0, w2, a0, a2, g2, k_k, k_a, r_k, ln_x, conv_w):
    T, D = x.shape
    W = D // 2
    o = np.cumsum((0, W, W, W, DECAY_LORA, DECAY_LORA, ICLR_LORA, ICLR_LORA, GATE_LORA, W, W))
    lora = [_pad_cols(w_in[:, o[i]:o[i + 1]], LANES) for i in range(3, 7)]
    w_in_p = jnp.concatenate([w_in[:, :o[3]], w_in[:, o[8]:], *lora, w_in[:, o[7]:o[8]]], axis=1).astype(BF16)
    tm = _pick(T, (1408, 768, 256))
    z = _matmul(h, w_in_p, name="even_w_in", tm=_pick(T, (704, 768, 256)), tn=768, tk=D)
    r_off, k_off, v_off, gb_off, gc_off, u_off = (i * W for i in range(6))
    lo = 6 * W

    def lora_mm(col, kdim, w, bias, act, epi):
        rows = () if bias is None else (bias.reshape(1, W),)
        return _matmul(z, _pad_rows(w, kdim), name="rwkv_lora", tm=tm, tn=1024, tk=kdim, a_col_off=col, a_act=act, rows=rows, epi=epi)

    decay_epi = lambda acc, row0, rows, fulls: -jnp.exp(-_softplus(-(rows[0] + acc)) - 0.5)
    iclr_epi = lambda acc, row0, rows, fulls: jax.nn.sigmoid(rows[0] + acc)
    lw_f = lora_mm(lo, LANES, w2[0], w0[0], jnp.tanh, decay_epi)
    lw_b = lora_mm(lo + LANES, LANES, w2[1], w0[1], jnp.tanh, decay_epi)
    ic_f = lora_mm(lo + 2 * LANES, LANES, a2[0], a0[0], None, iclr_epi)
    ic_b = lora_mm(lo + 3 * LANES, LANES, a2[1], a0[1], None, iclr_epi)
    gate = lora_mm(lo + 4 * LANES, GATE_LORA, g2, None, jax.nn.sigmoid, None)

    y_f, y_b = _rwkv_scan(z, r_off, k_off, v_off, lw_f, ic_f, lw_b, ic_b, k_k, k_a, n_ctx)
    o_rwkv = _rwkv_post(y_f, y_b, z, r_off, k_off, v_off, ic_f, ic_b, gate, k_a, r_k, ln_x)
    o_conv = _short_conv(z, gb_off, gc_off, u_off, conv_w, n_ctx)
    y = jnp.concatenate([o_rwkv, o_conv], axis=1)
    return _matmul(y, w_out.astype(BF16), name="even_w_out", tm=tm, tn=512, tk=D, rows=(mods[:, 2],), fulls=(x,),
                   epi=_gated_residual_epi(n_ctx))


def _rope_tables(n_ctx, n_lat):
    rows = n_lat // GRID_W
    row = jnp.repeat(jnp.arange(rows), GRID_W)
    col = jnp.tile(jnp.arange(GRID_W), rows)
    pos = jnp.stack([row, col], axis=-1).astype(F32)
    inv_freq = ROPE_BASE ** (-jnp.arange(ROPE_PAIRS, dtype=F32) / ROPE_PAIRS)
    ang = pos[:, :, None, None] * inv_freq
    shape = (n_lat, 2, 2, ROPE_PAIRS)
    cos = jnp.broadcast_to(jnp.cos(ang), shape).reshape(n_lat, QK_ROPE)
    sin = jnp.broadcast_to(jnp.sin(ang), shape).reshape(n_lat, QK_ROPE)
    cos = jnp.concatenate([jnp.ones((n_ctx, QK_ROPE), F32), cos], axis=0)
    sin = jnp.concatenate([jnp.zeros((n_ctx, QK_ROPE), F32), sin], axis=0)
    return jnp.concatenate([cos, sin], axis=1)


def _rot_cols(w):
    lead = w.shape[:-1]
    wr = w.reshape(*lead, 2, 2, ROPE_PAIRS)
    return jnp.stack([-wr[..., 1, :], wr[..., 0, :]], axis=-2).reshape(*lead, QK_ROPE)


def _odd_mixer(x, h, mods, n_ctx, w_in, w_out, q_norm, w_uq, kv_norm, w_ukv):
    T, D = x.shape
    W = D // 2
    n_lat = T - n_ctx
    kr_w = w_in[:, W + Q_LORA + KV_LORA:]
    w_in_p = jnp.concatenate([w_in, _rot_cols(kr_w)], axis=1).astype(BF16)
    w_in_p = _pad_cols(w_in_p, -(-w_in_p.shape[1] // 768) * 768)
    tm = _pick(T, (1408, 768, 256))
    z = _matmul(h, w_in_p, name="odd_w_in", tm=_pick(T, (704, 768, 256)), tn=768, tk=D)
    qa_off, kva_off, kr_off = W, W + Q_LORA, W + Q_LORA + KV_LORA

    qn = _rmsnorm_cols(z, qa_off, Q_LORA, q_norm)
    kvn = _rmsnorm_cols(z, kva_off, KV_LORA, kv_norm)
    uq = w_uq.reshape(Q_LORA, MLA_HEADS, QK_NOPE + QK_ROPE)
    uq_rope = uq[:, :, QK_NOPE:]
    w_uq_p = jnp.concatenate([uq[:, :, :QK_NOPE].reshape(Q_LORA, -1),
                              jnp.concatenate([uq_rope, _rot_cols(uq_rope)], axis=-1).reshape(Q_LORA, -1)], axis=1).astype(BF16)
    tml = _pick(n_lat, (1024, 512, 256))
    q_lat = _matmul(qn[n_ctx:], w_uq_p, name="mla_uq", tm=tml, tn=1024, tk=Q_LORA)
    kv = _matmul(kvn, w_ukv.astype(BF16), name="mla_ukv", tm=tm, tn=1024, tk=KV_LORA, out_dtype=BF16)

    tab = _rope_tables(n_ctx, n_lat)
    q_scale = SM_SCALE * math.log2(math.e)
    tab_lat = tab[n_ctx:]
    q_fin = _q_final(q_lat, tab_lat, q_scale)
    kr = _rope(z, kr_off, 1, tab, 1.0)
    att = _attention(q_fin, kv, kr, n_ctx)
    four = _fourier_mix(z, n_ctx, W)
    y = jnp.concatenate([four, att], axis=1)
    return _matmul(y, w_out.astype(BF16), name="odd_w_out", tm=tml, tn=512, tk=D, rows=(mods[:, 2],), fulls=(x[n_ctx:],),
                   epi=_gated_residual_epi(0))


def kernel(x, c, ctx, c_ctx, ada_w, ada_b, norm1_g, norm2_g, ev_w_in, ev_w_out, ev_w0, ev_w2, ev_a0, ev_a2, ev_g2, ev_k_k, ev_k_a, ev_r_k, ev_ln_x, ev_conv_w, od_w_in, od_w_out, od_q_norm, od_w_uq, od_kv_norm, od_w_ukv, router_w, router_b, moe_wg, moe_wu, moe_wd, shared_wg, shared_wu, shared_wd, final_g):
    B, n_lat, D = x.shape
    n_ctx = ctx.shape[1]
    depth = ada_w.shape[0]
    assert B == 1 and depth == 2 and n_ctx == ROW_BLOCK
    xs = jnp.concatenate([ctx[0], x[0]], axis=0)
    cond2 = jnp.concatenate([c_ctx[None], c], axis=0)
    router_w_p = _pad_cols(router_w.astype(F32), LANES)

    moe_w = (router_b, moe_wg, moe_wu, moe_wd, shared_wg, shared_wu, shared_wd)
    mods = _adaln(cond2, ada_w, ada_b, 0)
    mods_odd = _adaln(cond2, ada_w, ada_b, 1)
    h = _modulate(xs, norm1_g[0], mods, 0, 1, n_ctx)
    xs = _even_mixer(xs, h, mods, n_ctx, ev_w_in[0], ev_w_out[0], ev_w0[0], ev_w2[0], ev_a0[0], ev_a2[0], ev_g2[0],
                     ev_k_k[0], ev_k_a[0], ev_r_k[0], ev_ln_x[0], ev_conv_w[0])
    h, logits, h_packed = _modulate(xs, norm2_g[0], mods, 3, 4, n_ctx, router_w=router_w_p)
    xs = _moe_ffn(xs, h, logits, h_packed, *moe_w, 0, mods[:, 5], n_ctx)

    mods = mods_odd
    h = _modulate(xs, norm1_g[1], mods, 0, 1, n_ctx)
    xl = _odd_mixer(xs, h, mods, n_ctx, od_w_in[0], od_w_out[0], od_q_norm[0], od_w_uq[0], od_kv_norm[0], od_w_ukv[0])
    h, logits, h_packed = _modulate(xl, norm2_g[1], mods, 3, 4, 0, router_w=router_w_p)
    out = _moe_ffn(xl, h, logits, h_packed, *moe_w, 1, mods[:, 5], 0, final_g=final_g)
    return out[None]
```

```python
import functools
import math

import numpy as np
import jax
import jax.numpy as jnp
from jax import lax
from jax.experimental import pallas as pl
from jax.experimental.pallas import tpu as pltpu

F32 = jnp.float32
BF16 = jnp.bfloat16
HIGHEST = lax.Precision.HIGHEST

LANES = 128
SUBLANES = 8
VMEM_LIMIT_BYTES = 56 * 1024 * 1024

EPS = 1e-6
GN_EPS = 64e-5
RWKV_HEAD = 64
DECAY_LORA = 96
ICLR_LORA = 96
GATE_LORA = 256
CONV_K = 3
FOURIER_GROUP = 128
FFT_N2 = 128
MLA_HEADS = 16
QK_NOPE = 128
QK_ROPE = 64
V_HEAD = 128
Q_LORA = 1024
KV_LORA = 512
ROPE_PAIRS = QK_ROPE // 4
ROPE_BASE = 10000.0
GRID_W = 64
SM_SCALE = (QK_NOPE + QK_ROPE) ** -0.5
N_EXPERTS = 16
N_GROUPS = 4
EXPERTS_PER_GROUP = N_EXPERTS // N_GROUPS
TOP_K = 2
MOE_BLOCK = 512
MOE_DOWN_TN = 2048
ROW_BLOCK = 256
SCAN_CHUNK = 64
SCAN_PAIRS = 2


def _params(*sem):
    return pltpu.CompilerParams(dimension_semantics=sem, vmem_limit_bytes=VMEM_LIMIT_BYTES)


def _pack_bf16_halves(x):
    half = x.shape[1] // 2
    bits = lambda t: lax.bitcast_convert_type(t.astype(BF16).astype(F32), jnp.uint32)
    return (bits(x[:, half:]) & jnp.uint32(0xFFFF0000)) | (bits(x[:, :half]) >> 16)


def _unpack_bf16_halves(w):
    as_f32 = lambda bits: lax.bitcast_convert_type(bits, F32)
    return as_f32(w << 16), as_f32(w & jnp.uint32(0xFFFF0000))


def _pick(n, candidates):
    for c in candidates:
        if n % c == 0:
            return c
    raise ValueError(f"no tile for {n} among {candidates}")


def _mm_body(*refs, nk, a_act, epi, n_rows, n_fulls, tm):
    a_ref, b_ref = refs[0], refs[1]
    row_refs = refs[2:2 + n_rows]
    full_refs = refs[2 + n_rows:2 + n_rows + n_fulls]
    o_ref = refs[2 + n_rows + n_fulls]
    acc_ref = refs[3 + n_rows + n_fulls]
    k = pl.program_id(2)
    av = a_ref[...]
    if a_act is not None:
        av = a_act(av.astype(F32))
    part = jnp.dot(av.astype(BF16), b_ref[...].astype(BF16), preferred_element_type=F32)

    def finish(acc):
        if epi is not None:
            row0 = pl.program_id(0) * tm
            acc = epi(acc, row0, [r[...] for r in row_refs], [f[...] for f in full_refs])
        o_ref[...] = acc.astype(o_ref.dtype)

    if nk == 1:
        finish(part)
        return

    @pl.when(k == 0)
    def _():
        acc_ref[...] = part

    @pl.when(jnp.logical_and(k > 0, k < nk - 1))
    def _():
        acc_ref[...] += part

    @pl.when(k == nk - 1)
    def _():
        finish(acc_ref[...] + part)


def _matmul(a, b, *, name, tm, tn, tk, out_dtype=F32, out_tn=None, a_col_off=0, a_act=None, b_lead=None,
            rows=(), fulls=(), epi=None):
    K, N = b.shape[-2:]
    M = a.shape[0]
    out_tn = tn if out_tn is None else out_tn
    assert M % tm == 0 and N % tn == 0 and K % tk == 0 and a_col_off % tk == 0
    nk = K // tk
    ko = a_col_off // tk
    if b.ndim == 3:
        b_spec = pl.BlockSpec((None, tk, tn), lambda i, j, k: (b_lead, k, j))
    else:
        b_spec = pl.BlockSpec((tk, tn), lambda i, j, k: (k, j))
    in_specs = [pl.BlockSpec((tm, tk), lambda i, j, k: (i, k + ko)), b_spec]
    for r in rows:
        in_specs.append(pl.BlockSpec((r.shape[0], tn), lambda i, j, k: (0, j)))
    for f in fulls:
        in_specs.append(pl.BlockSpec((tm, f.shape[1] // (N // tn)), lambda i, j, k: (i, j)))
    body = functools.partial(_mm_body, nk=nk, a_act=a_act, epi=epi, n_rows=len(rows),
                             n_fulls=len(fulls), tm=tm)
    return pl.pallas_call(
        body,
        grid=(M // tm, N // tn, nk),
        in_specs=in_specs,
        out_specs=pl.BlockSpec((tm, out_tn), lambda i, j, k: (i, j)),
        out_shape=jax.ShapeDtypeStruct((M, N // tn * out_tn), out_dtype),
        scratch_shapes=[pltpu.VMEM((tm, tn) if nk > 1 else (SUBLANES, LANES), F32)],
        compiler_params=_params("parallel", "parallel", "arbitrary"),
        name=name,
    )(a, b, *rows, *fulls)


def _modulate_body(x_ref, g_ref, mod_ref, *rest, shift_idx, scale_idx, with_router):
    xv = x_ref[...]
    y = xv * lax.rsqrt(jnp.mean(xv * xv, axis=-1, keepdims=True) + EPS) * g_ref[...]
    h = y * (1.0 + mod_ref[0, scale_idx:scale_idx + 1, :]) + mod_ref[0, shift_idx:shift_idx + 1, :]
    if with_router:
        rw_ref, o_ref, lg_ref, pk_ref = rest
        lg_ref[...] = jnp.dot(h, rw_ref[...], precision=HIGHEST, preferred_element_type=F32)
        pk_ref[...] = _pack_bf16_halves(h)
    else:
        (o_ref,) = rest
    o_ref[...] = h.astype(o_ref.dtype)


def _modulate(x, g, mods, shift_idx, scale_idx, n_ctx, router_w=None):
    T, D = x.shape
    tm = ROW_BLOCK
    assert T % tm == 0 and n_ctx % tm == 0
    nc = n_ctx // tm
    in_specs = [
        pl.BlockSpec((tm, D), lambda i: (i, 0)),
        pl.BlockSpec((1, D), lambda i: (0, 0)),
        pl.BlockSpec((1, mods.shape[1], D), lambda i: (jnp.where(i < nc, 0, 1), 0, 0)),
    ]
    out_specs = [pl.BlockSpec((tm, D), lambda i: (i, 0))]
    out_shape = [jax.ShapeDtypeStruct((T, D), BF16)]
    args = [x, g.reshape(1, D), mods]
    if router_w is not None:
        in_specs.append(pl.BlockSpec((D, LANES), lambda i: (0, 0)))
        out_specs += [pl.BlockSpec((tm, LANES), lambda i: (i, 0)), pl.BlockSpec((tm, D // 2), lambda i: (i, 0))]
        out_shape += [jax.ShapeDtypeStruct((T, LANES), F32), jax.ShapeDtypeStruct((T, D // 2), jnp.uint32)]
        args.append(router_w)
    body = functools.partial(_modulate_body, shift_idx=shift_idx, scale_idx=scale_idx,
                             with_router=router_w is not None)
    outs = pl.pallas_call(body, grid=(T // tm,), in_specs=in_specs, out_specs=out_specs,
                          out_shape=out_shape, compiler_params=_params("parallel"), name="modulate")(*args)
    return outs if router_w is not None else outs[0]


def _rmsnorm_cols_body(x_ref, g_ref, o_ref):
    xv = x_ref[...]
    y = xv * lax.rsqrt(jnp.mean(xv * xv, axis=-1, keepdims=True) + EPS) * g_ref[...]
    o_ref[...] = y.astype(o_ref.dtype)


def _rmsnorm_cols(z, col_off, width, g, out_dtype=BF16):
    T = z.shape[0]
    tm = ROW_BLOCK
    assert col_off % width == 0 and T % tm == 0
    cb = col_off // width
    return pl.pallas_call(
        _rmsnorm_cols_body,
        name="rmsnorm_cols",
        grid=(T // tm,),
        in_specs=[pl.BlockSpec((tm, width), lambda i: (i, cb)), pl.BlockSpec((1, width), lambda i: (0, 0))],
        out_specs=pl.BlockSpec((tm, width), lambda i: (i, 0)),
        out_shape=jax.ShapeDtypeStruct((T, width), out_dtype),
        compiler_params=_params("parallel"),
    )(z, g.reshape(1, width))


def _conv_body(gb_ref, gc_ref, u_ref, gcp_ref, up_ref, gcn_ref, un_ref, w_ref, o_ref, *, tb, nb, nc):
    i = pl.program_id(0)
    p = gc_ref[...] * u_ref[...]
    prev_row = (gcp_ref[...] * up_ref[...])[SUBLANES - 1:SUBLANES, :]
    next_row = (gcn_ref[...] * un_ref[...])[0:1, :]
    starts = jnp.logical_or(i == 0, i == nc)
    ends = jnp.logical_or(i == nc - 1, i == nb - 1)
    prev_row = jnp.where(starts, 0.0, prev_row)
    next_row = jnp.where(ends, 0.0, next_row)
    rid = lax.broadcasted_iota(jnp.int32, p.shape, 0)
    xm1 = jnp.where(rid == 0, prev_row, pltpu.roll(p, 1, axis=0))
    xp1 = jnp.where(rid == tb - 1, next_row, pltpu.roll(p, tb - 1, axis=0))
    w = w_ref[...]
    o_ref[...] = (gb_ref[...] * (w[0:1] * xm1 + w[1:2] * p + w[2:3] * xp1)).astype(o_ref.dtype)


def _short_conv(z, gb_off, gc_off, u_off, conv_w, n_ctx):
    T = z.shape[0]
    C = conv_w.shape[1]
    tb, tn = ROW_BLOCK, 512
    nb, nc = T // tb, n_ctx // tb
    hb = tb // SUBLANES
    last_h = T // SUBLANES - 1
    cur = lambda off: pl.BlockSpec((tb, tn), lambda i, j: (i, off // tn + j))
    prv = lambda off: pl.BlockSpec((SUBLANES, tn), lambda i, j: (jnp.maximum(i * hb - 1, 0), off // tn + j))
    nxt = lambda off: pl.BlockSpec((SUBLANES, tn), lambda i, j: (jnp.minimum((i + 1) * hb, last_h), off // tn + j))
    return pl.pallas_call(
        functools.partial(_conv_body, tb=tb, nb=nb, nc=nc),
        name="short_conv",
        grid=(nb, C // tn),
        in_specs=[cur(gb_off), cur(gc_off), cur(u_off), prv(gc_off), prv(u_off), nxt(gc_off), nxt(u_off),
                  pl.BlockSpec((CONV_K, tn), lambda i, j: (0, j))],
        out_specs=pl.BlockSpec((tb, tn), lambda i, j: (i, j)),
        out_shape=jax.ShapeDtypeStruct((T, C), BF16),
        compiler_params=_params("parallel", "parallel"),
    )(z, z, z, z, z, z, z, conv_w)


NN = (((1,), (0,)), ((), ()))
NT = (((1,), (1,)), ((), ()))
TN = (((0,), (0,)), ((), ()))

SCAN_PIECES_GRAM = 1
SCAN_PIECES_INV = 1
SCAN_PIECES_OUT = 1


def _pieces(x, n):
    out = []
    for i in range(n):
        p = x.astype(BF16)
        out.append(p)
        if i + 1 < n:
            x = x - p.astype(F32)
    return out


def _pdot(ap, bp, dims=NN):
    order = max(len(ap), len(bp))
    acc = None
    for i, x in enumerate(ap):
        for j, y in enumerate(bp):
            if i + j < order:
                t = lax.dot_general(x, y, dims, preferred_element_type=F32)
                acc = t if acc is None else acc + t
    return acc


def _pcat(parts, axis):
    return [jnp.concatenate(ps, axis=axis) for ps in zip(*parts)]


def _scan_stages(C, masks):
    n = 2 * C
    reverse, in_h0, row_c, strict, incl, eye, diag_blocks, off_blocks = masks
    pg, pi, po = SCAN_PIECES_GRAM, SCAN_PIECES_INV, SCAN_PIECES_OUT

    def stack(x):
        return jnp.concatenate([jnp.where(in_h0, x, 0.0), jnp.where(in_h0, 0.0, x)], axis=0)

    def s_cum(d):
        for nm in ("lw", "r", "k", "v", "a", "b"):
            d[nm + "_s"] = stack(d[nm])
        x = d["lw"]
        sh = 1
        while sh < C:
            if reverse:
                x = x + jnp.where(row_c < C - sh, pltpu.roll(x, C - sh, axis=0), 0.0)
            else:
                x = x + jnp.where(row_c >= sh, pltpu.roll(x, sh, axis=0), 0.0)
            sh *= 2
        d["cum"] = stack(x)
        d["tot"] = x[0:1] if reverse else x[C - 1:C]

    def s_exp(d):
        cum, tot = d["cum"], d["tot"]
        inv, fin = jnp.exp(-cum), jnp.exp(tot - cum)
        d["r_hat"] = d["r_s"] * jnp.exp(cum)
        d["a_hat_p"] = _pieces(d["a_s"] * jnp.exp(cum - d["lw_s"]), max(pg, po))
        d["r_hat_p"] = _pieces(d["r_hat"], pg)
        d["bk_chk_p"] = _pcat([_pieces(d["b_s"] * inv, pg), _pieces(d["k_s"] * inv, pg)], 0)
        d["b_til_p"] = _pieces(d["b_s"] * fin, po)
        d["bk_til_p"] = _pcat([d["b_til_p"], _pieces(d["k_s"] * fin, po)], 0)
        d["v_p"] = _pieces(d["v_s"], po)
        d["decay"] = jnp.broadcast_to(jnp.exp(tot), (LANES, LANES)).T

    def s_gram(d):
        g = _pdot(_pcat([d["a_hat_p"][:pg], d["r_hat_p"]], 0), d["bk_chk_p"], NT)
        d["n_ab"] = jnp.where(strict, g[:n, :n], 0.0)
        d["m_ak_p"] = _pieces(jnp.where(strict, g[:n, n:], 0.0), po)
        d["m_rb_p"] = _pieces(jnp.where(incl, g[n:, :n], 0.0), po)
        d["m_rk_p"] = _pieces(jnp.where(incl, g[n:, n:], 0.0), po)
        nd = jnp.where(diag_blocks, d["n_ab"], 0.0)
        d["nd_p"] = _pieces(nd, pi)
        d["t"] = eye + nd

    def s_sq1(d):
        d["pw"] = _pdot(d["nd_p"], d["nd_p"])

    def s_ap1(d):
        d["pw_p"] = _pieces(d["pw"], pi)
        d["t"] = d["t"] + _pdot(d["pw_p"], _pieces(d["t"], pi))

    def s_sq2(d):
        d["pw_p"] = _pieces(_pdot(d["pw_p"], d["pw_p"]), pi)

    def s_ap2(d):
        d["t"] = d["t"] + _pdot(d["pw_p"], _pieces(d["t"], pi))

    def s_merge_a(off_mask):
        def f(d):
            d["t_p"] = _pieces(d["t"], pi)
            d["ot_p"] = _pieces(_pdot(_pieces(jnp.where(off_mask, d["n_ab"], 0.0), pi), d["t_p"]), pi)
        return f

    def s_merge_b(d):
        d["t"] = d["t"] + _pdot(d["t_p"], d["ot_p"])

    def s_abar(d):
        d["t_p"] = _pieces(d["t"], po)
        d["abar_p"] = _pieces(_pdot(d["t_p"], d["a_hat_p"][:po]), po)
        d["mv_p"] = _pieces(_pdot(d["m_ak_p"], d["v_p"]), po)

    def s_u0(d):
        d["uv_p"] = _pcat([_pieces(_pdot(d["t_p"], d["mv_p"]), po), d["v_p"]], 0)
        d["rbar_p"] = _pieces(d["r_hat"] + _pdot(d["m_rb_p"], d["abar_p"]), po)
        d["phi_p"] = _pieces(_pdot(d["b_til_p"], d["abar_p"], TN), po)

    def s_out(d):
        d["y0"] = _pdot(_pcat([d["m_rb_p"], d["m_rk_p"]], 1), d["uv_p"])
        d["s0"] = _pdot(d["bk_til_p"], d["uv_p"], TN)

    stages = [s_cum, s_exp, s_gram, s_sq1, s_ap1, s_sq2, s_ap2]
    for off_mask in off_blocks:
        stages += [s_merge_a(off_mask), s_merge_b]
    return stages + [s_abar, s_u0, s_out]


def _scan_apply(d, state):
    C = d["r"].shape[0]
    st_p = _pieces(state, SCAN_PIECES_OUT)
    y = _pdot(d["rbar_p"], st_p) + d["y0"]
    return y[:C] + y[C:], d["decay"] * state + _pdot(d["phi_p"], st_p) + d["s0"]


def _scan_masks(C, reverse):
    n = 2 * C
    lane = lax.broadcasted_iota(jnp.int32, (C, LANES), 1)
    row = lax.broadcasted_iota(jnp.int32, (n, n), 0)
    col = lax.broadcasted_iota(jnp.int32, (n, n), 1)
    same = (row // C) == (col // C)
    before = (col > row) if reverse else (col < row)
    strict = jnp.logical_and(same, before)
    incl = jnp.logical_and(same, jnp.logical_or(before, col == row))
    eye = jnp.where(row == col, 1.0, 0.0)
    row_c = lax.broadcasted_iota(jnp.int32, (C, LANES), 0)
    blk = 8
    diag_blocks = (row // blk) == (col // blk)
    off_blocks = []
    while blk < C:
        off_blocks.append(jnp.logical_and((row // (2 * blk)) == (col // (2 * blk)), (row // blk) != (col // blk)))
        blk *= 2
    return reverse, lane < RWKV_HEAD, row_c, strict, incl, eye, diag_blocks, off_blocks


def _scan_body(rf, kf, vf, lwf, icf, rb, kb, vb, lwb, icb, kk_ref, ka_ref, seg_ref, yf_ref, yb_ref, sf_ref, sb_ref, *, tb):
    @pl.when(pl.program_id(1) == 0)
    def _():
        sf_ref[...] = jnp.zeros_like(sf_ref)
        sb_ref[...] = jnp.zeros_like(sb_ref)

    seg = seg_ref[...].astype(BF16)
    nch = tb // SCAN_CHUNK
    scans = []
    for refs, y_ref, s_ref, reverse in ((rf, kf, vf, lwf, icf), yf_ref, sf_ref, False), ((rb, kb, vb, lwb, icb), yb_ref, sb_ref, True):
        stages = _scan_stages(SCAN_CHUNK, _scan_masks(SCAN_CHUNK, reverse))
        for p in range(SCAN_PAIRS):
            lanes = slice(p * LANES, (p + 1) * LANES)
            r, k, v, lw, ic = (t[:, lanes] for t in refs)
            k_k, k_a = kk_ref[:, lanes], ka_ref[:, lanes]
            kk = k * k_k
            kk = kk * lax.rsqrt(_pdot(_pieces(kk * kk, 3), [seg]) + 1e-12)
            a, b = -kk, kk * ic
            kd = k * (1.0 + (ic - 1.0) * k_a)
            chunks = []
            for c in (range(nch - 1, -1, -1) if reverse else range(nch)):
                sl = slice(c * SCAN_CHUNK, (c + 1) * SCAN_CHUNK)
                chunks.append(dict(r=r[sl], k=kd[sl], v=v[sl], a=a[sl], b=b[sl], lw=lw[sl], rows=sl))
            scans.append((chunks, stages, y_ref, s_ref, p, lanes))

    for step in range(len(scans[0][1])):
        for chunks, stages, *_ in scans:
            for d in chunks:
                stages[step](d)
    states = [s_ref[p] for _, _, _, s_ref, p, _ in scans]
    for c in range(nch):
        for i, (chunks, _, y_ref, _, _, lanes) in enumerate(scans):
            y, states[i] = _scan_apply(chunks[c], states[i])
            y_ref[chunks[c]["rows"], lanes] = y
    for (_, _, _, s_ref, p, _), state in zip(scans, states):
        s_ref[p] = state


def _rwkv_scan(z, r_off, k_off, v_off, lw_f, ic_f, lw_b, ic_b, k_k, k_a, n_ctx):
    T = z.shape[0]
    W = lw_f.shape[1]
    tb = ROW_BLOCK
    assert n_ctx == tb and T % tb == 0
    nb = T // tb
    tw = SCAN_PAIRS * LANES
    fwd = lambda i: i
    bwd = lambda i: jnp.where(i == 0, 0, nb - i)
    zspec = lambda off, o: pl.BlockSpec((tb, tw), lambda p, i: (o(i), off // tw + p))
    wspec = lambda o: pl.BlockSpec((tb, tw), lambda p, i: (o(i), p))
    par = pl.BlockSpec((1, tw), lambda p, i: (0, p))
    lane = np.arange(LANES)
    seg = jnp.asarray((lane[:, None] // RWKV_HEAD == lane[None, :] // RWKV_HEAD).astype(np.float32))
    return pl.pallas_call(
        functools.partial(_scan_body, tb=tb),
        name="rwkv_scan",
        grid=(W // tw, nb),
        in_specs=[zspec(r_off, fwd), zspec(k_off, fwd), zspec(v_off, fwd), wspec(fwd), wspec(fwd),
                  zspec(r_off, bwd), zspec(k_off, bwd), zspec(v_off, bwd), wspec(bwd), wspec(bwd),
                  par, par, pl.BlockSpec((LANES, LANES), lambda p, i: (0, 0))],
        out_specs=[wspec(fwd), wspec(bwd)],
        out_shape=[jax.ShapeDtypeStruct((T, W), F32)] * 2,
        scratch_shapes=[pltpu.VMEM((SCAN_PAIRS, LANES, LANES), F32)] * 2,
        compiler_params=_params("parallel", "arbitrary"),
    )(z, z, z, lw_f, ic_f, z, z, z, lw_b, ic_b, k_k.reshape(1, W), k_a.reshape(1, W), seg)


def _rwkv_post_body(yf, yb, r, k, v, icf, icb, gate, ka, rk, lnx, seg_ref, o_ref):
    seg = [seg_ref[...].astype(BF16)]
    head_sum = lambda t: _pdot(_pieces(t, 3), seg)
    inv_n = 1.0 / RWKV_HEAD
    wkv = yf[...] + yb[...]
    yc = wkv - head_sum(wkv) * inv_n
    yn = yc * lax.rsqrt(head_sum(yc * yc) * inv_n + GN_EPS) * lnx[...]
    kv, kav = k[...], ka[...]
    kd_sum = kv * (1.0 + (icf[...] - 1.0) * kav) + kv * (1.0 + (icb[...] - 1.0) * kav)
    bonus = head_sum(r[...] * kd_sum * rk[...]) * v[...]
    o_ref[...] = ((yn + bonus) * gate[...]).astype(o_ref.dtype)


def _rwkv_post(y_f, y_b, z, r_off, k_off, v_off, ic_f, ic_b, gate, k_a, r_k, ln_x):
    T, W = y_f.shape
    tb, tn = _pick(T, (768, 256)), 256
    blk = pl.BlockSpec((tb, tn), lambda i, j: (i, j))
    zspec = lambda off: pl.BlockSpec((tb, tn), lambda i, j: (i, off // tn + j))
    par = pl.BlockSpec((1, tn), lambda i, j: (0, j))
    lane = np.arange(tn)
    seg = jnp.asarray((lane[:, None] // RWKV_HEAD == lane[None, :] // RWKV_HEAD).astype(np.float32))
    return pl.pallas_call(
        _rwkv_post_body,
        name="rwkv_post",
        grid=(T // tb, W // tn),
        in_specs=[blk, blk, zspec(r_off), zspec(k_off), zspec(v_off), blk, blk, blk, par, par, par,
                  pl.BlockSpec((tn, tn), lambda i, j: (0, 0))],
        out_specs=blk,
        out_shape=jax.ShapeDtypeStruct((T, W), BF16),
        compiler_params=_params("parallel", "parallel"),
    )(y_f, y_b, z, z, z, ic_f, ic_b, gate, k_a.reshape(1, W), r_k.reshape(1, W), ln_x.reshape(1, W), seg)


def _dft_cols_body(u_ref, cs_ref, o_ref):
    G = FOURIER_GROUP
    for g in range(u_ref.shape[1] // G):
        pq = jnp.dot(u_ref[:, g * G:(g + 1) * G].astype(BF16), cs_ref[...], preferred_element_type=F32)
        o_ref[0, :, g * G:(g + 1) * G] = pq[:, :G].astype(o_ref.dtype)
        o_ref[1, :, g * G:(g + 1) * G] = pq[:, G:].astype(o_ref.dtype)


def _fft_stage1_body(z_ref, e_ref, o_ref):
    n2 = z_ref.shape[1]
    w = jnp.dot(e_ref[...], jnp.concatenate([z_ref[0], z_ref[1]], axis=0), preferred_element_type=F32)
    o_ref[0] = w[:n2].astype(o_ref.dtype)
    o_ref[1] = w[n2:].astype(o_ref.dtype)


def _fourier_mix(z, n_ctx, width):
    T = z.shape[0] - n_ctx
    tm = ROW_BLOCK
    ro = n_ctx // tm
    G = FOURIER_GROUP
    tn = _pick(width, (1024, 512, G))
    c = np.arange(G)
    ang_c = 2.0 * np.pi * ((c[:, None] * c[None, :]) % G) / G
    cs = jnp.asarray(np.concatenate([np.cos(ang_c), np.sin(ang_c)], axis=1), BF16)
    pq = pl.pallas_call(
        _dft_cols_body,
        name="dft_cols",
        grid=(T // tm, width // tn),
        in_specs=[pl.BlockSpec((tm, tn), lambda i, g: (i + ro, g)), pl.BlockSpec((G, 2 * G), lambda i, g: (0, 0))],
        out_specs=pl.BlockSpec((2, tm, tn), lambda i, g: (0, i, g)),
        out_shape=jax.ShapeDtypeStruct((2, T, width), BF16),
        compiler_params=_params("parallel", "parallel"),
    )(z, cs)
    n2 = FFT_N2
    n1 = T // n2
    zp = pq.reshape(2, n2, n1, width).transpose(0, 2, 1, 3)
    t = jnp.arange(n1, dtype=jnp.int32)[:, None, None] + n1 * jnp.arange(n2, dtype=jnp.int32)[None, None, :]
    ang = ((jnp.arange(n2, dtype=jnp.int32)[None, :, None] * t) % T).astype(F32) * (2.0 * math.pi / T)
    ec, es = jnp.cos(ang), jnp.sin(ang)
    e1 = jnp.concatenate([jnp.concatenate([ec, -es], axis=2), jnp.concatenate([-es, -ec], axis=2)], axis=1).astype(BF16)
    w = pl.pallas_call(
        _fft_stage1_body,
        name="fft_stage1",
        grid=(n1,),
        in_specs=[pl.BlockSpec((2, None, n2, width), lambda i: (0, i, 0, 0)),
                  pl.BlockSpec((None, 2 * n2, 2 * n2), lambda i: (i, 0, 0))],
        out_specs=pl.BlockSpec((2, None, n2, width), lambda i: (0, i, 0, 0)),
        out_shape=jax.ShapeDtypeStruct((2, n1, n2, width), BF16),
        compiler_params=_params("parallel"),
    )(zp, e1)
    a1 = np.arange(n1)
    ang1 = 2.0 * np.pi * ((a1[:, None] * a1[None, :]) % n1) / n1
    f1 = jnp.asarray(np.concatenate([np.cos(ang1), np.sin(ang1)], axis=1), BF16)
    scale = 1.0 / math.sqrt(T * G)
    out = _matmul(f1, w.reshape(2 * n1, n2 * width), name="fft_stage2", tm=n1, tn=_pick(n2 * width, (8192, 1024)),
                  tk=2 * n1, out_dtype=BF16, epi=lambda acc, row0, rows, fulls: acc * scale)
    return out.reshape(T, width)


def _rope_body(x_ref, tab_ref, o_ref, *, scale, keep_dup):
    xt = x_ref[...] * tab_ref[...]
    y = xt + pltpu.roll(xt, QK_ROPE, axis=1)
    if not keep_dup:
        lane = lax.broadcasted_iota(jnp.int32, y.shape, 1)
        y = jnp.where(lane < QK_ROPE, y, 0.0)
    o_ref[...] = (y * scale).astype(o_ref.dtype)


def _rope(z, col_off, n_heads, tab, scale):
    T = z.shape[0]
    tm = ROW_BLOCK
    cb = col_off // LANES
    return pl.pallas_call(
        functools.partial(_rope_body, scale=scale, keep_dup=False),
        name="rope",
        grid=(T // tm, n_heads),
        in_specs=[pl.BlockSpec((tm, LANES), lambda i, h: (i, cb + h)), pl.BlockSpec((tm, LANES), lambda i, h: (i, 0))],
        out_specs=pl.BlockSpec((tm, LANES), lambda i, h: (i, h)),
        out_shape=jax.ShapeDtypeStruct((T, n_heads * LANES), BF16),
        compiler_params=_params("parallel", "parallel"),
    )(z, tab)


def _q_final_body(qn_ref, qr_ref, tab_ref, o_ref, *, scale):
    xt = qr_ref[...] * tab_ref[...]
    y = xt + pltpu.roll(xt, QK_ROPE, axis=1)
    lane = lax.broadcasted_iota(jnp.int32, y.shape, 1)
    o_ref[:, :QK_NOPE] = (qn_ref[...] * scale).astype(o_ref.dtype)
    o_ref[:, QK_NOPE:] = (jnp.where(lane < QK_ROPE, y, 0.0) * scale).astype(o_ref.dtype)


def _q_final(q, tab, scale):
    T = q.shape[0]
    tm = _pick(T, (1024, 512, 256))
    blk = lambda off: pl.BlockSpec((tm, LANES), lambda i, h: (i, off + h))
    return pl.pallas_call(
        functools.partial(_q_final_body, scale=scale),
        name="q_final",
        grid=(T // tm, MLA_HEADS),
        in_specs=[blk(0), blk(MLA_HEADS), pl.BlockSpec((tm, LANES), lambda i, h: (i, 0))],
        out_specs=pl.BlockSpec((tm, 2 * LANES), lambda i, h: (i, h)),
        out_shape=jax.ShapeDtypeStruct((T, MLA_HEADS * 2 * LANES), BF16),
        compiler_params=_params("parallel", "parallel"),
    )(q, q, tab)


def _attn_body(q_ref, kn_ref, kr_ref, v_ref, o_ref, kc_ref, vc_ref, *, ts):
    @pl.when(pl.program_id(1) == 0)
    def _():
        kc_ref[:, :QK_NOPE] = kn_ref[...]
        kc_ref[:, QK_NOPE:] = kr_ref[...]
        vc_ref[:, :V_HEAD] = v_ref[...]
        vc_ref[:, V_HEAD:] = jnp.ones(v_ref.shape, BF16)

    q = q_ref[...]
    n_sub = kc_ref.shape[0] // ts

    def logits(c):
        return lax.dot_general(q, kc_ref[c * ts:(c + 1) * ts, :], NT, preferred_element_type=F32)

    def update(c, s, m_old, acc):
        m_new = jnp.maximum(m_old, jnp.max(s, axis=-1, keepdims=True))
        p = jnp.exp2(s - m_new).astype(BF16)
        pv = jnp.dot(p, vc_ref[c * ts:(c + 1) * ts, :], preferred_element_type=F32)
        return m_new, (pv if acc is None else jnp.exp2(m_old - m_new) * acc + pv)

    m = jnp.full((q.shape[0], 1), -jnp.inf, F32)
    acc = None
    s_prev = logits(0)
    for c in range(1, n_sub):
        s_next = logits(c)
        m, acc = update(c - 1, s_prev, m, acc)
        s_prev = s_next
    m, acc = update(n_sub - 1, s_prev, m, acc)
    o_ref[...] = (acc[:, :V_HEAD] / acc[:, V_HEAD:]).astype(o_ref.dtype)


def _attention(q, kv, kr, n_ctx):
    T = kv.shape[0]
    Tq = q.shape[0]
    tq = _pick(Tq, (1024, 512, 256))
    ts = _pick(T, (768, 256))
    kv_spec = lambda off: pl.BlockSpec((T, LANES), lambda h, i: (0, 2 * h + off))
    return pl.pallas_call(
        functools.partial(_attn_body, ts=ts),
        name="mla_attention",
        grid=(MLA_HEADS, Tq // tq),
        in_specs=[pl.BlockSpec((tq, 2 * LANES), lambda h, i: (i, h)), kv_spec(0),
                  pl.BlockSpec((T, LANES), lambda h, i: (0, 0)), kv_spec(1)],
        out_specs=pl.BlockSpec((tq, LANES), lambda h, i: (i, h)),
        out_shape=jax.ShapeDtypeStruct((Tq, MLA_HEADS * V_HEAD), BF16),
        scratch_shapes=[pltpu.VMEM((T, 2 * LANES), BF16), pltpu.VMEM((T, 2 * LANES), BF16)],
        compiler_params=_params("arbitrary", "arbitrary"),
    )(q, kv, kr, kv)


def _moe_up_body(be_ref, x_ref, wg_ref, wu_ref, o_ref, wgb_ref, wub_ref):
    b = pl.program_id(1)
    changed = jnp.logical_or(b == 0, be_ref[b] != be_ref[jnp.maximum(b - 1, 0)])

    @pl.when(changed)
    def _():
        wgb_ref[...] = wg_ref[...].astype(BF16)
        wub_ref[...] = wu_ref[...].astype(BF16)

    used = b < be_ref[pl.num_programs(1)]

    @pl.when(used)
    def _():
        half = x_ref.shape[1]
        x_lo, x_hi = (t.astype(BF16) for t in _unpack_bf16_halves(x_ref[...]))

        def proj(w_ref):
            return (jnp.dot(x_lo, w_ref[:half, :], preferred_element_type=F32)
                    + jnp.dot(x_hi, w_ref[half:, :], preferred_element_type=F32))

        gate, up = proj(wgb_ref), proj(wub_ref)
        o_ref[...] = (gate * jax.nn.sigmoid(gate) * up).astype(o_ref.dtype)

    @pl.when(jnp.logical_not(used))
    def _():
        o_ref[...] = jnp.zeros_like(o_ref)


def _moe_up(xb, block_e, wg, wu, layer):
    n_rows = xb.shape[0]
    D, DE = wg.shape[2:]
    tb, tn = MOE_BLOCK, 256
    w_spec = pl.BlockSpec((None, None, D, tn), lambda n, b, be: (layer, be[b], 0, n))
    return pl.pallas_call(
        _moe_up_body,
        name="moe_up",
        grid_spec=pltpu.PrefetchScalarGridSpec(
            num_scalar_prefetch=1,
            grid=(DE // tn, n_rows // tb),
            in_specs=[pl.BlockSpec((tb, D // 2), lambda n, b, be: (b, 0)), w_spec, w_spec],
            out_specs=pl.BlockSpec((tb, tn), lambda n, b, be: (b, n)),
            scratch_shapes=[pltpu.VMEM((D, tn), BF16), pltpu.VMEM((D, tn), BF16)]),
        out_shape=jax.ShapeDtypeStruct((n_rows, DE), BF16),
        compiler_params=_params("arbitrary", "arbitrary"),
    )(block_e, xb, wg, wu)


def _moe_down_body(be_ref, h_ref, wd_ref, sw_ref, o_ref, wdb_ref):
    b = pl.program_id(1)
    changed = jnp.logical_or(b == 0, be_ref[b] != be_ref[jnp.maximum(b - 1, 0)])

    @pl.when(changed)
    def _():
        wdb_ref[...] = wd_ref[...].astype(BF16)

    used = b < be_ref[pl.num_programs(1)]

    @pl.when(used)
    def _():
        y = jnp.dot(h_ref[...], wdb_ref[...], preferred_element_type=F32) * sw_ref[...]
        o_ref[...] = _pack_bf16_halves(y)

    @pl.when(jnp.logical_not(used))
    def _():
        o_ref[...] = jnp.zeros_like(o_ref)


def _moe_down(h, block_e, wd, slot_w, layer):
    n_rows, DE = h.shape
    D = wd.shape[3]
    tb, tn = MOE_BLOCK, MOE_DOWN_TN
    return pl.pallas_call(
        _moe_down_body,
        name="moe_down",
        grid_spec=pltpu.PrefetchScalarGridSpec(
            num_scalar_prefetch=1,
            grid=(D // tn, n_rows // tb),
            in_specs=[pl.BlockSpec((tb, DE), lambda n, b, be: (b, 0)),
                      pl.BlockSpec((None, None, DE, tn), lambda n, b, be: (layer, be[b], 0, n)),
                      pl.BlockSpec((tb, 1), lambda n, b, be: (b, 0))],
            out_specs=pl.BlockSpec((tb, tn // 2), lambda n, b, be: (b, n)),
            scratch_shapes=[pltpu.VMEM((DE, tn), BF16)]),
        out_shape=jax.ShapeDtypeStruct((n_rows, D // 2), jnp.uint32),
        compiler_params=_params("arbitrary", "arbitrary"),
    )(block_e, h, wd, slot_w.reshape(n_rows, 1))


def _shared_up_body(a_ref, wg_ref, wu_ref, o_ref, wgb_ref, wub_ref):
    @pl.when(pl.program_id(1) == 0)
    def _():
        wgb_ref[...] = wg_ref[...].astype(BF16)
        wub_ref[...] = wu_ref[...].astype(BF16)

    a = a_ref[...]
    gate = jnp.dot(a, wgb_ref[...], preferred_element_type=F32)
    up = jnp.dot(a, wub_ref[...], preferred_element_type=F32)
    o_ref[...] = (gate * jax.nn.sigmoid(gate) * up).astype(o_ref.dtype)


def _shared_up(h, sg, su, layer):
    T, D = h.shape
    DE = sg.shape[2]
    tm, tn = _pick(T, (1408, 1024, 768, 256)), 256
    w_spec = pl.BlockSpec((None, D, tn), lambda j, i: (layer, 0, j))
    return pl.pallas_call(
        _shared_up_body,
        name="shared_up",
        grid=(DE // tn, T // tm),
        in_specs=[pl.BlockSpec((tm, D), lambda j, i: (i, 0)), w_spec, w_spec],
        out_specs=pl.BlockSpec((tm, tn), lambda j, i: (i, j)),
        out_shape=jax.ShapeDtypeStruct((T, DE), BF16),
        scratch_shapes=[pltpu.VMEM((D, tn), BF16), pltpu.VMEM((D, tn), BF16)],
        compiler_params=_params("arbitrary", "arbitrary"),
    )(h, sg, su)


def _route(logits, router_b):
    T = logits.shape[0]
    s = jax.nn.sigmoid(logits)
    sel = (s + router_b.astype(F32)).reshape(T, N_GROUPS, EXPERTS_PER_GROUP)

    def top2(v):
        pos = jnp.arange(v.shape[-1])
        i0 = jnp.argmax(v, axis=-1)
        rest = jnp.where(pos == i0[..., None], -jnp.inf, v)
        return i0, jnp.argmax(rest, axis=-1), jnp.max(v, axis=-1), jnp.max(rest, axis=-1)

    _, _, v0, v1 = top2(sel)
    grp = jnp.argmax(v0 + v1, axis=-1)
    in_grp = jnp.arange(N_GROUPS)[None, :, None] == grp[:, None, None]
    loc0, loc1, _, _ = top2(jnp.sum(jnp.where(in_grp, sel, 0.0), axis=1))
    idx = grp[:, None] * EXPERTS_PER_GROUP + jnp.stack([loc0, loc1], axis=-1)
    picked = jnp.arange(N_EXPERTS)[None, None, :] == idx[:, :, None]
    wts = jnp.sum(jnp.where(picked, s[:, None, :], 0.0), axis=-1)
    return idx, wts / jnp.sum(wts, axis=-1, keepdims=True)


def _moe_ffn(x, h, logits, h_packed, router_b, wg, wu, wd, sg, su, sd, layer, gates, n_ctx, final_g=None):
    T, D = h.shape
    DE = sg.shape[2]
    hs = _shared_up(h, sg, su, layer)

    idx, wts = _route(logits[:, :N_EXPERTS], router_b)
    A = T * TOP_K
    flat_e = idx.reshape(A)
    onehot = (flat_e[:, None] == jnp.arange(N_EXPERTS)[None, :]).astype(jnp.int32)
    csum = jnp.cumsum(onehot, axis=0)
    counts = csum[-1]
    rank = jnp.sum(onehot * csum, axis=1) - 1
    padded = (counts + MOE_BLOCK - 1) // MOE_BLOCK * MOE_BLOCK
    pad_end = jnp.cumsum(padded)
    pad_start = pad_end - padded
    dest = (jnp.sum(onehot * pad_start[None, :], axis=1) + rank).astype(jnp.int32)
    n_blocks = -(-A // MOE_BLOCK) + N_EXPERTS
    n_slots = n_blocks * MOE_BLOCK
    slot_tok = (jnp.arange(n_slots, dtype=jnp.int32) % T).at[dest].set(jnp.arange(A, dtype=jnp.int32) // TOP_K)
    slot_w = jnp.zeros((n_slots,), F32).at[dest].set(wts.reshape(A))
    xb = h_packed[slot_tok]
    block_e = jnp.minimum(jnp.sum(pad_end[None, :] <= (jnp.arange(n_blocks) * MOE_BLOCK)[:, None], axis=1),
                          N_EXPERTS - 1).astype(jnp.int32)
    block_e = jnp.concatenate([block_e, (pad_end[-1:] // MOE_BLOCK).astype(jnp.int32)])
    y_slots = _moe_down(_moe_up(xb, block_e, wg, wu, layer), block_e, wd, slot_w, layer)
    slot_of = dest.reshape(T, TOP_K)
    routed = [y_slots[slot_of[:, k]] for k in range(TOP_K)]

    def unpack_tiles(w):
        lo, hi = _unpack_bf16_halves(w)
        tile = MOE_DOWN_TN // 2
        parts = [t[:, s:s + tile] for s in range(0, w.shape[1], tile) for t in (lo, hi)]
        return jnp.concatenate(parts, axis=1)

    def combine_epi(acc, row0, rows, fulls):
        rid = row0 + lax.broadcasted_iota(jnp.int32, acc.shape, 0)
        gate = jnp.where(rid < n_ctx, rows[0][0:1], rows[0][1:2])
        y = fulls[0] + gate * (unpack_tiles(fulls[1]) + unpack_tiles(fulls[2]) + acc)
        if final_g is not None:
            y = y * lax.rsqrt(jnp.mean(y * y, axis=-1, keepdims=True) + EPS) * rows[1]
        return y

    rows = (gates,) if final_g is None else (gates, final_g.reshape(1, D))
    return _matmul(hs, sd[layer].astype(BF16), name="shared_down_combine", tm=ROW_BLOCK, tn=D, tk=DE,
                   rows=rows, fulls=(x, *routed), epi=combine_epi)


def _silu(x):
    return x * jax.nn.sigmoid(x)


def _softplus(x):
    return jnp.maximum(x, 0.0) + jnp.log(1.0 + jnp.exp(-jnp.abs(x)))


def _adaln(cond2, w, b, layer):
    D = cond2.shape[1]
    a = jnp.zeros((SUBLANES, D), F32).at[:2].set(cond2)
    out = _matmul(a, w, name="adaln", tm=SUBLANES, tn=1024, tk=D, a_act=_silu, b_lead=layer,
                  rows=(b[layer].reshape(1, -1),), epi=lambda acc, row0, rows, fulls: acc + rows[0])
    return out[:2].reshape(2, -1, D)


def _gated_residual_epi(n_ctx):
    def epi(acc, row0, rows, fulls):
        rid = row0 + lax.broadcasted_iota(jnp.int32, acc.shape, 0)
        gate = jnp.where(rid < n_ctx, rows[0][0:1], rows[0][1:2])
        return fulls[0] + gate * acc
    return epi


def _pad_cols(w, width):
    return jnp.pad(w, ((0, 0), (0, width - w.shape[1])))


def _pad_rows(w, height):
    return jnp.pad(w, ((0, height - w.shape[0]), (0, 0)))


def _even_mixer(x, h, mods, n_ctx, w_in, w_out, w0, w2, a0, a2, g2, k_k, k_a, r_k, ln_x, conv_w):
    T, D = x.shape
    W = D // 2
    o = np.cumsum((0, W, W, W, DECAY_LORA, DECAY_LORA, ICLR_LORA, ICLR_LORA, GATE_LORA, W, W))
    lora = [_pad_cols(w_in[:, o[i]:o[i + 1]], LANES) for i in range(3, 7)]
    w_in_p = jnp.concatenate([w_in[:, :o[3]], w_in[:, o[8]:], *lora, w_in[:, o[7]:o[8]]], axis=1).astype(BF16)
    tm = _pick(T, (1408, 768, 256))
    z = _matmul(h, w_in_p, name="even_w_in", tm=_pick(T, (704, 768, 256)), tn=768, tk=D)
    r_off, k_off, v_off, gb_off, gc_off, u_off = (i * W for i in range(6))
    lo = 6 * W

    def lora_mm(col, kdim, w, bias, act, epi):
        rows = () if bias is None else (bias.reshape(1, W),)
        return _matmul(z, _pad_rows(w, kdim), name="rwkv_lora", tm=tm, tn=1024, tk=kdim, a_col_off=col, a_act=act, rows=rows, epi=epi)

    decay_epi = lambda acc, row0, rows, fulls: -jnp.exp(-_softplus(-(rows[0] + acc)) - 0.5)
    iclr_epi = lambda acc, row0, rows, fulls: jax.nn.sigmoid(rows[0] + acc)
    lw_f = lora_mm(lo, LANES, w2[0], w0[0], jnp.tanh, decay_epi)
    lw_b = lora_mm(lo + LANES, LANES, w2[1], w0[1], jnp.tanh, decay_epi)
    ic_f = lora_mm(lo + 2 * LANES, LANES, a2[0], a0[0], None, iclr_epi)
    ic_b = lora_mm(lo + 3 * LANES, LANES, a2[1], a0[1], None, iclr_epi)
    gate = lora_mm(lo + 4 * LANES, GATE_LORA, g2, None, jax.nn.sigmoid, None)

    y_f, y_b = _rwkv_scan(z, r_off, k_off, v_off, lw_f, ic_f, lw_b, ic_b, k_k, k_a, n_ctx)
    o_rwkv = _rwkv_post(y_f, y_b, z, r_off, k_off, v_off, ic_f, ic_b, gate, k_a, r_k, ln_x)
    o_conv = _short_conv(z, gb_off, gc_off, u_off, conv_w, n_ctx)
    y = jnp.concatenate([o_rwkv, o_conv], axis=1)
    return _matmul(y, w_out.astype(BF16), name="even_w_out", tm=tm, tn=512, tk=D, rows=(mods[:, 2],), fulls=(x,),
                   epi=_gated_residual_epi(n_ctx))


def _rope_tables(n_ctx, n_lat):
    rows = n_lat // GRID_W
    row = jnp.repeat(jnp.arange(rows), GRID_W)
    col = jnp.tile(jnp.arange(GRID_W), rows)
    pos = jnp.stack([row, col], axis=-1).astype(F32)
    inv_freq = ROPE_BASE ** (-jnp.arange(ROPE_PAIRS, dtype=F32) / ROPE_PAIRS)
    ang = pos[:, :, None, None] * inv_freq
    shape = (n_lat, 2, 2, ROPE_PAIRS)
    cos = jnp.broadcast_to(jnp.cos(ang), shape).reshape(n_lat, QK_ROPE)
    sin = jnp.broadcast_to(jnp.sin(ang), shape).reshape(n_lat, QK_ROPE)
    cos = jnp.concatenate([jnp.ones((n_ctx, QK_ROPE), F32), cos], axis=0)
    sin = jnp.concatenate([jnp.zeros((n_ctx, QK_ROPE), F32), sin], axis=0)
    return jnp.concatenate([cos, sin], axis=1)


def _rot_cols(w):
    lead = w.shape[:-1]
    wr = w.reshape(*lead, 2, 2, ROPE_PAIRS)
    return jnp.stack([-wr[..., 1, :], wr[..., 0, :]], axis=-2).reshape(*lead, QK_ROPE)


def _odd_mixer(x, h, mods, n_ctx, w_in, w_out, q_norm, w_uq, kv_norm, w_ukv):
    T, D = x.shape
    W = D // 2
    n_lat = T - n_ctx
    kr_w = w_in[:, W + Q_LORA + KV_LORA:]
    w_in_p = jnp.concatenate([w_in, _rot_cols(kr_w)], axis=1).astype(BF16)
    w_in_p = _pad_cols(w_in_p, -(-w_in_p.shape[1] // 768) * 768)
    tm = _pick(T, (1408, 768, 256))
    z = _matmul(h, w_in_p, name="odd_w_in", tm=_pick(T, (704, 768, 256)), tn=768, tk=D)
    qa_off, kva_off, kr_off = W, W + Q_LORA, W + Q_LORA + KV_LORA

    qn = _rmsnorm_cols(z, qa_off, Q_LORA, q_norm)
    kvn = _rmsnorm_cols(z, kva_off, KV_LORA, kv_norm)
    uq = w_uq.reshape(Q_LORA, MLA_HEADS, QK_NOPE + QK_ROPE)
    uq_rope = uq[:, :, QK_NOPE:]
    w_uq_p = jnp.concatenate([uq[:, :, :QK_NOPE].reshape(Q_LORA, -1),
                              jnp.concatenate([uq_rope, _rot_cols(uq_rope)], axis=-1).reshape(Q_LORA, -1)], axis=1).astype(BF16)
    tml = _pick(n_lat, (1024, 512, 256))
    q_lat = _matmul(qn[n_ctx:], w_uq_p, name="mla_uq", tm=tml, tn=1024, tk=Q_LORA)
    kv = _matmul(kvn, w_ukv.astype(BF16), name="mla_ukv", tm=tm, tn=1024, tk=KV_LORA, out_dtype=BF16)

    tab = _rope_tables(n_ctx, n_lat)
    q_scale = SM_SCALE * math.log2(math.e)
    tab_lat = tab[n_ctx:]
    q_fin = _q_final(q_lat, tab_lat, q_scale)
    kr = _rope(z, kr_off, 1, tab, 1.0)
    att = _attention(q_fin, kv, kr, n_ctx)
    four = _fourier_mix(z, n_ctx, W)
    y = jnp.concatenate([four, att], axis=1)
    return _matmul(y, w_out.astype(BF16), name="odd_w_out", tm=tml, tn=512, tk=D, rows=(mods[:, 2],), fulls=(x[n_ctx:],),
                   epi=_gated_residual_epi(0))


def kernel(x, c, ctx, c_ctx, ada_w, ada_b, norm1_g, norm2_g, ev_w_in, ev_w_out, ev_w0, ev_w2, ev_a0, ev_a2, ev_g2, ev_k_k, ev_k_a, ev_r_k, ev_ln_x, ev_conv_w, od_w_in, od_w_out, od_q_norm, od_w_uq, od_kv_norm, od_w_ukv, router_w, router_b, moe_wg, moe_wu, moe_wd, shared_wg, shared_wu, shared_wd, final_g):
    B, n_lat, D = x.shape
    n_ctx = ctx.shape[1]
    depth = ada_w.shape[0]
    assert B == 1 and depth == 2 and n_ctx == ROW_BLOCK
    xs = jnp.concatenate([ctx[0], x[0]], axis=0)
    cond2 = jnp.concatenate([c_ctx[None], c], axis=0)
    router_w_p = _pad_cols(router_w.astype(F32), LANES)

    moe_w = (router_b, moe_wg, moe_wu, moe_wd, shared_wg, shared_wu, shared_wd)
    mods = _adaln(cond2, ada_w, ada_b, 0)
    mods_odd = _adaln(cond2, ada_w, ada_b, 1)
    h = _modulate(xs, norm1_g[0], mods, 0, 1, n_ctx)
    xs = _even_mixer(xs, h, mods, n_ctx, ev_w_in[0], ev_w_out[0], ev_w0[0], ev_w2[0], ev_a0[0], ev_a2[0], ev_g2[0],
                     ev_k_k[0], ev_k_a[0], ev_r_k[0], ev_ln_x[0], ev_conv_w[0])
    h, logits, h_packed = _modulate(xs, norm2_g[0], mods, 3, 4, n_ctx, router_w=router_w_p)
    xs = _moe_ffn(xs, h, logits, h_packed, *moe_w, 0, mods[:, 5], n_ctx)

    mods = mods_odd
    h = _modulate(xs, norm1_g[1], mods, 0, 1, n_ctx)
    xl = _odd_mixer(xs, h, mods, n_ctx, od_w_in[0], od_w_out[0], od_q_norm[0], od_w_uq[0], od_kv_norm[0], od_w_ukv[0])
    h, logits, h_packed = _modulate(xl, norm2_g[1], mods, 3, 4, 0, router_w=router_w_p)
    out = _moe_ffn(xl, h, logits, h_packed, *moe_w, 1, mods[:, 5], 0, final_g=final_g)
    return out[None]
```

```python
import functools
import math

import numpy as np
import jax
import jax.numpy as jnp
from jax import lax
from jax.experimental import pallas as pl
from jax.experimental.pallas import tpu as pltpu

F32 = jnp.float32
BF16 = jnp.bfloat16

LANES = 128
SUBLANES = 8
VMEM_LIMIT_BYTES = 56 * 1024 * 1024

EPS = 1e-6
GN_EPS = 64e-5
RWKV_HEAD = 64
DECAY_LORA = 96
ICLR_LORA = 96
GATE_LORA = 256
CONV_K = 3
FOURIER_GROUP = 128
FFT_N2 = 128
MLA_HEADS = 16
QK_NOPE = 128
QK_ROPE = 64
V_HEAD = 128
Q_LORA = 1024
KV_LORA = 512
ROPE_PAIRS = QK_ROPE // 4
ROPE_BASE = 10000.0
GRID_W = 64
SM_SCALE = (QK_NOPE + QK_ROPE) ** -0.5
N_EXPERTS = 16
N_GROUPS = 4
EXPERTS_PER_GROUP = N_EXPERTS // N_GROUPS
TOP_K = 2
MOE_BLOCK = 512
MOE_UP_ROWS = 256
MOE_DOWN_TN = 2048
ROW_BLOCK = 256
SCAN_CHUNK = 64
SCAN_PAIRS = 2


def _params(*sem):
    return pltpu.CompilerParams(dimension_semantics=sem, vmem_limit_bytes=VMEM_LIMIT_BYTES)


def _pack_bf16_halves(x):
    half = x.shape[1] // 2
    bits = lambda t: lax.bitcast_convert_type(t.astype(BF16).astype(F32), jnp.uint32)
    return (bits(x[:, half:]) & jnp.uint32(0xFFFF0000)) | (bits(x[:, :half]) >> 16)


def _unpack_bf16_halves(w):
    as_f32 = lambda bits: lax.bitcast_convert_type(bits, F32)
    return as_f32(w << 16), as_f32(w & jnp.uint32(0xFFFF0000))


def _pick(n, candidates):
    for c in candidates:
        if n % c == 0:
            return c
    raise ValueError(f"no tile for {n} among {candidates}")


def _mm_body(*refs, nk, a_act, epi, n_rows, n_fulls, tm, two_a):
    a_ref, b_ref = refs[0], refs[1]
    row_refs = refs[2:2 + n_rows]
    full_refs = refs[2 + n_rows:2 + n_rows + n_fulls]
    n_in = 2 + n_rows + n_fulls + int(two_a)
    o_ref, acc_ref = refs[n_in], refs[n_in + 1]
    k = pl.program_id(2)
    av = a_ref[...]
    if a_act is not None:
        av = a_act(av.astype(F32))
    if two_a:
        a2_ref = refs[n_in - 1]
        k1 = a_ref.shape[1]
        part = (jnp.dot(av.astype(BF16), b_ref[:k1, :].astype(BF16), preferred_element_type=F32)
                + jnp.dot(a2_ref[...].astype(BF16), b_ref[k1:, :].astype(BF16), preferred_element_type=F32))
    else:
        part = jnp.dot(av.astype(BF16), b_ref[...].astype(BF16), preferred_element_type=F32)

    def finish(acc):
        if epi is not None:
            row0 = pl.program_id(0) * tm
            acc = epi(acc, row0, [r[...] for r in row_refs], [f[...] for f in full_refs])
        o_ref[...] = acc.astype(o_ref.dtype)

    if nk == 1:
        finish(part)
        return

    @pl.when(k == 0)
    def _():
        acc_ref[...] = part

    @pl.when(jnp.logical_and(k > 0, k < nk - 1))
    def _():
        acc_ref[...] += part

    @pl.when(k == nk - 1)
    def _():
        finish(acc_ref[...] + part)


def _matmul(a, b, *, name, tm, tn, tk, out_dtype=F32, a_col_off=0, a_act=None, a2=None, b_lead=None,
            rows=(), fulls=(), epi=None):
    K, N = b.shape[-2:]
    M = a.shape[0]
    assert M % tm == 0 and N % tn == 0 and K % tk == 0 and a_col_off % tk == 0
    nk = K // tk
    ko = a_col_off // tk
    if b.ndim == 3:
        b_spec = pl.BlockSpec((None, tk, tn), lambda i, j, k: (b_lead, k, j))
    else:
        b_spec = pl.BlockSpec((tk, tn), lambda i, j, k: (k, j))
    if a2 is None:
        a_specs = [pl.BlockSpec((tm, tk), lambda i, j, k: (i, k + ko))]
    else:
        assert nk == 1 and a.shape[1] + a2.shape[1] == K
        a_specs = [pl.BlockSpec((tm, a.shape[1]), lambda i, j, k: (i, 0))]
    in_specs = a_specs + [b_spec]
    for r in rows:
        in_specs.append(pl.BlockSpec((r.shape[0], tn), lambda i, j, k: (0, j)))
    for f in fulls:
        in_specs.append(pl.BlockSpec((tm, f.shape[1] // (N // tn)), lambda i, j, k: (i, j)))
    extra = ()
    if a2 is not None:
        in_specs.append(pl.BlockSpec((tm, a2.shape[1]), lambda i, j, k: (i, 0)))
        extra = (a2,)
    body = functools.partial(_mm_body, nk=nk, a_act=a_act, epi=epi, n_rows=len(rows),
                             n_fulls=len(fulls), tm=tm, two_a=a2 is not None)
    return pl.pallas_call(
        body,
        grid=(M // tm, N // tn, nk),
        in_specs=in_specs,
        out_specs=pl.BlockSpec((tm, tn), lambda i, j, k: (i, j)),
        out_shape=jax.ShapeDtypeStruct((M, N), out_dtype),
        scratch_shapes=[pltpu.VMEM((tm, tn) if nk > 1 else (SUBLANES, LANES), F32)],
        compiler_params=_params("parallel", "parallel", "arbitrary"),
        name=name,
    )(a, b, *rows, *fulls, *extra)


def _modulate_body(x_ref, g_ref, mod_ref, *rest, shift_idx, scale_idx, with_router):
    xv = x_ref[...]
    y = xv * lax.rsqrt(jnp.mean(xv * xv, axis=-1, keepdims=True) + EPS) * g_ref[...]
    h = y * (1.0 + mod_ref[0, scale_idx:scale_idx + 1, :]) + mod_ref[0, shift_idx:shift_idx + 1, :]
    if with_router:
        rw_ref, o_ref, lg_ref, pk_ref = rest
        lg_ref[...] = _pdot(_pieces(h, 2), _pieces(rw_ref[...], 2))
        pk_ref[...] = _pack_bf16_halves(h)
    else:
        (o_ref,) = rest
    o_ref[...] = h.astype(o_ref.dtype)


def _modulate(x, g, mods, shift_idx, scale_idx, n_ctx, router_w=None):
    T, D = x.shape
    tm = ROW_BLOCK
    assert T % tm == 0 and n_ctx % tm == 0
    nc = n_ctx // tm
    in_specs = [
        pl.BlockSpec((tm, D), lambda i: (i, 0)),
        pl.BlockSpec((1, D), lambda i: (0, 0)),
        pl.BlockSpec((1, mods.shape[1], D), lambda i: (jnp.where(i < nc, 0, 1), 0, 0)),
    ]
    out_specs = [pl.BlockSpec((tm, D), lambda i: (i, 0))]
    out_shape = [jax.ShapeDtypeStruct((T, D), BF16)]
    args = [x, g.reshape(1, D), mods]
    if router_w is not None:
        in_specs.append(pl.BlockSpec((D, LANES), lambda i: (0, 0)))
        out_specs += [pl.BlockSpec((tm, LANES), lambda i: (i, 0)), pl.BlockSpec((tm, D // 2), lambda i: (i, 0))]
        out_shape += [jax.ShapeDtypeStruct((T, LANES), F32), jax.ShapeDtypeStruct((T, D // 2), jnp.uint32)]
        args.append(router_w)
    body = functools.partial(_modulate_body, shift_idx=shift_idx, scale_idx=scale_idx,
                             with_router=router_w is not None)
    outs = pl.pallas_call(body, grid=(T // tm,), in_specs=in_specs, out_specs=out_specs,
                          out_shape=out_shape, compiler_params=_params("parallel"), name="modulate")(*args)
    return outs if router_w is not None else outs[0]


def _rmsnorm_cols_body(x_ref, g_ref, o_ref):
    xv = x_ref[...]
    y = xv * lax.rsqrt(jnp.mean(xv * xv, axis=-1, keepdims=True) + EPS) * g_ref[...]
    o_ref[...] = y.astype(o_ref.dtype)


def _rmsnorm_cols(z, col_off, width, g, out_dtype=BF16):
    T = z.shape[0]
    tm = ROW_BLOCK
    assert col_off % width == 0 and T % tm == 0
    cb = col_off // width
    return pl.pallas_call(
        _rmsnorm_cols_body,
        name="rmsnorm_cols",
        grid=(T // tm,),
        in_specs=[pl.BlockSpec((tm, width), lambda i: (i, cb)), pl.BlockSpec((1, width), lambda i: (0, 0))],
        out_specs=pl.BlockSpec((tm, width), lambda i: (i, 0)),
        out_shape=jax.ShapeDtypeStruct((T, width), out_dtype),
        compiler_params=_params("parallel"),
    )(z, g.reshape(1, width))


def _conv_body(gb_ref, gc_ref, u_ref, gcp_ref, up_ref, gcn_ref, un_ref, w_ref, o_ref, *, tb, nb, nc):
    i = pl.program_id(0)
    p = gc_ref[...] * u_ref[...]
    prev_row = (gcp_ref[...] * up_ref[...])[SUBLANES - 1:SUBLANES, :]
    next_row = (gcn_ref[...] * un_ref[...])[0:1, :]
    starts = jnp.logical_or(i == 0, i == nc)
    ends = jnp.logical_or(i == nc - 1, i == nb - 1)
    prev_row = jnp.where(starts, 0.0, prev_row)
    next_row = jnp.where(ends, 0.0, next_row)
    rid = lax.broadcasted_iota(jnp.int32, p.shape, 0)
    xm1 = jnp.where(rid == 0, prev_row, pltpu.roll(p, 1, axis=0))
    xp1 = jnp.where(rid == tb - 1, next_row, pltpu.roll(p, tb - 1, axis=0))
    w = w_ref[...]
    o_ref[...] = (gb_ref[...] * (w[0:1] * xm1 + w[1:2] * p + w[2:3] * xp1)).astype(o_ref.dtype)


def _short_conv(z, gb_off, gc_off, u_off, conv_w, n_ctx):
    T = z.shape[0]
    C = conv_w.shape[1]
    tb, tn = ROW_BLOCK, 512
    nb, nc = T // tb, n_ctx // tb
    hb = tb // SUBLANES
    last_h = T // SUBLANES - 1
    cur = lambda off: pl.BlockSpec((tb, tn), lambda i, j: (i, off // tn + j))
    prv = lambda off: pl.BlockSpec((SUBLANES, tn), lambda i, j: (jnp.maximum(i * hb - 1, 0), off // tn + j))
    nxt = lambda off: pl.BlockSpec((SUBLANES, tn), lambda i, j: (jnp.minimum((i + 1) * hb, last_h), off // tn + j))
    return pl.pallas_call(
        functools.partial(_conv_body, tb=tb, nb=nb, nc=nc),
        name="short_conv",
        grid=(nb, C // tn),
        in_specs=[cur(gb_off), cur(gc_off), cur(u_off), prv(gc_off), prv(u_off), nxt(gc_off), nxt(u_off),
                  pl.BlockSpec((CONV_K, tn), lambda i, j: (0, j))],
        out_specs=pl.BlockSpec((tb, tn), lambda i, j: (i, j)),
        out_shape=jax.ShapeDtypeStruct((T, C), BF16),
        compiler_params=_params("parallel", "parallel"),
    )(z, z, z, z, z, z, z, conv_w)


NN = (((1,), (0,)), ((), ()))
NT = (((1,), (1,)), ((), ()))
TN = (((0,), (0,)), ((), ()))

SCAN_PIECES_GRAM = 1
SCAN_PIECES_INV = 1
SCAN_PIECES_OUT = 1


def _pieces(x, n):
    out = []
    for i in range(n):
        p = x.astype(BF16)
        out.append(p)
        if i + 1 < n:
            x = x - p.astype(F32)
    return out


def _pdot(ap, bp, dims=NN):
    order = max(len(ap), len(bp))
    acc = None
    for i, x in enumerate(ap):
        for j, y in enumerate(bp):
            if i + j < order:
                t = lax.dot_general(x, y, dims, preferred_element_type=F32)
                acc = t if acc is None else acc + t
    return acc


def _pcat(parts, axis):
    return [jnp.concatenate(ps, axis=axis) for ps in zip(*parts)]


def _scan_stages(C, masks):
    n = 2 * C
    reverse, in_h0, row_c, strict, incl, eye, diag_blocks, off_blocks = masks
    pg, pi, po = SCAN_PIECES_GRAM, SCAN_PIECES_INV, SCAN_PIECES_OUT

    def stack(x):
        return jnp.concatenate([jnp.where(in_h0, x, 0.0), jnp.where(in_h0, 0.0, x)], axis=0)

    def s_cum(d):
        for nm in ("lw", "r", "k", "v", "a", "b"):
            d[nm + "_s"] = stack(d[nm])
        x = d["lw"]
        sh = 1
        while sh < C:
            if reverse:
                x = x + jnp.where(row_c < C - sh, pltpu.roll(x, C - sh, axis=0), 0.0)
            else:
                x = x + jnp.where(row_c >= sh, pltpu.roll(x, sh, axis=0), 0.0)
            sh *= 2
        d["cum"] = stack(x)
        d["tot"] = x[0:1] if reverse else x[C - 1:C]

    def s_exp(d):
        cum, tot = d["cum"], d["tot"]
        inv, fin = jnp.exp(-cum), jnp.exp(tot - cum)
        d["r_hat"] = d["r_s"] * jnp.exp(cum)
        d["a_hat_p"] = _pieces(d["a_s"] * jnp.exp(cum - d["lw_s"]), max(pg, po))
        d["r_hat_p"] = _pieces(d["r_hat"], pg)
        d["bk_chk_p"] = _pcat([_pieces(d["b_s"] * inv, pg), _pieces(d["k_s"] * inv, pg)], 0)
        d["b_til_p"] = _pieces(d["b_s"] * fin, po)
        d["bk_til_p"] = _pcat([d["b_til_p"], _pieces(d["k_s"] * fin, po)], 0)
        d["v_p"] = _pieces(d["v_s"], po)
        d["decay"] = jnp.broadcast_to(jnp.exp(tot), (LANES, LANES)).T

    def s_gram(d):
        g = _pdot(_pcat([d["a_hat_p"][:pg], d["r_hat_p"]], 0), d["bk_chk_p"], NT)
        d["n_ab"] = jnp.where(strict, g[:n, :n], 0.0)
        d["m_ak_p"] = _pieces(jnp.where(strict, g[:n, n:], 0.0), po)
        d["m_rb_p"] = _pieces(jnp.where(incl, g[n:, :n], 0.0), po)
        d["m_rk_p"] = _pieces(jnp.where(incl, g[n:, n:], 0.0), po)
        nd = jnp.where(diag_blocks, d["n_ab"], 0.0)
        d["nd_p"] = _pieces(nd, pi)
        d["t"] = eye + nd

    def s_sq1(d):
        d["pw"] = _pdot(d["nd_p"], d["nd_p"])

    def s_ap1(d):
        d["pw_p"] = _pieces(d["pw"], pi)
        d["t"] = d["t"] + _pdot(d["pw_p"], _pieces(d["t"], pi))

    def s_sq2(d):
        d["pw_p"] = _pieces(_pdot(d["pw_p"], d["pw_p"]), pi)

    def s_ap2(d):
        d["t"] = d["t"] + _pdot(d["pw_p"], _pieces(d["t"], pi))

    def s_merge_a(off_mask):
        def f(d):
            d["t_p"] = _pieces(d["t"], pi)
            d["ot_p"] = _pieces(_pdot(_pieces(jnp.where(off_mask, d["n_ab"], 0.0), pi), d["t_p"]), pi)
        return f

    def s_merge_b(d):
        d["t"] = d["t"] + _pdot(d["t_p"], d["ot_p"])

    def s_abar(d):
        d["t_p"] = _pieces(d["t"], po)
        d["abar_p"] = _pieces(_pdot(d["t_p"], d["a_hat_p"][:po]), po)
        d["mv_p"] = _pieces(_pdot(d["m_ak_p"], d["v_p"]), po)

    def s_u0(d):
        d["uv_p"] = _pcat([_pieces(_pdot(d["t_p"], d["mv_p"]), po), d["v_p"]], 0)
        d["rbar_p"] = _pieces(d["r_hat"] + _pdot(d["m_rb_p"], d["abar_p"]), po)
        d["phi_p"] = _pieces(_pdot(d["b_til_p"], d["abar_p"], TN), po)

    def s_out(d):
        d["y0"] = _pdot(_pcat([d["m_rb_p"], d["m_rk_p"]], 1), d["uv_p"])
        d["s0"] = _pdot(d["bk_til_p"], d["uv_p"], TN)

    stages = [s_cum, s_exp, s_gram, s_sq1, s_ap1, s_sq2, s_ap2]
    for off_mask in off_blocks:
        stages += [s_merge_a(off_mask), s_merge_b]
    return stages + [s_abar, s_u0, s_out]


def _scan_apply(d, state):
    C = d["r"].shape[0]
    st_p = _pieces(state, SCAN_PIECES_OUT)
    y = _pdot(d["rbar_p"], st_p) + d["y0"]
    return y[:C] + y[C:], d["decay"] * state + _pdot(d["phi_p"], st_p) + d["s0"]


def _scan_masks(C, reverse):
    n = 2 * C
    lane = lax.broadcasted_iota(jnp.int32, (C, LANES), 1)
    row = lax.broadcasted_iota(jnp.int32, (n, n), 0)
    col = lax.broadcasted_iota(jnp.int32, (n, n), 1)
    same = (row // C) == (col // C)
    before = (col > row) if reverse else (col < row)
    strict = jnp.logical_and(same, before)
    incl = jnp.logical_and(same, jnp.logical_or(before, col == row))
    eye = jnp.where(row == col, 1.0, 0.0)
    row_c = lax.broadcasted_iota(jnp.int32, (C, LANES), 0)
    blk = 8
    diag_blocks = (row // blk) == (col // blk)
    off_blocks = []
    while blk < C:
        off_blocks.append(jnp.logical_and((row // (2 * blk)) == (col // (2 * blk)), (row // blk) != (col // blk)))
        blk *= 2
    return reverse, lane < RWKV_HEAD, row_c, strict, incl, eye, diag_blocks, off_blocks


def _scan_body(rf, kf, vf, lwf, icf, rb, kb, vb, lwb, icb, kk_ref, ka_ref, seg_ref, yf_ref, yb_ref, sf_ref, sb_ref, *, tb):
    @pl.when(pl.program_id(1) == 0)
    def _():
        sf_ref[...] = jnp.zeros_like(sf_ref)
        sb_ref[...] = jnp.zeros_like(sb_ref)

    seg = seg_ref[...].astype(BF16)
    nch = tb // SCAN_CHUNK
    scans = []
    for refs, y_ref, s_ref, reverse in ((rf, kf, vf, lwf, icf), yf_ref, sf_ref, False), ((rb, kb, vb, lwb, icb), yb_ref, sb_ref, True):
        stages = _scan_stages(SCAN_CHUNK, _scan_masks(SCAN_CHUNK, reverse))
        for p in range(SCAN_PAIRS):
            lanes = slice(p * LANES, (p + 1) * LANES)
            r, k, v, lw, ic = (t[:, lanes] for t in refs)
            k_k, k_a = kk_ref[:, lanes], ka_ref[:, lanes]
            kk = k * k_k
            kk = kk * lax.rsqrt(_pdot(_pieces(kk * kk, 3), [seg]) + 1e-12)
            a, b = -kk, kk * ic
            kd = k * (1.0 + (ic - 1.0) * k_a)
            chunks = []
            for c in (range(nch - 1, -1, -1) if reverse else range(nch)):
                sl = slice(c * SCAN_CHUNK, (c + 1) * SCAN_CHUNK)
                chunks.append(dict(r=r[sl], k=kd[sl], v=v[sl], a=a[sl], b=b[sl], lw=lw[sl], rows=sl))
            scans.append((chunks, stages, y_ref, s_ref, p, lanes))

    for step in range(len(scans[0][1])):
        for chunks, stages, *_ in scans:
            for d in chunks:
                stages[step](d)
    states = [s_ref[p] for _, _, _, s_ref, p, _ in scans]
    for c in range(nch):
        for i, (chunks, _, y_ref, _, _, lanes) in enumerate(scans):
            y, states[i] = _scan_apply(chunks[c], states[i])
            y_ref[chunks[c]["rows"], lanes] = y
    for (_, _, _, s_ref, p, _), state in zip(scans, states):
        s_ref[p] = state


def _rwkv_scan(z, r_off, k_off, v_off, lw_f, ic_f, lw_b, ic_b, k_k, k_a, n_ctx):
    T = z.shape[0]
    W = lw_f.shape[1]
    tb = ROW_BLOCK
    assert n_ctx == tb and T % tb == 0
    nb = T // tb
    tw = SCAN_PAIRS * LANES
    fwd = lambda i: i
    bwd = lambda i: jnp.where(i == 0, 0, nb - i)
    zspec = lambda off, o: pl.BlockSpec((tb, tw), lambda p, i: (o(i), off // tw + p))
    wspec = lambda o: pl.BlockSpec((tb, tw), lambda p, i: (o(i), p))
    par = pl.BlockSpec((1, tw), lambda p, i: (0, p))
    lane = np.arange(LANES)
    seg = jnp.asarray((lane[:, None] // RWKV_HEAD == lane[None, :] // RWKV_HEAD).astype(np.float32))
    return pl.pallas_call(
        functools.partial(_scan_body, tb=tb),
        name="rwkv_scan",
        grid=(W // tw, nb),
        in_specs=[zspec(r_off, fwd), zspec(k_off, fwd), zspec(v_off, fwd), wspec(fwd), wspec(fwd),
                  zspec(r_off, bwd), zspec(k_off, bwd), zspec(v_off, bwd), wspec(bwd), wspec(bwd),
                  par, par, pl.BlockSpec((LANES, LANES), lambda p, i: (0, 0))],
        out_specs=[wspec(fwd), wspec(bwd)],
        out_shape=[jax.ShapeDtypeStruct((T, W), F32)] * 2,
        scratch_shapes=[pltpu.VMEM((SCAN_PAIRS, LANES, LANES), F32)] * 2,
        compiler_params=_params("parallel", "arbitrary"),
    )(z, z, z, lw_f, ic_f, z, z, z, lw_b, ic_b, k_k.reshape(1, W), k_a.reshape(1, W), seg)


def _rwkv_post_body(yf, yb, r, k, v, icf, icb, gate, ka, rk, lnx, seg_ref, o_ref):
    seg = [seg_ref[...].astype(BF16)]
    head_sum = lambda t: _pdot(_pieces(t, 3), seg)
    inv_n = 1.0 / RWKV_HEAD
    wkv = yf[...] + yb[...]
    yc = wkv - head_sum(wkv) * inv_n
    yn = yc * lax.rsqrt(head_sum(yc * yc) * inv_n + GN_EPS) * lnx[...]
    kv, kav = k[...], ka[...]
    kd_sum = kv * (1.0 + (icf[...] - 1.0) * kav) + kv * (1.0 + (icb[...] - 1.0) * kav)
    bonus = head_sum(r[...] * kd_sum * rk[...]) * v[...]
    o_ref[...] = ((yn + bonus) * gate[...]).astype(o_ref.dtype)


def _rwkv_post(y_f, y_b, z, r_off, k_off, v_off, ic_f, ic_b, gate, k_a, r_k, ln_x):
    T, W = y_f.shape
    tb, tn = _pick(T, (768, 256)), 256
    blk = pl.BlockSpec((tb, tn), lambda i, j: (i, j))
    zspec = lambda off: pl.BlockSpec((tb, tn), lambda i, j: (i, off // tn + j))
    par = pl.BlockSpec((1, tn), lambda i, j: (0, j))
    lane = np.arange(tn)
    seg = jnp.asarray((lane[:, None] // RWKV_HEAD == lane[None, :] // RWKV_HEAD).astype(np.float32))
    return pl.pallas_call(
        _rwkv_post_body,
        name="rwkv_post",
        grid=(T // tb, W // tn),
        in_specs=[blk, blk, zspec(r_off), zspec(k_off), zspec(v_off), blk, blk, blk, par, par, par,
                  pl.BlockSpec((tn, tn), lambda i, j: (0, 0))],
        out_specs=blk,
        out_shape=jax.ShapeDtypeStruct((T, W), BF16),
        compiler_params=_params("parallel", "parallel"),
    )(y_f, y_b, z, z, z, ic_f, ic_b, gate, k_a.reshape(1, W), r_k.reshape(1, W), ln_x.reshape(1, W), seg)


def _dft_cols_body(u_ref, cs_ref, o_ref):
    G = FOURIER_GROUP
    for g in range(u_ref.shape[1] // G):
        pq = jnp.dot(u_ref[:, g * G:(g + 1) * G].astype(BF16), cs_ref[...], preferred_element_type=F32)
        o_ref[0, :, g * G:(g + 1) * G] = pq[:, :G].astype(o_ref.dtype)
        o_ref[1, :, g * G:(g + 1) * G] = pq[:, G:].astype(o_ref.dtype)


def _fft_stage1_body(z_ref, e_ref, o_ref):
    n2 = z_ref.shape[1]
    w = jnp.dot(e_ref[...], jnp.concatenate([z_ref[0], z_ref[1]], axis=0), preferred_element_type=F32)
    o_ref[0] = w[:n2].astype(o_ref.dtype)
    o_ref[1] = w[n2:].astype(o_ref.dtype)


def _fourier_mix(z, n_ctx, width):
    T = z.shape[0] - n_ctx
    tm = ROW_BLOCK
    ro = n_ctx // tm
    G = FOURIER_GROUP
    tn = _pick(width, (1024, 512, G))
    c = np.arange(G)
    ang_c = 2.0 * np.pi * ((c[:, None] * c[None, :]) % G) / G
    cs = jnp.asarray(np.concatenate([np.cos(ang_c), np.sin(ang_c)], axis=1), BF16)
    pq = pl.pallas_call(
        _dft_cols_body,
        name="dft_cols",
        grid=(T // tm, width // tn),
        in_specs=[pl.BlockSpec((tm, tn), lambda i, g: (i + ro, g)), pl.BlockSpec((G, 2 * G), lambda i, g: (0, 0))],
        out_specs=pl.BlockSpec((2, tm, tn), lambda i, g: (0, i, g)),
        out_shape=jax.ShapeDtypeStruct((2, T, width), BF16),
        compiler_params=_params("parallel", "parallel"),
    )(z, cs)
    n2 = FFT_N2
    n1 = T // n2
    zp = pq.reshape(2, n2, n1, width).transpose(0, 2, 1, 3)
    t = jnp.arange(n1, dtype=jnp.int32)[:, None, None] + n1 * jnp.arange(n2, dtype=jnp.int32)[None, None, :]
    ang = ((jnp.arange(n2, dtype=jnp.int32)[None, :, None] * t) % T).astype(F32) * (2.0 * math.pi / T)
    ec, es = jnp.cos(ang), jnp.sin(ang)
    e1 = jnp.concatenate([jnp.concatenate([ec, -es], axis=2), jnp.concatenate([-es, -ec], axis=2)], axis=1).astype(BF16)
    w = pl.pallas_call(
        _fft_stage1_body,
        name="fft_stage1",
        grid=(n1,),
        in_specs=[pl.BlockSpec((2, None, n2, width), lambda i: (0, i, 0, 0)),
                  pl.BlockSpec((None, 2 * n2, 2 * n2), lambda i: (i, 0, 0))],
        out_specs=pl.BlockSpec((2, None, n2, width), lambda i: (0, i, 0, 0)),
        out_shape=jax.ShapeDtypeStruct((2, n1, n2, width), BF16),
        compiler_params=_params("parallel"),
    )(zp, e1)
    a1 = np.arange(n1)
    ang1 = 2.0 * np.pi * ((a1[:, None] * a1[None, :]) % n1) / n1
    f1 = jnp.asarray(np.concatenate([np.cos(ang1), np.sin(ang1)], axis=1), BF16)
    scale = 1.0 / math.sqrt(T * G)
    out = _matmul(f1, w.reshape(2 * n1, n2 * width), name="fft_stage2", tm=n1, tn=_pick(n2 * width, (8192, 1024)),
                  tk=2 * n1, out_dtype=BF16, epi=lambda acc, row0, rows, fulls: acc * scale)
    return out.reshape(T, width)


def _rope_body(x_ref, tab_ref, o_ref, *, scale, keep_dup):
    xt = x_ref[...] * tab_ref[...]
    y = xt + pltpu.roll(xt, QK_ROPE, axis=1)
    if not keep_dup:
        lane = lax.broadcasted_iota(jnp.int32, y.shape, 1)
        y = jnp.where(lane < QK_ROPE, y, 0.0)
    o_ref[...] = (y * scale).astype(o_ref.dtype)


def _rope(z, col_off, n_heads, tab, scale):
    T = z.shape[0]
    tm = ROW_BLOCK
    cb = col_off // LANES
    return pl.pallas_call(
        functools.partial(_rope_body, scale=scale, keep_dup=False),
        name="rope",
        grid=(T // tm, n_heads),
        in_specs=[pl.BlockSpec((tm, LANES), lambda i, h: (i, cb + h)), pl.BlockSpec((tm, LANES), lambda i, h: (i, 0))],
        out_specs=pl.BlockSpec((tm, LANES), lambda i, h: (i, h)),
        out_shape=jax.ShapeDtypeStruct((T, n_heads * LANES), BF16),
        compiler_params=_params("parallel", "parallel"),
    )(z, tab)


def _q_final_body(qn_ref, qr_ref, tab_ref, o_ref, *, scale):
    xt = qr_ref[...] * tab_ref[...]
    y = xt + pltpu.roll(xt, QK_ROPE, axis=1)
    lane = lax.broadcasted_iota(jnp.int32, y.shape, 1)
    o_ref[:, :QK_NOPE] = (qn_ref[...] * scale).astype(o_ref.dtype)
    o_ref[:, QK_NOPE:] = (jnp.where(lane < QK_ROPE, y, 0.0) * scale).astype(o_ref.dtype)


def _q_final(q, tab, scale):
    T = q.shape[0]
    tm = _pick(T, (1024, 512, 256))
    blk = lambda off: pl.BlockSpec((tm, LANES), lambda i, h: (i, off + h))
    return pl.pallas_call(
        functools.partial(_q_final_body, scale=scale),
        name="q_final",
        grid=(T // tm, MLA_HEADS),
        in_specs=[blk(0), blk(MLA_HEADS), pl.BlockSpec((tm, LANES), lambda i, h: (i, 0))],
        out_specs=pl.BlockSpec((tm, 2 * LANES), lambda i, h: (i, h)),
        out_shape=jax.ShapeDtypeStruct((T, MLA_HEADS * 2 * LANES), BF16),
        compiler_params=_params("parallel", "parallel"),
    )(q, q, tab)


def _attn_body(q_ref, kn_ref, kr_ref, v_ref, o_ref, kc_ref, vc_ref, *, ts):
    @pl.when(pl.program_id(1) == 0)
    def _():
        kc_ref[:, :QK_NOPE] = kn_ref[...]
        kc_ref[:, QK_NOPE:] = kr_ref[...]
        vc_ref[:, :V_HEAD] = v_ref[...]
        vc_ref[:, V_HEAD:] = jnp.ones(v_ref.shape, BF16)

    q = q_ref[...]
    n_sub = kc_ref.shape[0] // ts

    def logits(c):
        return lax.dot_general(q, kc_ref[c * ts:(c + 1) * ts, :], NT, preferred_element_type=F32)

    def update(c, s, m_old, acc):
        m_new = jnp.maximum(m_old, jnp.max(s, axis=-1, keepdims=True))
        p = jnp.exp2(s - m_new).astype(BF16)
        pv = jnp.dot(p, vc_ref[c * ts:(c + 1) * ts, :], preferred_element_type=F32)
        return m_new, (pv if acc is None else jnp.exp2(m_old - m_new) * acc + pv)

    m = jnp.full((q.shape[0], 1), -jnp.inf, F32)
    acc = None
    s_prev = logits(0)
    for c in range(1, n_sub):
        s_next = logits(c)
        m, acc = update(c - 1, s_prev, m, acc)
        s_prev = s_next
    m, acc = update(n_sub - 1, s_prev, m, acc)
    o_ref[...] = (acc[:, :V_HEAD] / acc[:, V_HEAD:]).astype(o_ref.dtype)


def _attention(q, kv, kr, n_ctx):
    T = kv.shape[0]
    Tq = q.shape[0]
    tq = _pick(Tq, (1024, 512, 256))
    ts = _pick(T, (768, 256))
    kv_spec = lambda off: pl.BlockSpec((T, LANES), lambda h, i: (0, 2 * h + off))
    return pl.pallas_call(
        functools.partial(_attn_body, ts=ts),
        name="mla_attention",
        grid=(MLA_HEADS, Tq // tq),
        in_specs=[pl.BlockSpec((tq, 2 * LANES), lambda h, i: (i, h)), kv_spec(0),
                  pl.BlockSpec((T, LANES), lambda h, i: (0, 0)), kv_spec(1)],
        out_specs=pl.BlockSpec((tq, LANES), lambda h, i: (i, h)),
        out_shape=jax.ShapeDtypeStruct((Tq, MLA_HEADS * V_HEAD), BF16),
        scratch_shapes=[pltpu.VMEM((T, 2 * LANES), BF16), pltpu.VMEM((T, 2 * LANES), BF16)],
        compiler_params=_params("arbitrary", "arbitrary"),
    )(q, kv, kr, kv)


def _moe_up_body(be_ref, x_ref, wg_ref, wu_ref, o_ref, wgb_ref, wub_ref):
    b = pl.program_id(1)
    changed = jnp.logical_or(b == 0, be_ref[b] != be_ref[jnp.maximum(b - 1, 0)])

    @pl.when(changed)
    def _():
        wgb_ref[...] = wg_ref[...].astype(BF16)
        wub_ref[...] = wu_ref[...].astype(BF16)

    used = be_ref[pl.num_programs(1) + b] != 0

    @pl.when(used)
    def _():
        half = x_ref.shape[1]
        x_lo, x_hi = (t.astype(BF16) for t in _unpack_bf16_halves(x_ref[...]))

        def proj(w_ref):
            return (jnp.dot(x_lo, w_ref[:half, :], preferred_element_type=F32)
                    + jnp.dot(x_hi, w_ref[half:, :], preferred_element_type=F32))

        gate, up = proj(wgb_ref), proj(wub_ref)
        o_ref[...] = (gate * jax.nn.sigmoid(gate) * up).astype(o_ref.dtype)

    @pl.when(jnp.logical_not(used))
    def _():
        o_ref[...] = jnp.zeros_like(o_ref)


def _moe_up(xb, block_e, wg, wu, layer):
    n_rows = xb.shape[0]
    D, DE = wg.shape[2:]
    tb, tn = MOE_UP_ROWS, 512
    w_spec = pl.BlockSpec((None, None, D, tn), lambda n, b, be: (layer, be[b], 0, n))
    return pl.pallas_call(
        _moe_up_body,
        name="moe_up",
        grid_spec=pltpu.PrefetchScalarGridSpec(
            num_scalar_prefetch=1,
            grid=(DE // tn, n_rows // tb),
            in_specs=[pl.BlockSpec((tb, D // 2), lambda n, b, be: (b, 0)), w_spec, w_spec],
            out_specs=pl.BlockSpec((tb, tn), lambda n, b, be: (b, n)),
            scratch_shapes=[pltpu.VMEM((D, tn), BF16), pltpu.VMEM((D, tn), BF16)]),
        out_shape=jax.ShapeDtypeStruct((n_rows, DE), BF16),
        compiler_params=_params("arbitrary", "arbitrary"),
    )(block_e, xb, wg, wu)


def _moe_down_body(be_ref, h_ref, wd_ref, o_ref, wdb_ref):
    b = pl.program_id(1)
    changed = jnp.logical_or(b == 0, be_ref[b] != be_ref[jnp.maximum(b - 1, 0)])

    @pl.when(changed)
    def _():
        wdb_ref[...] = wd_ref[...].astype(BF16)

    used = b < be_ref[pl.num_programs(1)]

    @pl.when(used)
    def _():
        o_ref[...] = _pack_bf16_halves(jnp.dot(h_ref[...], wdb_ref[...], preferred_element_type=F32))

    @pl.when(jnp.logical_not(used))
    def _():
        o_ref[...] = jnp.zeros_like(o_ref)


def _moe_down(h, block_e, wd, layer):
    n_rows, DE = h.shape
    D = wd.shape[3]
    tb, tn = MOE_BLOCK, MOE_DOWN_TN
    return pl.pallas_call(
        _moe_down_body,
        name="moe_down",
        grid_spec=pltpu.PrefetchScalarGridSpec(
            num_scalar_prefetch=1,
            grid=(D // tn, n_rows // tb),
            in_specs=[pl.BlockSpec((tb, DE), lambda n, b, be: (b, 0)),
                      pl.BlockSpec((None, None, DE, tn), lambda n, b, be: (layer, be[b], 0, n))],
            out_specs=pl.BlockSpec((tb, tn // 2), lambda n, b, be: (b, n)),
            scratch_shapes=[pltpu.VMEM((DE, tn), BF16)]),
        out_shape=jax.ShapeDtypeStruct((n_rows, D // 2), jnp.uint32),
        compiler_params=_params("arbitrary", "arbitrary"),
    )(block_e, h, wd)


def _shared_up_body(a_ref, wg_ref, wu_ref, o_ref, wgb_ref, wub_ref):
    @pl.when(pl.program_id(1) == 0)
    def _():
        wgb_ref[...] = wg_ref[...].astype(BF16)
        wub_ref[...] = wu_ref[...].astype(BF16)

    a = a_ref[...]
    gate = jnp.dot(a, wgb_ref[...], preferred_element_type=F32)
    up = jnp.dot(a, wub_ref[...], preferred_element_type=F32)
    o_ref[...] = (gate * jax.nn.sigmoid(gate) * up).astype(o_ref.dtype)


def _shared_up(h, sg, su, layer):
    T, D = h.shape
    DE = sg.shape[2]
    tm, tn = _pick(T, (1408, 1024, 768, 256)), 256
    w_spec = pl.BlockSpec((None, D, tn), lambda j, i: (layer, 0, j))
    return pl.pallas_call(
        _shared_up_body,
        name="shared_up",
        grid=(DE // tn, T // tm),
        in_specs=[pl.BlockSpec((tm, D), lambda j, i: (i, 0)), w_spec, w_spec],
        out_specs=pl.BlockSpec((tm, tn), lambda j, i: (i, j)),
        out_shape=jax.ShapeDtypeStruct((T, DE), BF16),
        scratch_shapes=[pltpu.VMEM((D, tn), BF16), pltpu.VMEM((D, tn), BF16)],
        compiler_params=_params("arbitrary", "arbitrary"),
    )(h, sg, su)


def _route(logits, router_b):
    T = logits.shape[0]
    s = jax.nn.sigmoid(logits)
    sel = (s + router_b.astype(F32)).reshape(T, N_GROUPS, EXPERTS_PER_GROUP)

    def top2(v):
        pos = jnp.arange(v.shape[-1])
        i0 = jnp.argmax(v, axis=-1)
        rest = jnp.where(pos == i0[..., None], -jnp.inf, v)
        return i0, jnp.argmax(rest, axis=-1), jnp.max(v, axis=-1), jnp.max(rest, axis=-1)

    _, _, v0, v1 = top2(sel)
    grp = jnp.argmax(v0 + v1, axis=-1)
    in_grp = jnp.arange(N_GROUPS)[None, :, None] == grp[:, None, None]
    loc0, loc1, _, _ = top2(jnp.sum(jnp.where(in_grp, sel, 0.0), axis=1))
    idx = grp[:, None] * EXPERTS_PER_GROUP + jnp.stack([loc0, loc1], axis=-1)
    picked = jnp.arange(N_EXPERTS)[None, None, :] == idx[:, :, None]
    wts = jnp.sum(jnp.where(picked, s[:, None, :], 0.0), axis=-1)
    return idx, wts / jnp.sum(wts, axis=-1, keepdims=True)


def _moe_ffn(x, h, logits, h_packed, router_b, wg, wu, wd, sg, su, sd, layer, gates, n_ctx, final_g=None):
    T, D = h.shape
    DE = sg.shape[2]
    hs = _shared_up(h, sg, su, layer)

    idx, wts = _route(logits[:, :N_EXPERTS], router_b)
    A = T * TOP_K
    flat_e = idx.reshape(A)
    onehot = (flat_e[:, None] == jnp.arange(N_EXPERTS)[None, :]).astype(jnp.int32)
    csum = jnp.cumsum(onehot, axis=0)
    counts = csum[-1]
    rank = jnp.sum(onehot * csum, axis=1) - 1
    padded = (counts + MOE_BLOCK - 1) // MOE_BLOCK * MOE_BLOCK
    pad_end = jnp.cumsum(padded)
    pad_start = pad_end - padded
    dest = (jnp.sum(onehot * pad_start[None, :], axis=1) + rank).astype(jnp.int32)
    n_blocks = -(-A // MOE_BLOCK) + N_EXPERTS
    n_slots = n_blocks * MOE_BLOCK
    slot_tok = (jnp.arange(n_slots, dtype=jnp.int32) % T).at[dest].set(jnp.arange(A, dtype=jnp.int32) // TOP_K)
    xb = h_packed[slot_tok]
    block_e = jnp.minimum(jnp.sum(pad_end[None, :] <= (jnp.arange(n_blocks) * MOE_BLOCK)[:, None], axis=1),
                          N_EXPERTS - 1).astype(jnp.int32)
    sub_e = jnp.repeat(block_e, MOE_BLOCK // MOE_UP_ROWS)
    last = jnp.sum(jnp.where(sub_e[:, None] == jnp.arange(N_EXPERTS)[None, :], (pad_start + counts)[None, :], 0), axis=1)
    sub_used = (jnp.arange(sub_e.shape[0]) * MOE_UP_ROWS < last).astype(jnp.int32)
    hmid = _moe_up(xb, jnp.concatenate([sub_e, sub_used]), wg, wu, layer)
    n_used = (pad_end[-1:] // MOE_BLOCK).astype(jnp.int32)
    y_slots = _moe_down(hmid, jnp.concatenate([block_e, n_used]), wd, layer)
    slot_of = dest.reshape(T, TOP_K)
    routed = [y_slots[slot_of[:, k]] for k in range(TOP_K)]
    wts_cols = [jnp.broadcast_to(wts[:, k:k + 1], (T, LANES)) for k in range(TOP_K)]

    def unpack_tiles(w):
        lo, hi = _unpack_bf16_halves(w)
        tile = MOE_DOWN_TN // 2
        parts = [t[:, s:s + tile] for s in range(0, w.shape[1], tile) for t in (lo, hi)]
        return jnp.concatenate(parts, axis=1)

    def combine_epi(acc, row0, rows, fulls):
        rid = row0 + lax.broadcasted_iota(jnp.int32, acc.shape, 0)
        gate = jnp.where(rid < n_ctx, rows[0][0:1], rows[0][1:2])
        routed_sum = unpack_tiles(fulls[1]) * fulls[3][:, :1] + unpack_tiles(fulls[2]) * fulls[4][:, :1]
        y = fulls[0] + gate * (routed_sum + acc)
        if final_g is not None:
            y = y * lax.rsqrt(jnp.mean(y * y, axis=-1, keepdims=True) + EPS) * rows[1]
        return y

    rows = (gates,) if final_g is None else (gates, final_g.reshape(1, D))
    return _matmul(hs, sd[layer].astype(BF16), name="shared_down_combine", tm=ROW_BLOCK, tn=D, tk=DE,
                   rows=rows, fulls=(x, *routed, *wts_cols), epi=combine_epi)


def _silu(x):
    return x * jax.nn.sigmoid(x)


def _softplus(x):
    return jnp.maximum(x, 0.0) + jnp.log(1.0 + jnp.exp(-jnp.abs(x)))


def _adaln(cond2, w, b, layer):
    D = cond2.shape[1]
    a = jnp.zeros((SUBLANES, D), F32).at[:2].set(cond2)
    out = _matmul(a, w, name="adaln", tm=SUBLANES, tn=1024, tk=D, a_act=_silu, b_lead=layer,
                  rows=(b[layer].reshape(1, -1),), epi=lambda acc, row0, rows, fulls: acc + rows[0])
    return out[:2].reshape(2, -1, D)


def _gated_residual_epi(n_ctx):
    def epi(acc, row0, rows, fulls):
        rid = row0 + lax.broadcasted_iota(jnp.int32, acc.shape, 0)
        gate = jnp.where(rid < n_ctx, rows[0][0:1], rows[0][1:2])
        return fulls[0] + gate * acc
    return epi


def _pad_cols(w, width):
    return jnp.pad(w, ((0, 0), (0, width - w.shape[1])))


def _pad_rows(w, height):
    return jnp.pad(w, ((0, height - w.shape[0]), (0, 0)))


def _even_mixer(x, h, mods, n_ctx, w_in, w_out, w0, w2, a0, a2, g2, k_k, k_a, r_k, ln_x, conv_w):
    T, D = x.shape
    W = D // 2
    o = np.cumsum((0, W, W, W, DECAY_LORA, DECAY_LORA, ICLR_LORA, ICLR_LORA, GATE_LORA, W, W))
    lora = [_pad_cols(w_in[:, o[i]:o[i + 1]], LANES) for i in range(3, 7)]
    w_in_p = jnp.concatenate([w_in[:, :o[3]], w_in[:, o[8]:], *lora, w_in[:, o[7]:o[8]]], axis=1).astype(BF16)
    tm = _pick(T, (1408, 768, 256))
    z = _matmul(h, w_in_p, name="even_w_in", tm=_pick(T, (704, 768, 256)), tn=768, tk=D)
    r_off, k_off, v_off, gb_off, gc_off, u_off = (i * W for i in range(6))
    lo = 6 * W

    def lora_mm(col, kdim, w, bias, act, epi):
        rows = () if bias is None else (bias.reshape(1, W),)
        return _matmul(z, _pad_rows(w, kdim), name="rwkv_lora", tm=tm, tn=1024, tk=kdim, a_col_off=col, a_act=act, rows=rows, epi=epi)

    decay_epi = lambda acc, row0, rows, fulls: -jnp.exp(-_softplus(-(rows[0] + acc)) - 0.5)
    iclr_epi = lambda acc, row0, rows, fulls: jax.nn.sigmoid(rows[0] + acc)
    lw_f = lora_mm(lo, LANES, w2[0], w0[0], jnp.tanh, decay_epi)
    lw_b = lora_mm(lo + LANES, LANES, w2[1], w0[1], jnp.tanh, decay_epi)
    ic_f = lora_mm(lo + 2 * LANES, LANES, a2[0], a0[0], None, iclr_epi)
    ic_b = lora_mm(lo + 3 * LANES, LANES, a2[1], a0[1], None, iclr_epi)
    gate = lora_mm(lo + 4 * LANES, GATE_LORA, g2, None, jax.nn.sigmoid, None)

    y_f, y_b = _rwkv_scan(z, r_off, k_off, v_off, lw_f, ic_f, lw_b, ic_b, k_k, k_a, n_ctx)
    o_rwkv = _rwkv_post(y_f, y_b, z, r_off, k_off, v_off, ic_f, ic_b, gate, k_a, r_k, ln_x)
    o_conv = _short_conv(z, gb_off, gc_off, u_off, conv_w, n_ctx)
    return _matmul(o_rwkv, w_out.astype(BF16), a2=o_conv, name="even_w_out", tm=tm, tn=512, tk=D, rows=(mods[:, 2],), fulls=(x,),
                   epi=_gated_residual_epi(n_ctx))


def _rope_tables(n_ctx, n_lat):
    rows = n_lat // GRID_W
    row = jnp.repeat(jnp.arange(rows), GRID_W)
    col = jnp.tile(jnp.arange(GRID_W), rows)
    pos = jnp.stack([row, col], axis=-1).astype(F32)
    inv_freq = ROPE_BASE ** (-jnp.arange(ROPE_PAIRS, dtype=F32) / ROPE_PAIRS)
    ang = pos[:, :, None, None] * inv_freq
    shape = (n_lat, 2, 2, ROPE_PAIRS)
    cos = jnp.broadcast_to(jnp.cos(ang), shape).reshape(n_lat, QK_ROPE)
    sin = jnp.broadcast_to(jnp.sin(ang), shape).reshape(n_lat, QK_ROPE)
    cos = jnp.concatenate([jnp.ones((n_ctx, QK_ROPE), F32), cos], axis=0)
    sin = jnp.concatenate([jnp.zeros((n_ctx, QK_ROPE), F32), sin], axis=0)
    return jnp.concatenate([cos, sin], axis=1)


def _rot_cols(w):
    lead = w.shape[:-1]
    wr = w.reshape(*lead, 2, 2, ROPE_PAIRS)
    return jnp.stack([-wr[..., 1, :], wr[..., 0, :]], axis=-2).reshape(*lead, QK_ROPE)


def _odd_mixer(x, h, mods, n_ctx, w_in, w_out, q_norm, w_uq, kv_norm, w_ukv):
    T, D = x.shape
    W = D // 2
    n_lat = T - n_ctx
    kr_w = w_in[:, W + Q_LORA + KV_LORA:]
    w_in_p = jnp.concatenate([w_in, _rot_cols(kr_w)], axis=1).astype(BF16)
    w_in_p = _pad_cols(w_in_p, -(-w_in_p.shape[1] // 768) * 768)
    tm = _pick(T, (1408, 768, 256))
    z = _matmul(h, w_in_p, name="odd_w_in", tm=_pick(T, (704, 768, 256)), tn=768, tk=D)
    qa_off, kva_off, kr_off = W, W + Q_LORA, W + Q_LORA + KV_LORA

    qn = _rmsnorm_cols(z, qa_off, Q_LORA, q_norm)
    kvn = _rmsnorm_cols(z, kva_off, KV_LORA, kv_norm)
    uq = w_uq.reshape(Q_LORA, MLA_HEADS, QK_NOPE + QK_ROPE)
    uq_rope = uq[:, :, QK_NOPE:]
    w_uq_p = jnp.concatenate([uq[:, :, :QK_NOPE].reshape(Q_LORA, -1),
                              jnp.concatenate([uq_rope, _rot_cols(uq_rope)], axis=-1).reshape(Q_LORA, -1)], axis=1).astype(BF16)
    tml = _pick(n_lat, (1024, 512, 256))
    q_lat = _matmul(qn[n_ctx:], w_uq_p, name="mla_uq", tm=tml, tn=1024, tk=Q_LORA)
    kv = _matmul(kvn, w_ukv.astype(BF16), name="mla_ukv", tm=tm, tn=1024, tk=KV_LORA, out_dtype=BF16)

    tab = _rope_tables(n_ctx, n_lat)
    q_scale = SM_SCALE * math.log2(math.e)
    tab_lat = tab[n_ctx:]
    q_fin = _q_final(q_lat, tab_lat, q_scale)
    kr = _rope(z, kr_off, 1, tab, 1.0)
    att = _attention(q_fin, kv, kr, n_ctx)
    four = _fourier_mix(z, n_ctx, W)
    return _matmul(four, w_out.astype(BF16), a2=att, name="odd_w_out", tm=tml, tn=512, tk=D, rows=(mods[:, 2],), fulls=(x[n_ctx:],),
                   epi=_gated_residual_epi(0))


def kernel(x, c, ctx, c_ctx, ada_w, ada_b, norm1_g, norm2_g, ev_w_in, ev_w_out, ev_w0, ev_w2, ev_a0, ev_a2, ev_g2, ev_k_k, ev_k_a, ev_r_k, ev_ln_x, ev_conv_w, od_w_in, od_w_out, od_q_norm, od_w_uq, od_kv_norm, od_w_ukv, router_w, router_b, moe_wg, moe_wu, moe_wd, shared_wg, shared_wu, shared_wd, final_g):
    B, n_lat, D = x.shape
    n_ctx = ctx.shape[1]
    depth = ada_w.shape[0]
    assert B == 1 and depth == 2 and n_ctx == ROW_BLOCK
    xs = jnp.concatenate([ctx[0], x[0]], axis=0)
    cond2 = jnp.concatenate([c_ctx[None], c], axis=0)
    router_w_p = _pad_cols(router_w.astype(F32), LANES)

    moe_w = (router_b, moe_wg, moe_wu, moe_wd, shared_wg, shared_wu, shared_wd)
    mods = _adaln(cond2, ada_w, ada_b, 0)
    mods_odd = _adaln(cond2, ada_w, ada_b, 1)
    h = _modulate(xs, norm1_g[0], mods, 0, 1, n_ctx)
    xs = _even_mixer(xs, h, mods, n_ctx, ev_w_in[0], ev_w_out[0], ev_w0[0], ev_w2[0], ev_a0[0], ev_a2[0], ev_g2[0],
                     ev_k_k[0], ev_k_a[0], ev_r_k[0], ev_ln_x[0], ev_conv_w[0])
    h, logits, h_packed = _modulate(xs, norm2_g[0], mods, 3, 4, n_ctx, router_w=router_w_p)
    xs = _moe_ffn(xs, h, logits, h_packed, *moe_w, 0, mods[:, 5], n_ctx)

    mods = mods_odd
    h = _modulate(xs, norm1_g[1], mods, 0, 1, n_ctx)
    xl = _odd_mixer(xs, h, mods, n_ctx, od_w_in[0], od_w_out[0], od_q_norm[0], od_w_uq[0], od_kv_norm[0], od_w_ukv[0])
    h, logits, h_packed = _modulate(xl, norm2_g[1], mods, 3, 4, 0, router_w=router_w_p)
    out = _moe_ffn(xl, h, logits, h_packed, *moe_w, 1, mods[:, 5], 0, final_g=final_g)
    return out[None]
```

```python
import functools
import math

import numpy as np
import jax
import jax.numpy as jnp
from jax import lax
from jax.experimental import pallas as pl
from jax.experimental.pallas import tpu as pltpu

F32 = jnp.float32
BF16 = jnp.bfloat16

LANES = 128
SUBLANES = 8
VMEM_LIMIT_BYTES = 56 * 1024 * 1024

EPS = 1e-6
GN_EPS = 64e-5
RWKV_HEAD = 64
DECAY_LORA = 96
ICLR_LORA = 96
GATE_LORA = 256
CONV_K = 3
FOURIER_GROUP = 128
FFT_N2 = 128
MLA_HEADS = 16
QK_NOPE = 128
QK_ROPE = 64
V_HEAD = 128
Q_LORA = 1024
KV_LORA = 512
ROPE_PAIRS = QK_ROPE // 4
ROPE_BASE = 10000.0
GRID_W = 64
SM_SCALE = (QK_NOPE + QK_ROPE) ** -0.5
N_EXPERTS = 16
N_GROUPS = 4
EXPERTS_PER_GROUP = N_EXPERTS // N_GROUPS
TOP_K = 2
MOE_BLOCK = 512
MOE_UP_ROWS = 256
MOE_DOWN_TN = 2048
ROW_BLOCK = 256
SCAN_CHUNK = 64
SCAN_PAIRS = 4


def _params(*sem):
    return pltpu.CompilerParams(dimension_semantics=sem, vmem_limit_bytes=VMEM_LIMIT_BYTES)


def _pack_bf16_halves(x):
    half = x.shape[1] // 2
    bits = lambda t: lax.bitcast_convert_type(t.astype(BF16).astype(F32), jnp.uint32)
    return (bits(x[:, half:]) & jnp.uint32(0xFFFF0000)) | (bits(x[:, :half]) >> 16)


def _unpack_bf16_halves(w):
    as_f32 = lambda bits: lax.bitcast_convert_type(bits, F32)
    return as_f32(w << 16), as_f32(w & jnp.uint32(0xFFFF0000))


def _pick(n, candidates):
    for c in candidates:
        if n % c == 0:
            return c
    raise ValueError(f"no tile for {n} among {candidates}")


def _mm_body(*refs, nk, a_act, epi, n_rows, n_fulls, tm, two_a):
    a_ref, b_ref = refs[0], refs[1]
    row_refs = refs[2:2 + n_rows]
    full_refs = refs[2 + n_rows:2 + n_rows + n_fulls]
    n_in = 2 + n_rows + n_fulls + int(two_a)
    o_ref, acc_ref = refs[n_in], refs[n_in + 1]
    k = pl.program_id(2)
    av = a_ref[...]
    if a_act is not None:
        av = a_act(av.astype(F32))
    if two_a:
        a2_ref = refs[n_in - 1]
        k1 = a_ref.shape[1]
        part = (jnp.dot(av.astype(BF16), b_ref[:k1, :].astype(BF16), preferred_element_type=F32)
                + jnp.dot(a2_ref[...].astype(BF16), b_ref[k1:, :].astype(BF16), preferred_element_type=F32))
    else:
        part = jnp.dot(av.astype(BF16), b_ref[...].astype(BF16), preferred_element_type=F32)

    def finish(acc):
        if epi is not None:
            row0 = pl.program_id(0) * tm
            acc = epi(acc, row0, [r[...] for r in row_refs], [f[...] for f in full_refs])
        o_ref[...] = acc.astype(o_ref.dtype)

    if nk == 1:
        finish(part)
        return

    @pl.when(k == 0)
    def _():
        acc_ref[...] = part

    @pl.when(jnp.logical_and(k > 0, k < nk - 1))
    def _():
        acc_ref[...] += part

    @pl.when(k == nk - 1)
    def _():
        finish(acc_ref[...] + part)


def _matmul(a, b, *, name, tm, tn, tk, out_dtype=F32, a_col_off=0, a_act=None, a2=None, b_lead=None,
            rows=(), fulls=(), epi=None):
    K, N = b.shape[-2:]
    M = a.shape[0]
    assert M % tm == 0 and N % tn == 0 and K % tk == 0 and a_col_off % tk == 0
    nk = K // tk
    ko = a_col_off // tk
    if b.ndim == 3:
        b_spec = pl.BlockSpec((None, tk, tn), lambda i, j, k: (b_lead, k, j))
    else:
        b_spec = pl.BlockSpec((tk, tn), lambda i, j, k: (k, j))
    if a2 is None:
        a_specs = [pl.BlockSpec((tm, tk), lambda i, j, k: (i, k + ko))]
    else:
        assert nk == 1 and a.shape[1] + a2.shape[1] == K
        a_specs = [pl.BlockSpec((tm, a.shape[1]), lambda i, j, k: (i, 0))]
    in_specs = a_specs + [b_spec]
    for r in rows:
        in_specs.append(pl.BlockSpec((r.shape[0], tn), lambda i, j, k: (0, j)))
    for f in fulls:
        in_specs.append(pl.BlockSpec((tm, f.shape[1] // (N // tn)), lambda i, j, k: (i, j)))
    extra = ()
    if a2 is not None:
        in_specs.append(pl.BlockSpec((tm, a2.shape[1]), lambda i, j, k: (i, 0)))
        extra = (a2,)
    body = functools.partial(_mm_body, nk=nk, a_act=a_act, epi=epi, n_rows=len(rows),
                             n_fulls=len(fulls), tm=tm, two_a=a2 is not None)
    return pl.pallas_call(
        body,
        grid=(M // tm, N // tn, nk),
        in_specs=in_specs,
        out_specs=pl.BlockSpec((tm, tn), lambda i, j, k: (i, j)),
        out_shape=jax.ShapeDtypeStruct((M, N), out_dtype),
        scratch_shapes=[pltpu.VMEM((tm, tn) if nk > 1 else (SUBLANES, LANES), F32)],
        compiler_params=_params("parallel", "parallel", "arbitrary"),
        name=name,
    )(a, b, *rows, *fulls, *extra)


def _modulate_body(x_ref, g_ref, mod_ref, *rest, shift_idx, scale_idx, with_router):
    xv = x_ref[...]
    y = xv * lax.rsqrt(jnp.mean(xv * xv, axis=-1, keepdims=True) + EPS) * g_ref[...]
    h = y * (1.0 + mod_ref[0, scale_idx:scale_idx + 1, :]) + mod_ref[0, shift_idx:shift_idx + 1, :]
    if with_router:
        rw_ref, o_ref, lg_ref, pk_ref = rest
        lg_ref[...] = _pdot(_pieces(h, 2), _pieces(rw_ref[...], 2))
        pk_ref[...] = _pack_bf16_halves(h)
    else:
        (o_ref,) = rest
    o_ref[...] = h.astype(o_ref.dtype)


def _modulate(x, g, mods, shift_idx, scale_idx, n_ctx, router_w=None):
    T, D = x.shape
    tm = ROW_BLOCK
    assert T % tm == 0 and n_ctx % tm == 0
    nc = n_ctx // tm
    in_specs = [
        pl.BlockSpec((tm, D), lambda i: (i, 0)),
        pl.BlockSpec((1, D), lambda i: (0, 0)),
        pl.BlockSpec((1, mods.shape[1], D), lambda i: (jnp.where(i < nc, 0, 1), 0, 0)),
    ]
    out_specs = [pl.BlockSpec((tm, D), lambda i: (i, 0))]
    out_shape = [jax.ShapeDtypeStruct((T, D), BF16)]
    args = [x, g.reshape(1, D), mods]
    if router_w is not None:
        in_specs.append(pl.BlockSpec((D, LANES), lambda i: (0, 0)))
        out_specs += [pl.BlockSpec((tm, LANES), lambda i: (i, 0)), pl.BlockSpec((tm, D // 2), lambda i: (i, 0))]
        out_shape += [jax.ShapeDtypeStruct((T, LANES), F32), jax.ShapeDtypeStruct((T, D // 2), jnp.uint32)]
        args.append(router_w)
    body = functools.partial(_modulate_body, shift_idx=shift_idx, scale_idx=scale_idx,
                             with_router=router_w is not None)
    outs = pl.pallas_call(body, grid=(T // tm,), in_specs=in_specs, out_specs=out_specs,
                          out_shape=out_shape, compiler_params=_params("parallel"), name="modulate")(*args)
    return outs if router_w is not None else outs[0]


def _rmsnorm_cols_body(x_ref, g_ref, o_ref):
    xv = x_ref[...]
    y = xv * lax.rsqrt(jnp.mean(xv * xv, axis=-1, keepdims=True) + EPS) * g_ref[...]
    o_ref[...] = y.astype(o_ref.dtype)


def _rmsnorm_cols(z, col_off, width, g, out_dtype=BF16):
    T = z.shape[0]
    tm = ROW_BLOCK
    assert col_off % width == 0 and T % tm == 0
    cb = col_off // width
    return pl.pallas_call(
        _rmsnorm_cols_body,
        name="rmsnorm_cols",
        grid=(T // tm,),
        in_specs=[pl.BlockSpec((tm, width), lambda i: (i, cb)), pl.BlockSpec((1, width), lambda i: (0, 0))],
        out_specs=pl.BlockSpec((tm, width), lambda i: (i, 0)),
        out_shape=jax.ShapeDtypeStruct((T, width), out_dtype),
        compiler_params=_params("parallel"),
    )(z, g.reshape(1, width))


def _conv_body(gb_ref, gc_ref, u_ref, gcp_ref, up_ref, gcn_ref, un_ref, w_ref, o_ref, *, tb, nb, nc):
    i = pl.program_id(0)
    p = gc_ref[...] * u_ref[...]
    prev_row = (gcp_ref[...] * up_ref[...])[SUBLANES - 1:SUBLANES, :]
    next_row = (gcn_ref[...] * un_ref[...])[0:1, :]
    starts = jnp.logical_or(i == 0, i == nc)
    ends = jnp.logical_or(i == nc - 1, i == nb - 1)
    prev_row = jnp.where(starts, 0.0, prev_row)
    next_row = jnp.where(ends, 0.0, next_row)
    rid = lax.broadcasted_iota(jnp.int32, p.shape, 0)
    xm1 = jnp.where(rid == 0, prev_row, pltpu.roll(p, 1, axis=0))
    xp1 = jnp.where(rid == tb - 1, next_row, pltpu.roll(p, tb - 1, axis=0))
    w = w_ref[...]
    o_ref[...] = (gb_ref[...] * (w[0:1] * xm1 + w[1:2] * p + w[2:3] * xp1)).astype(o_ref.dtype)


def _short_conv(z, gb_off, gc_off, u_off, conv_w, n_ctx):
    T = z.shape[0]
    C = conv_w.shape[1]
    tb, tn = ROW_BLOCK, 512
    nb, nc = T // tb, n_ctx // tb
    hb = tb // SUBLANES
    last_h = T // SUBLANES - 1
    cur = lambda off: pl.BlockSpec((tb, tn), lambda i, j: (i, off // tn + j))
    prv = lambda off: pl.BlockSpec((SUBLANES, tn), lambda i, j: (jnp.maximum(i * hb - 1, 0), off // tn + j))
    nxt = lambda off: pl.BlockSpec((SUBLANES, tn), lambda i, j: (jnp.minimum((i + 1) * hb, last_h), off // tn + j))
    return pl.pallas_call(
        functools.partial(_conv_body, tb=tb, nb=nb, nc=nc),
        name="short_conv",
        grid=(nb, C // tn),
        in_specs=[cur(gb_off), cur(gc_off), cur(u_off), prv(gc_off), prv(u_off), nxt(gc_off), nxt(u_off),
                  pl.BlockSpec((CONV_K, tn), lambda i, j: (0, j))],
        out_specs=pl.BlockSpec((tb, tn), lambda i, j: (i, j)),
        out_shape=jax.ShapeDtypeStruct((T, C), BF16),
        compiler_params=_params("parallel", "parallel"),
    )(z, z, z, z, z, z, z, conv_w)


NN = (((1,), (0,)), ((), ()))
NT = (((1,), (1,)), ((), ()))
TN = (((0,), (0,)), ((), ()))

SCAN_PIECES_GRAM = 1
SCAN_PIECES_INV = 1
SCAN_PIECES_OUT = 1


def _pieces(x, n):
    out = []
    for i in range(n):
        p = x.astype(BF16)
        out.append(p)
        if i + 1 < n:
            x = x - p.astype(F32)
    return out


def _pdot(ap, bp, dims=NN):
    order = max(len(ap), len(bp))
    acc = None
    for i, x in enumerate(ap):
        for j, y in enumerate(bp):
            if i + j < order:
                t = lax.dot_general(x, y, dims, preferred_element_type=F32)
                acc = t if acc is None else acc + t
    return acc


def _pcat(parts, axis):
    return [jnp.concatenate(ps, axis=axis) for ps in zip(*parts)]


def _scan_stages(C, masks):
    n = 2 * C
    reverse, in_h0, row_c, strict, incl, eye, diag_blocks, off_blocks = masks
    pg, pi, po = SCAN_PIECES_GRAM, SCAN_PIECES_INV, SCAN_PIECES_OUT

    def stack(x):
        return jnp.concatenate([jnp.where(in_h0, x, 0.0), jnp.where(in_h0, 0.0, x)], axis=0)

    def s_cum(d):
        for nm in ("lw", "r", "k", "v", "a", "b"):
            d[nm + "_s"] = stack(d[nm])
        x = d["lw"]
        sh = 1
        while sh < C:
            if reverse:
                x = x + jnp.where(row_c < C - sh, pltpu.roll(x, C - sh, axis=0), 0.0)
            else:
                x = x + jnp.where(row_c >= sh, pltpu.roll(x, sh, axis=0), 0.0)
            sh *= 2
        d["cum"] = stack(x)
        d["tot"] = x[0:1] if reverse else x[C - 1:C]

    def s_exp(d):
        cum, tot = d["cum"], d["tot"]
        inv, fin = jnp.exp(-cum), jnp.exp(tot - cum)
        d["r_hat"] = d["r_s"] * jnp.exp(cum)
        d["a_hat_p"] = _pieces(d["a_s"] * jnp.exp(cum - d["lw_s"]), max(pg, po))
        d["r_hat_p"] = _pieces(d["r_hat"], pg)
        d["bk_chk_p"] = _pcat([_pieces(d["b_s"] * inv, pg), _pieces(d["k_s"] * inv, pg)], 0)
        d["b_til_p"] = _pieces(d["b_s"] * fin, po)
        d["bk_til_p"] = _pcat([d["b_til_p"], _pieces(d["k_s"] * fin, po)], 0)
        d["v_p"] = _pieces(d["v_s"], po)
        d["decay"] = jnp.broadcast_to(jnp.exp(tot), (LANES, LANES)).T

    def s_gram(d):
        g = _pdot(_pcat([d["a_hat_p"][:pg], d["r_hat_p"]], 0), d["bk_chk_p"], NT)
        d["n_ab"] = jnp.where(strict, g[:n, :n], 0.0)
        d["m_ak_p"] = _pieces(jnp.where(strict, g[:n, n:], 0.0), po)
        d["m_rb_p"] = _pieces(jnp.where(incl, g[n:, :n], 0.0), po)
        d["m_rk_p"] = _pieces(jnp.where(incl, g[n:, n:], 0.0), po)
        nd = jnp.where(diag_blocks, d["n_ab"], 0.0)
        d["nd_p"] = _pieces(nd, pi)
        d["t"] = eye + nd

    def s_sq1(d):
        d["pw"] = _pdot(d["nd_p"], d["nd_p"])

    def s_ap1(d):
        d["pw_p"] = _pieces(d["pw"], pi)
        d["t"] = d["t"] + _pdot(d["pw_p"], _pieces(d["t"], pi))

    def s_sq2(d):
        d["pw_p"] = _pieces(_pdot(d["pw_p"], d["pw_p"]), pi)

    def s_ap2(d):
        d["t"] = d["t"] + _pdot(d["pw_p"], _pieces(d["t"], pi))

    def s_merge_a(off_mask):
        def f(d):
            d["t_p"] = _pieces(d["t"], pi)
            d["ot_p"] = _pieces(_pdot(_pieces(jnp.where(off_mask, d["n_ab"], 0.0), pi), d["t_p"]), pi)
        return f

    def s_merge_b(d):
        d["t"] = d["t"] + _pdot(d["t_p"], d["ot_p"])

    def s_abar(d):
        d["t_p"] = _pieces(d["t"], po)
        d["abar_p"] = _pieces(_pdot(d["t_p"], d["a_hat_p"][:po]), po)
        d["mv_p"] = _pieces(_pdot(d["m_ak_p"], d["v_p"]), po)

    def s_u0(d):
        d["uv_p"] = _pcat([_pieces(_pdot(d["t_p"], d["mv_p"]), po), d["v_p"]], 0)
        d["rbar_p"] = _pieces(d["r_hat"] + _pdot(d["m_rb_p"], d["abar_p"]), po)
        d["phi_p"] = _pieces(_pdot(d["b_til_p"], d["abar_p"], TN), po)

    def s_out(d):
        d["y0"] = _pdot(_pcat([d["m_rb_p"], d["m_rk_p"]], 1), d["uv_p"])
        d["s0"] = _pdot(d["bk_til_p"], d["uv_p"], TN)

    stages = [s_cum, s_exp, s_gram, s_sq1, s_ap1, s_sq2, s_ap2]
    for off_mask in off_blocks:
        stages += [s_merge_a(off_mask), s_merge_b]
    return stages + [s_abar, s_u0, s_out]


def _scan_apply(d, state):
    C = d["r"].shape[0]
    st_p = _pieces(state, SCAN_PIECES_OUT)
    y = _pdot(d["rbar_p"], st_p) + d["y0"]
    return y[:C] + y[C:], d["decay"] * state + _pdot(d["phi_p"], st_p) + d["s0"]


def _scan_masks(C, reverse):
    n = 2 * C
    lane = lax.broadcasted_iota(jnp.int32, (C, LANES), 1)
    row = lax.broadcasted_iota(jnp.int32, (n, n), 0)
    col = lax.broadcasted_iota(jnp.int32, (n, n), 1)
    same = (row // C) == (col // C)
    before = (col > row) if reverse else (col < row)
    strict = jnp.logical_and(same, before)
    incl = jnp.logical_and(same, jnp.logical_or(before, col == row))
    eye = jnp.where(row == col, 1.0, 0.0)
    row_c = lax.broadcasted_iota(jnp.int32, (C, LANES), 0)
    blk = 8
    diag_blocks = (row // blk) == (col // blk)
    off_blocks = []
    while blk < C:
        off_blocks.append(jnp.logical_and((row // (2 * blk)) == (col // (2 * blk)), (row // blk) != (col // blk)))
        blk *= 2
    return reverse, lane < RWKV_HEAD, row_c, strict, incl, eye, diag_blocks, off_blocks


def _scan_body(rf, kf, vf, lwf, icf, rb, kb, vb, lwb, icb, kk_ref, ka_ref, seg_ref, yf_ref, yb_ref, sf_ref, sb_ref, *, tb):
    @pl.when(pl.program_id(1) == 0)
    def _():
        sf_ref[...] = jnp.zeros_like(sf_ref)
        sb_ref[...] = jnp.zeros_like(sb_ref)

    seg = seg_ref[...].astype(BF16)
    nch = tb // SCAN_CHUNK
    scans = []
    for refs, y_ref, s_ref, reverse in ((rf, kf, vf, lwf, icf), yf_ref, sf_ref, False), ((rb, kb, vb, lwb, icb), yb_ref, sb_ref, True):
        stages = _scan_stages(SCAN_CHUNK, _scan_masks(SCAN_CHUNK, reverse))
        for p in range(SCAN_PAIRS):
            lanes = slice(p * LANES, (p + 1) * LANES)
            r, k, v, lw, ic = (t[:, lanes] for t in refs)
            k_k, k_a = kk_ref[:, lanes], ka_ref[:, lanes]
            kk = k * k_k
            kk = kk * lax.rsqrt(_pdot(_pieces(kk * kk, 3), [seg]) + 1e-12)
            a, b = -kk, kk * ic
            kd = k * (1.0 + (ic - 1.0) * k_a)
            chunks = []
            for c in (range(nch - 1, -1, -1) if reverse else range(nch)):
                sl = slice(c * SCAN_CHUNK, (c + 1) * SCAN_CHUNK)
                chunks.append(dict(r=r[sl], k=kd[sl], v=v[sl], a=a[sl], b=b[sl], lw=lw[sl], rows=sl))
            scans.append((chunks, stages, y_ref, s_ref, p, lanes))

    for step in range(len(scans[0][1])):
        for chunks, stages, *_ in scans:
            for d in chunks:
                stages[step](d)
    states = [s_ref[p] for _, _, _, s_ref, p, _ in scans]
    for c in range(nch):
        for i, (chunks, _, y_ref, _, _, lanes) in enumerate(scans):
            y, states[i] = _scan_apply(chunks[c], states[i])
            y_ref[chunks[c]["rows"], lanes] = y
    for (_, _, _, s_ref, p, _), state in zip(scans, states):
        s_ref[p] = state


def _rwkv_scan(z, r_off, k_off, v_off, lw_f, ic_f, lw_b, ic_b, k_k, k_a, n_ctx):
    T = z.shape[0]
    W = lw_f.shape[1]
    tb = ROW_BLOCK
    assert n_ctx == tb and T % tb == 0
    nb = T // tb
    tw = SCAN_PAIRS * LANES
    fwd = lambda i: i
    bwd = lambda i: jnp.where(i == 0, 0, nb - i)
    zspec = lambda off, o: pl.BlockSpec((tb, tw), lambda p, i: (o(i), off // tw + p))
    wspec = lambda o: pl.BlockSpec((tb, tw), lambda p, i: (o(i), p))
    par = pl.BlockSpec((1, tw), lambda p, i: (0, p))
    lane = np.arange(LANES)
    seg = jnp.asarray((lane[:, None] // RWKV_HEAD == lane[None, :] // RWKV_HEAD).astype(np.float32))
    return pl.pallas_call(
        functools.partial(_scan_body, tb=tb),
        name="rwkv_scan",
        grid=(W // tw, nb),
        in_specs=[zspec(r_off, fwd), zspec(k_off, fwd), zspec(v_off, fwd), wspec(fwd), wspec(fwd),
                  zspec(r_off, bwd), zspec(k_off, bwd), zspec(v_off, bwd), wspec(bwd), wspec(bwd),
                  par, par, pl.BlockSpec((LANES, LANES), lambda p, i: (0, 0))],
        out_specs=[wspec(fwd), wspec(bwd)],
        out_shape=[jax.ShapeDtypeStruct((T, W), F32)] * 2,
        scratch_shapes=[pltpu.VMEM((SCAN_PAIRS, LANES, LANES), F32)] * 2,
        compiler_params=_params("parallel", "arbitrary"),
    )(z, z, z, lw_f, ic_f, z, z, z, lw_b, ic_b, k_k.reshape(1, W), k_a.reshape(1, W), seg)


def _rwkv_post_body(yf, yb, r, k, v, icf, icb, gate, ka, rk, lnx, seg_ref, o_ref):
    seg = [seg_ref[...].astype(BF16)]
    head_sum = lambda t: _pdot(_pieces(t, 3), seg)
    inv_n = 1.0 / RWKV_HEAD
    wkv = yf[...] + yb[...]
    yc = wkv - head_sum(wkv) * inv_n
    yn = yc * lax.rsqrt(head_sum(yc * yc) * inv_n + GN_EPS) * lnx[...]
    kv, kav = k[...], ka[...]
    kd_sum = kv * (1.0 + (icf[...] - 1.0) * kav) + kv * (1.0 + (icb[...] - 1.0) * kav)
    bonus = head_sum(r[...] * kd_sum * rk[...]) * v[...]
    o_ref[...] = ((yn + bonus) * gate[...]).astype(o_ref.dtype)


def _rwkv_post(y_f, y_b, z, r_off, k_off, v_off, ic_f, ic_b, gate, k_a, r_k, ln_x):
    T, W = y_f.shape
    tb, tn = _pick(T, (768, 256)), 256
    blk = pl.BlockSpec((tb, tn), lambda i, j: (i, j))
    zspec = lambda off: pl.BlockSpec((tb, tn), lambda i, j: (i, off // tn + j))
    par = pl.BlockSpec((1, tn), lambda i, j: (0, j))
    lane = np.arange(tn)
    seg = jnp.asarray((lane[:, None] // RWKV_HEAD == lane[None, :] // RWKV_HEAD).astype(np.float32))
    return pl.pallas_call(
        _rwkv_post_body,
        name="rwkv_post",
        grid=(T // tb, W // tn),
        in_specs=[blk, blk, zspec(r_off), zspec(k_off), zspec(v_off), blk, blk, blk, par, par, par,
                  pl.BlockSpec((tn, tn), lambda i, j: (0, 0))],
        out_specs=blk,
        out_shape=jax.ShapeDtypeStruct((T, W), BF16),
        compiler_params=_params("parallel", "parallel"),
    )(y_f, y_b, z, z, z, ic_f, ic_b, gate, k_a.reshape(1, W), r_k.reshape(1, W), ln_x.reshape(1, W), seg)


def _dft_cols_body(u_ref, cs_ref, o_ref):
    G = FOURIER_GROUP
    for g in range(u_ref.shape[1] // G):
        pq = jnp.dot(u_ref[:, g * G:(g + 1) * G].astype(BF16), cs_ref[...], preferred_element_type=F32)
        o_ref[0, :, g * G:(g + 1) * G] = pq[:, :G].astype(o_ref.dtype)
        o_ref[1, :, g * G:(g + 1) * G] = pq[:, G:].astype(o_ref.dtype)


def _fft_stage1_body(z_ref, e_ref, o_ref):
    n2 = z_ref.shape[1]
    w = jnp.dot(e_ref[...], jnp.concatenate([z_ref[0], z_ref[1]], axis=0), preferred_element_type=F32)
    o_ref[0] = w[:n2].astype(o_ref.dtype)
    o_ref[1] = w[n2:].astype(o_ref.dtype)


def _fourier_mix(z, n_ctx, width):
    T = z.shape[0] - n_ctx
    tm = ROW_BLOCK
    ro = n_ctx // tm
    G = FOURIER_GROUP
    tn = _pick(width, (1024, 512, G))
    c = np.arange(G)
    ang_c = 2.0 * np.pi * ((c[:, None] * c[None, :]) % G) / G
    cs = jnp.asarray(np.concatenate([np.cos(ang_c), np.sin(ang_c)], axis=1), BF16)
    pq = pl.pallas_call(
        _dft_cols_body,
        name="dft_cols",
        grid=(T // tm, width // tn),
        in_specs=[pl.BlockSpec((tm, tn), lambda i, g: (i + ro, g)), pl.BlockSpec((G, 2 * G), lambda i, g: (0, 0))],
        out_specs=pl.BlockSpec((2, tm, tn), lambda i, g: (0, i, g)),
        out_shape=jax.ShapeDtypeStruct((2, T, width), BF16),
        compiler_params=_params("parallel", "parallel"),
    )(z, cs)
    n2 = FFT_N2
    n1 = T // n2
    zp = pq.reshape(2, n2, n1, width).transpose(0, 2, 1, 3)
    t = jnp.arange(n1, dtype=jnp.int32)[:, None, None] + n1 * jnp.arange(n2, dtype=jnp.int32)[None, None, :]
    ang = ((jnp.arange(n2, dtype=jnp.int32)[None, :, None] * t) % T).astype(F32) * (2.0 * math.pi / T)
    ec, es = jnp.cos(ang), jnp.sin(ang)
    e1 = jnp.concatenate([jnp.concatenate([ec, -es], axis=2), jnp.concatenate([-es, -ec], axis=2)], axis=1).astype(BF16)
    w = pl.pallas_call(
        _fft_stage1_body,
        name="fft_stage1",
        grid=(n1,),
        in_specs=[pl.BlockSpec((2, None, n2, width), lambda i: (0, i, 0, 0)),
                  pl.BlockSpec((None, 2 * n2, 2 * n2), lambda i: (i, 0, 0))],
        out_specs=pl.BlockSpec((2, None, n2, width), lambda i: (0, i, 0, 0)),
        out_shape=jax.ShapeDtypeStruct((2, n1, n2, width), BF16),
        compiler_params=_params("parallel"),
    )(zp, e1)
    a1 = np.arange(n1)
    ang1 = 2.0 * np.pi * ((a1[:, None] * a1[None, :]) % n1) / n1
    f1 = jnp.asarray(np.concatenate([np.cos(ang1), np.sin(ang1)], axis=1), BF16)
    scale = 1.0 / math.sqrt(T * G)
    out = _matmul(f1, w.reshape(2 * n1, n2 * width), name="fft_stage2", tm=n1, tn=_pick(n2 * width, (8192, 1024)),
                  tk=2 * n1, out_dtype=BF16, epi=lambda acc, row0, rows, fulls: acc * scale)
    return out.reshape(T, width)


def _rope_body(x_ref, tab_ref, o_ref):
    xt = x_ref[...] * tab_ref[...]
    y = xt + pltpu.roll(xt, QK_ROPE, axis=1)
    lane = lax.broadcasted_iota(jnp.int32, y.shape, 1)
    o_ref[...] = jnp.where(lane < QK_ROPE, y, 0.0).astype(o_ref.dtype)


def _rope(z, col_off, tab):
    T = z.shape[0]
    tm = ROW_BLOCK
    cb = col_off // LANES
    return pl.pallas_call(
        _rope_body,
        name="rope",
        grid=(T // tm,),
        in_specs=[pl.BlockSpec((tm, LANES), lambda i: (i, cb)), pl.BlockSpec((tm, LANES), lambda i: (i, 0))],
        out_specs=pl.BlockSpec((tm, LANES), lambda i: (i, 0)),
        out_shape=jax.ShapeDtypeStruct((T, LANES), BF16),
        compiler_params=_params("parallel"),
    )(z, tab)


def _q_final_body(qn_ref, qr_ref, tab_ref, o_ref, *, scale):
    xt = qr_ref[...] * tab_ref[...]
    y = xt + pltpu.roll(xt, QK_ROPE, axis=1)
    lane = lax.broadcasted_iota(jnp.int32, y.shape, 1)
    o_ref[:, :QK_NOPE] = (qn_ref[...] * scale).astype(o_ref.dtype)
    o_ref[:, QK_NOPE:] = (jnp.where(lane < QK_ROPE, y, 0.0) * scale).astype(o_ref.dtype)


def _q_final(q, tab, scale):
    T = q.shape[0]
    tm = _pick(T, (1024, 512, 256))
    blk = lambda off: pl.BlockSpec((tm, LANES), lambda i, h: (i, off + h))
    return pl.pallas_call(
        functools.partial(_q_final_body, scale=scale),
        name="q_final",
        grid=(T // tm, MLA_HEADS),
        in_specs=[blk(0), blk(MLA_HEADS), pl.BlockSpec((tm, LANES), lambda i, h: (i, 0))],
        out_specs=pl.BlockSpec((tm, 2 * LANES), lambda i, h: (i, h)),
        out_shape=jax.ShapeDtypeStruct((T, MLA_HEADS * 2 * LANES), BF16),
        compiler_params=_params("parallel", "parallel"),
    )(q, q, tab)


def _attn_body(q_ref, kn_ref, kr_ref, v_ref, o_ref, kc_ref, vc_ref, *, ts):
    @pl.when(pl.program_id(1) == 0)
    def _():
        kc_ref[:, :QK_NOPE] = kn_ref[...]
        kc_ref[:, QK_NOPE:] = kr_ref[...]
        vc_ref[:, :V_HEAD] = v_ref[...]
        vc_ref[:, V_HEAD:] = jnp.ones(v_ref.shape, BF16)

    q = q_ref[...]
    n_sub = kc_ref.shape[0] // ts

    def logits(c):
        return lax.dot_general(q, kc_ref[c * ts:(c + 1) * ts, :], NT, preferred_element_type=F32)

    def update(c, s, m_old, acc):
        m_new = jnp.maximum(m_old, jnp.max(s, axis=-1, keepdims=True))
        p = jnp.exp2(s - m_new).astype(BF16)
        pv = jnp.dot(p, vc_ref[c * ts:(c + 1) * ts, :], preferred_element_type=F32)
        return m_new, (pv if acc is None else jnp.exp2(m_old - m_new) * acc + pv)

    m = jnp.full((q.shape[0], 1), -jnp.inf, F32)
    acc = None
    s_prev = logits(0)
    for c in range(1, n_sub):
        s_next = logits(c)
        m, acc = update(c - 1, s_prev, m, acc)
        s_prev = s_next
    m, acc = update(n_sub - 1, s_prev, m, acc)
    o_ref[...] = (acc[:, :V_HEAD] / acc[:, V_HEAD:]).astype(o_ref.dtype)


def _attention(q, kv, kr):
    T = kv.shape[0]
    Tq = q.shape[0]
    tq = _pick(Tq, (1024, 512, 256))
    ts = _pick(T, (768, 256))
    kv_spec = lambda off: pl.BlockSpec((T, LANES), lambda h, i: (0, 2 * h + off))
    return pl.pallas_call(
        functools.partial(_attn_body, ts=ts),
        name="mla_attention",
        grid=(MLA_HEADS, Tq // tq),
        in_specs=[pl.BlockSpec((tq, 2 * LANES), lambda h, i: (i, h)), kv_spec(0),
                  pl.BlockSpec((T, LANES), lambda h, i: (0, 0)), kv_spec(1)],
        out_specs=pl.BlockSpec((tq, LANES), lambda h, i: (i, h)),
        out_shape=jax.ShapeDtypeStruct((Tq, MLA_HEADS * V_HEAD), BF16),
        scratch_shapes=[pltpu.VMEM((T, 2 * LANES), BF16), pltpu.VMEM((T, 2 * LANES), BF16)],
        compiler_params=_params("arbitrary", "arbitrary"),
    )(q, kv, kr, kv)


def _moe_up_body(be_ref, x_ref, wg_ref, wu_ref, o_ref, wgb_ref, wub_ref):
    b = pl.program_id(1)
    changed = jnp.logical_or(b == 0, be_ref[b] != be_ref[jnp.maximum(b - 1, 0)])

    @pl.when(changed)
    def _():
        wgb_ref[...] = wg_ref[...].astype(BF16)
        wub_ref[...] = wu_ref[...].astype(BF16)

    used = be_ref[pl.num_programs(1) + b] != 0

    @pl.when(used)
    def _():
        half = x_ref.shape[1]
        x_lo, x_hi = (t.astype(BF16) for t in _unpack_bf16_halves(x_ref[...]))

        def proj(w_ref):
            return (jnp.dot(x_lo, w_ref[:half, :], preferred_element_type=F32)
                    + jnp.dot(x_hi, w_ref[half:, :], preferred_element_type=F32))

        gate, up = proj(wgb_ref), proj(wub_ref)
        o_ref[...] = (gate * jax.nn.sigmoid(gate) * up).astype(o_ref.dtype)

    @pl.when(jnp.logical_not(used))
    def _():
        o_ref[...] = jnp.zeros_like(o_ref)


def _moe_up(xb, block_e, wg, wu, layer):
    n_rows = xb.shape[0]
    D, DE = wg.shape[2:]
    tb, tn = MOE_UP_ROWS, 512
    w_spec = pl.BlockSpec((None, None, D, tn), lambda n, b, be: (layer, be[b], 0, n))
    return pl.pallas_call(
        _moe_up_body,
        name="moe_up",
        grid_spec=pltpu.PrefetchScalarGridSpec(
            num_scalar_prefetch=1,
            grid=(DE // tn, n_rows // tb),
            in_specs=[pl.BlockSpec((tb, D // 2), lambda n, b, be: (b, 0)), w_spec, w_spec],
            out_specs=pl.BlockSpec((tb, tn), lambda n, b, be: (b, n)),
            scratch_shapes=[pltpu.VMEM((D, tn), BF16), pltpu.VMEM((D, tn), BF16)]),
        out_shape=jax.ShapeDtypeStruct((n_rows, DE), BF16),
        compiler_params=_params("arbitrary", "arbitrary"),
    )(block_e, xb, wg, wu)


def _moe_down_body(be_ref, h_ref, wd_ref, o_ref, wdb_ref):
    b = pl.program_id(1)
    changed = jnp.logical_or(b == 0, be_ref[b] != be_ref[jnp.maximum(b - 1, 0)])

    @pl.when(changed)
    def _():
        wdb_ref[...] = wd_ref[...].astype(BF16)

    used = b < be_ref[pl.num_programs(1)]

    @pl.when(used)
    def _():
        o_ref[...] = _pack_bf16_halves(jnp.dot(h_ref[...], wdb_ref[...], preferred_element_type=F32))

    @pl.when(jnp.logical_not(used))
    def _():
        o_ref[...] = jnp.zeros_like(o_ref)


def _moe_down(h, block_e, wd, layer):
    n_rows, DE = h.shape
    D = wd.shape[3]
    tb, tn = MOE_BLOCK, MOE_DOWN_TN
    return pl.pallas_call(
        _moe_down_body,
        name="moe_down",
        grid_spec=pltpu.PrefetchScalarGridSpec(
            num_scalar_prefetch=1,
            grid=(D // tn, n_rows // tb),
            in_specs=[pl.BlockSpec((tb, DE), lambda n, b, be: (b, 0)),
                      pl.BlockSpec((None, None, DE, tn), lambda n, b, be: (layer, be[b], 0, n))],
            out_specs=pl.BlockSpec((tb, tn // 2), lambda n, b, be: (b, n)),
            scratch_shapes=[pltpu.VMEM((DE, tn), BF16)]),
        out_shape=jax.ShapeDtypeStruct((n_rows, D // 2), jnp.uint32),
        compiler_params=_params("arbitrary", "arbitrary"),
    )(block_e, h, wd)


def _shared_up_body(a_ref, wg_ref, wu_ref, o_ref, wgb_ref, wub_ref):
    @pl.when(pl.program_id(1) == 0)
    def _():
        wgb_ref[...] = wg_ref[...].astype(BF16)
        wub_ref[...] = wu_ref[...].astype(BF16)

    a = a_ref[...]
    gate = jnp.dot(a, wgb_ref[...], preferred_element_type=F32)
    up = jnp.dot(a, wub_ref[...], preferred_element_type=F32)
    o_ref[...] = (gate * jax.nn.sigmoid(gate) * up).astype(o_ref.dtype)


def _shared_up(h, sg, su, layer):
    T, D = h.shape
    DE = sg.shape[2]
    tm, tn = _pick(T, (1408, 1024, 768, 256)), 256
    w_spec = pl.BlockSpec((None, D, tn), lambda j, i: (layer, 0, j))
    return pl.pallas_call(
        _shared_up_body,
        name="shared_up",
        grid=(DE // tn, T // tm),
        in_specs=[pl.BlockSpec((tm, D), lambda j, i: (i, 0)), w_spec, w_spec],
        out_specs=pl.BlockSpec((tm, tn), lambda j, i: (i, j)),
        out_shape=jax.ShapeDtypeStruct((T, DE), BF16),
        scratch_shapes=[pltpu.VMEM((D, tn), BF16), pltpu.VMEM((D, tn), BF16)],
        compiler_params=_params("arbitrary", "arbitrary"),
    )(h, sg, su)


def _route(logits, router_b):
    T = logits.shape[0]
    s = jax.nn.sigmoid(logits)
    sel = (s + router_b.astype(F32)).reshape(T, N_GROUPS, EXPERTS_PER_GROUP)

    def top2(v):
        pos = jnp.arange(v.shape[-1])
        i0 = jnp.argmax(v, axis=-1)
        rest = jnp.where(pos == i0[..., None], -jnp.inf, v)
        return i0, jnp.argmax(rest, axis=-1), jnp.max(v, axis=-1), jnp.max(rest, axis=-1)

    _, _, v0, v1 = top2(sel)
    grp = jnp.argmax(v0 + v1, axis=-1)
    in_grp = jnp.arange(N_GROUPS)[None, :, None] == grp[:, None, None]
    loc0, loc1, _, _ = top2(jnp.sum(jnp.where(in_grp, sel, 0.0), axis=1))
    idx = grp[:, None] * EXPERTS_PER_GROUP + jnp.stack([loc0, loc1], axis=-1)
    picked = jnp.arange(N_EXPERTS)[None, None, :] == idx[:, :, None]
    wts = jnp.sum(jnp.where(picked, s[:, None, :], 0.0), axis=-1)
    return idx, wts / jnp.sum(wts, axis=-1, keepdims=True)


def _moe_ffn(x, h, logits, h_packed, router_b, wg, wu, wd, sg, su, sd, layer, gates, n_ctx, final_g=None):
    T, D = h.shape
    DE = sg.shape[2]
    hs = _shared_up(h, sg, su, layer)

    idx, wts = _route(logits[:, :N_EXPERTS], router_b)
    A = T * TOP_K
    flat_e = idx.reshape(A)
    onehot = (flat_e[:, None] == jnp.arange(N_EXPERTS)[None, :]).astype(jnp.int32)
    csum = jnp.cumsum(onehot, axis=0)
    counts = csum[-1]
    rank = jnp.sum(onehot * csum, axis=1) - 1
    padded = (counts + MOE_BLOCK - 1) // MOE_BLOCK * MOE_BLOCK
    pad_end = jnp.cumsum(padded)
    pad_start = pad_end - padded
    dest = (jnp.sum(onehot * pad_start[None, :], axis=1) + rank).astype(jnp.int32)
    n_blocks = -(-A // MOE_BLOCK) + N_EXPERTS
    n_slots = n_blocks * MOE_BLOCK
    slot_tok = (jnp.arange(n_slots, dtype=jnp.int32) % T).at[dest].set(jnp.arange(A, dtype=jnp.int32) // TOP_K)
    xb = h_packed[slot_tok]
    block_e = jnp.minimum(jnp.sum(pad_end[None, :] <= (jnp.arange(n_blocks) * MOE_BLOCK)[:, None], axis=1),
                          N_EXPERTS - 1).astype(jnp.int32)
    sub_e = jnp.repeat(block_e, MOE_BLOCK // MOE_UP_ROWS)
    last = jnp.sum(jnp.where(sub_e[:, None] == jnp.arange(N_EXPERTS)[None, :], (pad_start + counts)[None, :], 0), axis=1)
    sub_used = (jnp.arange(sub_e.shape[0]) * MOE_UP_ROWS < last).astype(jnp.int32)
    hmid = _moe_up(xb, jnp.concatenate([sub_e, sub_used]), wg, wu, layer)
    n_used = (pad_end[-1:] // MOE_BLOCK).astype(jnp.int32)
    y_slots = _moe_down(hmid, jnp.concatenate([block_e, n_used]), wd, layer)
    slot_of = dest.reshape(T, TOP_K)
    routed = [y_slots[slot_of[:, k]] for k in range(TOP_K)]
    wts_cols = [jnp.broadcast_to(wts[:, k:k + 1], (T, LANES)) for k in range(TOP_K)]

    def unpack_tiles(w):
        lo, hi = _unpack_bf16_halves(w)
        tile = MOE_DOWN_TN // 2
        parts = [t[:, s:s + tile] for s in range(0, w.shape[1], tile) for t in (lo, hi)]
        return jnp.concatenate(parts, axis=1)

    def combine_epi(acc, row0, rows, fulls):
        rid = row0 + lax.broadcasted_iota(jnp.int32, acc.shape, 0)
        gate = jnp.where(rid < n_ctx, rows[0][0:1], rows[0][1:2])
        routed_sum = unpack_tiles(fulls[1]) * fulls[3][:, :1] + unpack_tiles(fulls[2]) * fulls[4][:, :1]
        y = fulls[0] + gate * (routed_sum + acc)
        if final_g is not None:
            y = y * lax.rsqrt(jnp.mean(y * y, axis=-1, keepdims=True) + EPS) * rows[1]
        return y

    rows = (gates,) if final_g is None else (gates, final_g.reshape(1, D))
    return _matmul(hs, sd[layer].astype(BF16), name="shared_down_combine", tm=ROW_BLOCK, tn=D, tk=DE,
                   rows=rows, fulls=(x, *routed, *wts_cols), epi=combine_epi)


def _silu(x):
    return x * jax.nn.sigmoid(x)


def _softplus(x):
    return jnp.maximum(x, 0.0) + jnp.log(1.0 + jnp.exp(-jnp.abs(x)))


def _adaln(cond2, w, b, layer):
    D = cond2.shape[1]
    a = jnp.zeros((SUBLANES, D), F32).at[:2].set(cond2)
    out = _matmul(a, w, name="adaln", tm=SUBLANES, tn=1024, tk=D, a_act=_silu, b_lead=layer,
                  rows=(b[layer].reshape(1, -1),), epi=lambda acc, row0, rows, fulls: acc + rows[0])
    return out[:2].reshape(2, -1, D)


def _gated_residual_epi(n_ctx):
    def epi(acc, row0, rows, fulls):
        rid = row0 + lax.broadcasted_iota(jnp.int32, acc.shape, 0)
        gate = jnp.where(rid < n_ctx, rows[0][0:1], rows[0][1:2])
        return fulls[0] + gate * acc
    return epi


def _pad_cols(w, width):
    return jnp.pad(w, ((0, 0), (0, width - w.shape[1])))


def _pad_rows(w, height):
    return jnp.pad(w, ((0, height - w.shape[0]), (0, 0)))


def _even_mixer(x, h, mods, n_ctx, w_in, w_out, w0, w2, a0, a2, g2, k_k, k_a, r_k, ln_x, conv_w):
    T, D = x.shape
    W = D // 2
    o = np.cumsum((0, W, W, W, DECAY_LORA, DECAY_LORA, ICLR_LORA, ICLR_LORA, GATE_LORA, W, W))
    lora = [_pad_cols(w_in[:, o[i]:o[i + 1]], LANES) for i in range(3, 7)]
    w_in_p = jnp.concatenate([w_in[:, :o[3]], w_in[:, o[8]:], *lora, w_in[:, o[7]:o[8]]], axis=1).astype(BF16)
    tm = _pick(T, (1408, 768, 256))
    z = _matmul(h, w_in_p, name="even_w_in", tm=_pick(T, (704, 768, 256)), tn=768, tk=D)
    r_off, k_off, v_off, gb_off, gc_off, u_off = (i * W for i in range(6))
    lo = 6 * W

    def lora_mm(col, kdim, w, bias, act, epi):
        rows = () if bias is None else (bias.reshape(1, W),)
        return _matmul(z, _pad_rows(w, kdim), name="rwkv_lora", tm=tm, tn=1024, tk=kdim, a_col_off=col, a_act=act, rows=rows, epi=epi)

    decay_epi = lambda acc, row0, rows, fulls: -jnp.exp(-_softplus(-(rows[0] + acc)) - 0.5)
    iclr_epi = lambda acc, row0, rows, fulls: jax.nn.sigmoid(rows[0] + acc)
    lw_f = lora_mm(lo, LANES, w2[0], w0[0], jnp.tanh, decay_epi)
    lw_b = lora_mm(lo + LANES, LANES, w2[1], w0[1], jnp.tanh, decay_epi)
    ic_f = lora_mm(lo + 2 * LANES, LANES, a2[0], a0[0], None, iclr_epi)
    ic_b = lora_mm(lo + 3 * LANES, LANES, a2[1], a0[1], None, iclr_epi)
    gate = lora_mm(lo + 4 * LANES, GATE_LORA, g2, None, jax.nn.sigmoid, None)

    y_f, y_b = _rwkv_scan(z, r_off, k_off, v_off, lw_f, ic_f, lw_b, ic_b, k_k, k_a, n_ctx)
    o_rwkv = _rwkv_post(y_f, y_b, z, r_off, k_off, v_off, ic_f, ic_b, gate, k_a, r_k, ln_x)
    o_conv = _short_conv(z, gb_off, gc_off, u_off, conv_w, n_ctx)
    return _matmul(o_rwkv, w_out.astype(BF16), a2=o_conv, name="even_w_out", tm=tm, tn=512, tk=D, rows=(mods[:, 2],), fulls=(x,),
                   epi=_gated_residual_epi(n_ctx))


def _rope_tables(n_ctx, n_lat):
    rows = n_lat // GRID_W
    row = jnp.repeat(jnp.arange(rows), GRID_W)
    col = jnp.tile(jnp.arange(GRID_W), rows)
    pos = jnp.stack([row, col], axis=-1).astype(F32)
    inv_freq = ROPE_BASE ** (-jnp.arange(ROPE_PAIRS, dtype=F32) / ROPE_PAIRS)
    ang = pos[:, :, None, None] * inv_freq
    shape = (n_lat, 2, 2, ROPE_PAIRS)
    cos = jnp.broadcast_to(jnp.cos(ang), shape).reshape(n_lat, QK_ROPE)
    sin = jnp.broadcast_to(jnp.sin(ang), shape).reshape(n_lat, QK_ROPE)
    cos = jnp.concatenate([jnp.ones((n_ctx, QK_ROPE), F32), cos], axis=0)
    sin = jnp.concatenate([jnp.zeros((n_ctx, QK_ROPE), F32), sin], axis=0)
    return jnp.concatenate([cos, sin], axis=1)


def _rot_cols(w):
    lead = w.shape[:-1]
    wr = w.reshape(*lead, 2, 2, ROPE_PAIRS)
    return jnp.stack([-wr[..., 1, :], wr[..., 0, :]], axis=-2).reshape(*lead, QK_ROPE)


def _odd_mixer(x, h, mods, n_ctx, w_in, w_out, q_norm, w_uq, kv_norm, w_ukv):
    T, D = x.shape
    W = D // 2
    n_lat = T - n_ctx
    kr_w = w_in[:, W + Q_LORA + KV_LORA:]
    w_in_p = jnp.concatenate([w_in, _rot_cols(kr_w)], axis=1).astype(BF16)
    w_in_p = _pad_cols(w_in_p, -(-w_in_p.shape[1] // 768) * 768)
    tm = _pick(T, (1408, 768, 256))
    z = _matmul(h, w_in_p, name="odd_w_in", tm=_pick(T, (704, 768, 256)), tn=768, tk=D)
    qa_off, kva_off, kr_off = W, W + Q_LORA, W + Q_LORA + KV_LORA

    qn = _rmsnorm_cols(z, qa_off, Q_LORA, q_norm)
    kvn = _rmsnorm_cols(z, kva_off, KV_LORA, kv_norm)
    uq = w_uq.reshape(Q_LORA, MLA_HEADS, QK_NOPE + QK_ROPE)
    uq_rope = uq[:, :, QK_NOPE:]
    w_uq_p = jnp.concatenate([uq[:, :, :QK_NOPE].reshape(Q_LORA, -1),
                              jnp.concatenate([uq_rope, _rot_cols(uq_rope)], axis=-1).reshape(Q_LORA, -1)], axis=1).astype(BF16)
    tml = _pick(n_lat, (1024, 512, 256))
    q_lat = _matmul(qn[n_ctx:], w_uq_p, name="mla_uq", tm=tml, tn=1024, tk=Q_LORA)
    kv = _matmul(kvn, w_ukv.astype(BF16), name="mla_ukv", tm=tm, tn=1024, tk=KV_LORA, out_dtype=BF16)

    tab = _rope_tables(n_ctx, n_lat)
    q_scale = SM_SCALE * math.log2(math.e)
    tab_lat = tab[n_ctx:]
    q_fin = _q_final(q_lat, tab_lat, q_scale)
    kr = _rope(z, kr_off, tab)
    att = _attention(q_fin, kv, kr)
    four = _fourier_mix(z, n_ctx, W)
    return _matmul(four, w_out.astype(BF16), a2=att, name="odd_w_out", tm=tml, tn=512, tk=D, rows=(mods[:, 2],), fulls=(x[n_ctx:],),
                   epi=_gated_residual_epi(0))


def kernel(x, c, ctx, c_ctx, ada_w, ada_b, norm1_g, norm2_g, ev_w_in, ev_w_out, ev_w0, ev_w2, ev_a0, ev_a2, ev_g2, ev_k_k, ev_k_a, ev_r_k, ev_ln_x, ev_conv_w, od_w_in, od_w_out, od_q_norm, od_w_uq, od_kv_norm, od_w_ukv, router_w, router_b, moe_wg, moe_wu, moe_wd, shared_wg, shared_wu, shared_wd, final_g):
    B, n_lat, D = x.shape
    n_ctx = ctx.shape[1]
    depth = ada_w.shape[0]
    assert B == 1 and depth == 2 and n_ctx == ROW_BLOCK
    xs = jnp.concatenate([ctx[0], x[0]], axis=0)
    cond2 = jnp.concatenate([c_ctx[None], c], axis=0)
    router_w_p = _pad_cols(router_w.astype(F32), LANES)

    moe_w = (router_b, moe_wg, moe_wu, moe_wd, shared_wg, shared_wu, shared_wd)
    mods = _adaln(cond2, ada_w, ada_b, 0)
    mods_odd = _adaln(cond2, ada_w, ada_b, 1)
    h = _modulate(xs, norm1_g[0], mods, 0, 1, n_ctx)
    xs = _even_mixer(xs, h, mods, n_ctx, ev_w_in[0], ev_w_out[0], ev_w0[0], ev_w2[0], ev_a0[0], ev_a2[0], ev_g2[0],
                     ev_k_k[0], ev_k_a[0], ev_r_k[0], ev_ln_x[0], ev_conv_w[0])
    h, logits, h_packed = _modulate(xs, norm2_g[0], mods, 3, 4, n_ctx, router_w=router_w_p)
    xs = _moe_ffn(xs, h, logits, h_packed, *moe_w, 0, mods[:, 5], n_ctx)

    mods = mods_odd
    h = _modulate(xs, norm1_g[1], mods, 0, 1, n_ctx)
    xl = _odd_mixer(xs, h, mods, n_ctx, od_w_in[0], od_w_out[0], od_q_norm[0], od_w_uq[0], od_kv_norm[0], od_w_ukv[0])
    h, logits, h_packed = _modulate(xl, norm2_g[1], mods, 3, 4, 0, router_w=router_w_p)
    out = _moe_ffn(xl, h, logits, h_packed, *moe_w, 1, mods[:, 5], 0, final_g=final_g)
    return out[None]
```
